```python
import jax
import jax.numpy as jnp
from jax import lax
import numpy as np


D_MODEL = 2048
BATCH = 1
SEQ = 8192
DEPTH = 1

GRID_W = 64
CTX_LEN = 256
HEAD_DIM = 128
N_Q_HEADS = 8
N_KV_HEADS = 2
Q_PER_KV = N_Q_HEADS // N_KV_HEADS
ATTN_WIDTH = N_Q_HEADS * HEAD_DIM
KV_WIDTH = N_KV_HEADS * HEAD_DIM
CONV_WIDTH = D_MODEL - ATTN_WIDTH
CONV_KSIZE = 31
WINDOW = 128
BLOCK = 128
ROPE_THETA = 10000.0
ROPE_AXIS_DIM = HEAD_DIM // 2
N_EXPERTS = 16
EC_CAPACITY = 2
D_FF = 5632
N_MOD = 6
LN_EPS = 1e-5
NEG_INF = -1e30
DEEPNORM_ALPHA = (2.0 * DEPTH) ** 0.25
DEEPNORM_BETA = (8.0 * DEPTH) ** -0.25
CONV_G_START = CONV_WIDTH
Q_START = 2 * CONV_WIDTH
K_START = Q_START + ATTN_WIDTH
V_START = K_START + KV_WIDTH
IN_COLS = V_START + KV_WIDTH

kernel_name = 'hybrid_conv_swa_ecmoe_diffusion_block'


def layer_norm(x, g=None, b=None):
    xf = x.astype(jnp.float32)
    mu = jnp.mean(xf, axis=-1, keepdims=True)
    var = jnp.mean(jnp.square(xf - mu), axis=-1, keepdims=True)
    y = (xf - mu) * lax.rsqrt(var + LN_EPS)
    if g is not None:
        y = y * g.astype(jnp.float32) + b.astype(jnp.float32)
    return y.astype(x.dtype)


def modulate(x, shift, scale):
    return layer_norm(x) * (1.0 + scale) + shift


def axial_rope_tables(row, col):
    inv = ROPE_THETA ** (-jnp.arange(0, ROPE_AXIS_DIM, 2, dtype=jnp.float32) / ROPE_AXIS_DIM)
    ang_r = row[:, None] * inv[None, :]
    ang_c = col[:, None] * inv[None, :]
    expand = lambda a: a[None, :, None, :]
    return (expand(jnp.cos(ang_r)), expand(jnp.sin(ang_r)), expand(jnp.cos(ang_c)), expand(jnp.sin(ang_c)))


def rotate_half_rope(x, cos, sin):
    x1, x2 = jnp.split(x, 2, axis=-1)
    return jnp.concatenate([x1 * cos - x2 * sin, x2 * cos + x1 * sin], axis=-1)


def apply_axial_rope(x, tables):
    cos_r, sin_r, cos_c, sin_c = tables
    xf = x.astype(jnp.float32)
    xr = rotate_half_rope(xf[..., :ROPE_AXIS_DIM], cos_r, sin_r)
    xc = rotate_half_rope(xf[..., ROPE_AXIS_DIM:], cos_c, sin_c)
    return jnp.concatenate([xr, xc], axis=-1).astype(x.dtype)


def split_proj(p, n_batch, n):
    v_conv = p[..., :CONV_G_START]
    g_conv = p[..., CONV_G_START:Q_START]
    q = p[..., Q_START:K_START].reshape(n_batch, n, N_Q_HEADS, HEAD_DIM)
    k = p[..., K_START:V_START].reshape(n_batch, n, N_KV_HEADS, HEAD_DIM)
    v = p[..., V_START:].reshape(n_batch, n, N_KV_HEADS, HEAD_DIM)
    return v_conv, g_conv, q, k, v


def conv_group(v_conv, g_conv, w_dw, b_dw, ln_g, ln_b):
    u = v_conv * jax.nn.sigmoid(g_conv)
    u = lax.conv_general_dilated(
        u, w_dw[:, None, :].astype(u.dtype), window_strides=(1,),
        padding=[(CONV_KSIZE // 2, CONV_KSIZE // 2)],
        dimension_numbers=('NWC', 'WIO', 'NWC'),
        feature_group_count=u.shape[-1]) + b_dw
    u = layer_norm(u, ln_g, ln_b)
    return jax.nn.silu(u)


def window_attention(q, k, v, kc, vc, sink):
    b, n = q.shape[0], q.shape[1]
    nb = n // BLOCK
    scale = HEAD_DIM ** -0.5
    qb = q.reshape(b, nb, BLOCK, N_KV_HEADS, Q_PER_KV, HEAD_DIM)
    pad = ((0, 0), (BLOCK, BLOCK), (0, 0), (0, 0))
    kp = jnp.pad(k, pad).reshape(b, nb + 2, BLOCK, N_KV_HEADS, HEAD_DIM)
    vp = jnp.pad(v, pad).reshape(b, nb + 2, BLOCK, N_KV_HEADS, HEAD_DIM)
    kb = jnp.concatenate([kp[:, :-2], kp[:, 1:-1], kp[:, 2:]], axis=2)
    vb = jnp.concatenate([vp[:, :-2], vp[:, 1:-1], vp[:, 2:]], axis=2)
    s_win = jnp.einsum('bnqkgd,bnmkd->bnkgqm', qb, kb).astype(jnp.float32) * scale
    s_ctx = jnp.einsum('bnqkgd,bckd->bnkgqc', qb, kc).astype(jnp.float32) * scale
    qi = jnp.arange(BLOCK)[:, None]
    m = jnp.arange(3 * BLOCK)[None, :]
    kpos = jnp.arange(nb)[:, None, None] * BLOCK - BLOCK + m[None]
    valid = (jnp.abs(m - qi - BLOCK) <= WINDOW)[None] & (kpos >= 0) & (kpos < n)
    s_win = jnp.where(valid[None, :, None, None], s_win, NEG_INF)
    s_sink = jnp.broadcast_to(
        sink.astype(jnp.float32).reshape(N_KV_HEADS, Q_PER_KV)[None, None, :, :, None, None],
        s_win.shape[:-1] + (1,))
    p = jax.nn.softmax(jnp.concatenate([s_ctx, s_win, s_sink], axis=-1), axis=-1)
    n_ctx = kc.shape[1]
    p_ctx = p[..., :n_ctx].astype(v.dtype)
    p_win = p[..., n_ctx:n_ctx + 3 * BLOCK].astype(v.dtype)
    o = (jnp.einsum('bnkgqc,bckd->bnqkgd', p_ctx, vc)
         + jnp.einsum('bnkgqm,bnmkd->bnqkgd', p_win, vb))
    return o.reshape(b, n, ATTN_WIDTH)


def context_attention(q, k, v, sink):
    b, n_ctx = q.shape[0], q.shape[1]
    qg = q.reshape(b, n_ctx, N_KV_HEADS, Q_PER_KV, HEAD_DIM)
    s = jnp.einsum('bqkgd,bckd->bkgqc', qg, k).astype(jnp.float32) * HEAD_DIM ** -0.5
    s_sink = jnp.broadcast_to(
        sink.astype(jnp.float32).reshape(N_KV_HEADS, Q_PER_KV)[None, :, :, None, None],
        s.shape[:-1] + (1,))
    p = jax.nn.softmax(jnp.concatenate([s, s_sink], axis=-1), axis=-1)
    o = jnp.einsum('bkgqc,bckd->bqkgd', p[..., :n_ctx].astype(v.dtype), v)
    return o.reshape(b, n_ctx, ATTN_WIDTH)


def expert_choice_ffn(h, w_router, w_gate, w_up, w_down):
    b, n, _ = h.shape
    cap = EC_CAPACITY * n // N_EXPERTS
    aff = jax.nn.softmax(jnp.einsum('bnd,de->bne', h, w_router).astype(jnp.float32), axis=-1)
    g, idx = lax.top_k(jnp.transpose(aff, (0, 2, 1)), cap)
    bidx = jnp.arange(b)[:, None, None]
    xs = h[bidx, idx]
    a = jnp.einsum('becd,edf->becf', xs, w_gate)
    u = jnp.einsum('becd,edf->becf', xs, w_up)
    y = jnp.einsum('becf,efd->becd', jax.nn.silu(a) * u, w_down) * g[..., None].astype(h.dtype)
    return jnp.zeros_like(h).at[bidx, idx].add(y)


def setup_inputs(seed: int = 0) -> dict:
    key = jax.random.key(seed)
    ks = jax.random.split(key, 24)
    nrm = lambda k, shape, s: jax.random.normal(k, shape, jnp.float32) * s
    d = D_MODEL
    return {
        'x': nrm(ks[0], (BATCH, SEQ, d), 1.0),
        'c': nrm(ks[1], (BATCH, d), 1.0),
        'ctx': nrm(ks[2], (BATCH, CTX_LEN, d), 1.0),
        'c_ctx': nrm(ks[3], (d,), 1.0),
        'w_mod': nrm(ks[4], (DEPTH, d, N_MOD * d), 0.5 * d ** -0.5),
        'b_mod': nrm(ks[5], (DEPTH, N_MOD * d), 0.02),
        'w_in': nrm(ks[6], (DEPTH, d, IN_COLS), d ** -0.5),
        'b_in': nrm(ks[7], (DEPTH, IN_COLS), 0.02),
        'w_dw': nrm(ks[8], (DEPTH, CONV_KSIZE, CONV_WIDTH), CONV_KSIZE ** -0.5),
        'b_dw': nrm(ks[9], (DEPTH, CONV_WIDTH), 0.02),
        'conv_ln_g': 1.0 + nrm(ks[10], (DEPTH, CONV_WIDTH), 0.02),
        'conv_ln_b': nrm(ks[11], (DEPTH, CONV_WIDTH), 0.02),
        'sink': nrm(ks[12], (DEPTH, N_Q_HEADS), 0.5),
        'w_out': nrm(ks[13], (DEPTH, d, d), DEEPNORM_BETA * d ** -0.5),
        'b_out': nrm(ks[14], (DEPTH, d), 0.02),
        'ln1_g': 1.0 + nrm(ks[15], (DEPTH, d), 0.02),
        'ln1_b': nrm(ks[16], (DEPTH, d), 0.02),
        'w_router': nrm(ks[17], (DEPTH, d, N_EXPERTS), d ** -0.5),
        'w_gate': nrm(ks[18], (DEPTH, N_EXPERTS, d, D_FF), d ** -0.5),
        'w_up': nrm(ks[19], (DEPTH, N_EXPERTS, d, D_FF), d ** -0.5),
        'w_down': nrm(ks[20], (DEPTH, N_EXPERTS, D_FF, d), DEEPNORM_BETA * D_FF ** -0.5),
        'ln2_g': 1.0 + nrm(ks[21], (DEPTH, d), 0.02),
        'ln2_b': nrm(ks[22], (DEPTH, d), 0.02),
    }


def reference(x, c, ctx, c_ctx, w_mod, b_mod, w_in, b_in, w_dw, b_dw, conv_ln_g, conv_ln_b,
              sink, w_out, b_out, ln1_g, ln1_b, w_router, w_gate, w_up, w_down, ln2_g, ln2_b):
    b, n, _ = x.shape
    n_ctx = ctx.shape[1]
    rows = n // GRID_W
    row = jnp.repeat(jnp.arange(rows, dtype=jnp.float32), GRID_W)
    col = jnp.tile(jnp.arange(GRID_W, dtype=jnp.float32), rows)
    rope = axial_rope_tables(row, col)
    for l in range(DEPTH):
        last = l == DEPTH - 1
        mod = (jax.nn.silu(c) @ w_mod[l] + b_mod[l]).reshape(b, N_MOD, 1, D_MODEL)
        mod_c = (jax.nn.silu(c_ctx) @ w_mod[l] + b_mod[l]).reshape(1, N_MOD, 1, D_MODEL)
        h = modulate(x, mod[:, 0], mod[:, 1])
        hc = modulate(ctx, mod_c[:, 0], mod_c[:, 1])
        v_conv, g_conv, q, k, v = split_proj(h @ w_in[l] + b_in[l], b, n)
        q = apply_axial_rope(q, rope)
        k = apply_axial_rope(k, rope)
        if last:
            pc = hc @ w_in[l][:, K_START:] + b_in[l][K_START:]
            kc = pc[..., :KV_WIDTH].reshape(b, n_ctx, N_KV_HEADS, HEAD_DIM)
            vc = pc[..., KV_WIDTH:].reshape(b, n_ctx, N_KV_HEADS, HEAD_DIM)
        else:
            v_conv_c, g_conv_c, qc, kc, vc = split_proj(hc @ w_in[l] + b_in[l], b, n_ctx)
        a_conv = conv_group(v_conv, g_conv, w_dw[l], b_dw[l], conv_ln_g[l], conv_ln_b[l])
        a_attn = window_attention(q, k, v, kc, vc, sink[l])
        mix = jnp.concatenate([a_conv, a_attn], axis=-1) @ w_out[l] + b_out[l]
        x_mid = layer_norm(DEEPNORM_ALPHA * x + mod[:, 2] * mix, ln1_g[l], ln1_b[l])
        h2 = modulate(x_mid, mod[:, 3], mod[:, 4])
        ffn = expert_choice_ffn(h2, w_router[l], w_gate[l], w_up[l], w_down[l])
        x_next = layer_norm(DEEPNORM_ALPHA * x_mid + mod[:, 5] * ffn, ln2_g[l], ln2_b[l])
        if not last:
            a_conv_c = conv_group(v_conv_c, g_conv_c, w_dw[l], b_dw[l], conv_ln_g[l], conv_ln_b[l])
            a_attn_c = context_attention(qc, kc, vc, sink[l])
            mix_c = jnp.concatenate([a_conv_c, a_attn_c], axis=-1) @ w_out[l] + b_out[l]
            ctx_mid = layer_norm(DEEPNORM_ALPHA * ctx + mod_c[:, 2] * mix_c, ln1_g[l], ln1_b[l])
            h2c = modulate(ctx_mid, mod_c[:, 3], mod_c[:, 4])
            ffn_c = expert_choice_ffn(h2c, w_router[l], w_gate[l], w_up[l], w_down[l])
            ctx = layer_norm(DEEPNORM_ALPHA * ctx_mid + mod_c[:, 5] * ffn_c, ln2_g[l], ln2_b[l])
        x = x_next
    return x
```

```python
import functools

import jax
import jax.numpy as jnp
from jax import lax
from jax.experimental import pallas as pl
from jax.experimental.pallas import tpu as pltpu

D_MODEL = 2048
SEQ = 8192
GRID_W = 64
CTX_LEN = 256
HEAD_DIM = 128
N_Q_HEADS = 8
N_KV_HEADS = 2
Q_PER_KV = N_Q_HEADS // N_KV_HEADS
ATTN_WIDTH = N_Q_HEADS * HEAD_DIM
KV_WIDTH = N_KV_HEADS * HEAD_DIM
CONV_WIDTH = D_MODEL - ATTN_WIDTH
CONV_KSIZE = 31
WINDOW = 128
ROPE_THETA = 10000.0
ROPE_AXIS_DIM = HEAD_DIM // 2
N_EXPERTS = 16
EC_CAPACITY = 2
CAP = EC_CAPACITY * SEQ // N_EXPERTS
D_FF = 5632
N_MOD = 6
LN_EPS = 1e-5
NEG_INF = -1e30
DEPTH = 1
DEEPNORM_ALPHA = (2.0 * DEPTH) ** 0.25
Q_START = 2 * CONV_WIDTH
K_START = Q_START + ATTN_WIDTH
V_START = K_START + KV_WIDTH
IN_COLS = V_START + KV_WIDTH

LANES = 128
SUBLANES = 8
H2_HALVES = D_MODEL // (SUBLANES * LANES)
CHUNK = LANES
N_CHUNKS = SEQ // CHUNK
VMEM_LIMIT = 56 * 1024 * 1024

F32 = jnp.float32
BF16 = jnp.bfloat16


def _ln(xv):
    mu = jnp.mean(xv, axis=-1, keepdims=True)
    xc = xv - mu
    var = jnp.mean(xc * xc, axis=-1, keepdims=True)
    return xc * lax.rsqrt(var + LN_EPS)


def _cparams(sem, vmem=VMEM_LIMIT):
    return pltpu.CompilerParams(dimension_semantics=sem, vmem_limit_bytes=vmem)


MOD_TN = 512


def _mod_kernel(ct_ref, w_ref, b_ref, o_ref, s0_ref, s1_ref):
    @pl.when(pl.program_id(0) == 0)
    def _():
        ct = ct_ref[...]
        s = ct * jax.nn.sigmoid(ct)
        s0_ref[...] = jnp.broadcast_to(s[:, 0:1], (D_MODEL, LANES))
        s1_ref[...] = jnp.broadcast_to(s[:, 1:2], (D_MODEL, LANES))

    for j in range(MOD_TN // LANES):
        w = w_ref[:, j * LANES:(j + 1) * LANES]
        r0 = jnp.sum(w * s0_ref[...], axis=0, keepdims=True)
        r1 = jnp.sum(w * s1_ref[...], axis=0, keepdims=True)
        b = b_ref[:, j * LANES:(j + 1) * LANES]
        o_ref[0:1, j * LANES:(j + 1) * LANES] = r0 + b
        o_ref[1:2, j * LANES:(j + 1) * LANES] = r1 + b


def _mod(ct, w_mod, b_mod):
    n_out = N_MOD * D_MODEL
    return pl.pallas_call(
        _mod_kernel,
        grid=(n_out // MOD_TN,),
        in_specs=[
            pl.BlockSpec((D_MODEL, 2), lambda j: (0, 0)),
            pl.BlockSpec((D_MODEL, MOD_TN), lambda j: (0, j)),
            pl.BlockSpec((1, MOD_TN), lambda j: (0, j)),
        ],
        out_specs=pl.BlockSpec((2, MOD_TN), lambda j: (0, j)),
        out_shape=jax.ShapeDtypeStruct((2, n_out), F32),
        scratch_shapes=[pltpu.VMEM((D_MODEL, LANES), F32), pltpu.VMEM((D_MODEL, LANES), F32)],
        compiler_params=_cparams(("arbitrary",)),
        name="mod",
    )(ct, w_mod, b_mod)


IN_TM = 512
IN_TN = 512


def _rope(p, cos, sina, sinb):
    return (p * cos + pltpu.roll(p, HEAD_DIM - ROPE_AXIS_DIM // 2, axis=1) * sina
            + pltpu.roll(p, ROPE_AXIS_DIM // 2, axis=1) * sinb)


def _in_kernel(x_ref, mod_ref, w_ref, b_ref, cos_ref, sina_ref, sinb_ref,
               u_ref, q_ref, k_ref, v_ref):
    shift = mod_ref[0:1, :]
    scale = mod_ref[1:2, :]
    h = (_ln(x_ref[...]) * (1.0 + scale) + shift).astype(BF16)
    cos = cos_ref[...]
    sina = sina_ref[...]
    sinb = sinb_ref[...]

    def proj(c0, width):
        return (jnp.dot(h, w_ref[:, c0:c0 + width], preferred_element_type=F32)
                + b_ref[:, c0:c0 + width])

    for j in range(CONV_WIDTH // IN_TN):
        pv = proj(j * IN_TN, IN_TN)
        pg = proj(CONV_WIDTH + j * IN_TN, IN_TN)
        u_ref[:, j * IN_TN:(j + 1) * IN_TN] = pv * jax.nn.sigmoid(pg)
    for j in range(ATTN_WIDTH // IN_TN):
        pq = proj(Q_START + j * IN_TN, IN_TN)
        for hh in range(IN_TN // HEAD_DIM):
            c0 = j * IN_TN + hh * HEAD_DIM
            q_ref[:, c0:c0 + HEAD_DIM] = _rope(
                pq[:, hh * HEAD_DIM:(hh + 1) * HEAD_DIM], cos, sina, sinb).astype(BF16)
    pk = proj(K_START, KV_WIDTH)
    for hh in range(N_KV_HEADS):
        k_ref[:, hh * HEAD_DIM:(hh + 1) * HEAD_DIM] = _rope(
            pk[:, hh * HEAD_DIM:(hh + 1) * HEAD_DIM], cos, sina, sinb).astype(BF16)
    v_ref[...] = proj(V_START, KV_WIDTH).astype(BF16)


def _in_proj(x2, mod6, w_in_bf, b_in, cos, sina, sinb):
    n = x2.shape[0]
    row = lambda i: (i, 0)
    fixed = lambda i: (0, 0)
    return pl.pallas_call(
        _in_kernel,
        grid=(n // IN_TM,),
        in_specs=[
            pl.BlockSpec((IN_TM, D_MODEL), row),
            pl.BlockSpec((N_MOD, D_MODEL), fixed),
            pl.BlockSpec((D_MODEL, IN_COLS), fixed),
            pl.BlockSpec((1, IN_COLS), fixed),
            pl.BlockSpec((IN_TM, HEAD_DIM), row),
            pl.BlockSpec((IN_TM, HEAD_DIM), row),
            pl.BlockSpec((IN_TM, HEAD_DIM), row),
        ],
        out_specs=[
            pl.BlockSpec((IN_TM, CONV_WIDTH), row),
            pl.BlockSpec((IN_TM, ATTN_WIDTH), row),
            pl.BlockSpec((IN_TM, KV_WIDTH), row),
            pl.BlockSpec((IN_TM, KV_WIDTH), row),
        ],
        out_shape=[
            jax.ShapeDtypeStruct((n, CONV_WIDTH), F32),
            jax.ShapeDtypeStruct((n, ATTN_WIDTH), BF16),
            jax.ShapeDtypeStruct((n, KV_WIDTH), BF16),
            jax.ShapeDtypeStruct((n, KV_WIDTH), BF16),
        ],
        compiler_params=_cparams(("arbitrary",)),
        name="in_proj",
    )(x2, mod6, w_in_bf, b_in, cos, sina, sinb)


def _ctx_kernel(x_ref, mod_ref, w_ref, b_ref, kc_ref, vc_ref):
    shift = mod_ref[0:1, :]
    scale = mod_ref[1:2, :]
    h = (_ln(x_ref[...]) * (1.0 + scale) + shift).astype(BF16)
    p = jnp.dot(h, w_ref[...], preferred_element_type=F32) + b_ref[...]
    kc_ref[...] = p[:, :KV_WIDTH].astype(BF16)
    vc_ref[...] = p[:, KV_WIDTH:].astype(BF16)


def _ctx_kv(ctx2, modc6, w_in_bf, b_in):
    kvw = 2 * KV_WIDTH
    fixed = lambda i: (0, 0)
    return pl.pallas_call(
        _ctx_kernel,
        grid=(1,),
        in_specs=[
            pl.BlockSpec((CTX_LEN, D_MODEL), fixed),
            pl.BlockSpec((N_MOD, D_MODEL), fixed),
            pl.BlockSpec((D_MODEL, kvw), lambda i: (0, K_START // kvw)),
            pl.BlockSpec((1, kvw), lambda i: (0, K_START // kvw)),
        ],
        out_specs=[pl.BlockSpec((CTX_LEN, KV_WIDTH), fixed), pl.BlockSpec((CTX_LEN, KV_WIDTH), fixed)],
        out_shape=[jax.ShapeDtypeStruct((CTX_LEN, KV_WIDTH), BF16),
                   jax.ShapeDtypeStruct((CTX_LEN, KV_WIDTH), BF16)],
        compiler_params=_cparams(("arbitrary",)),
        name="ctx_kv",
    )(ctx2, modc6, w_in_bf, b_in)


CONV_T = 512
CONV_HALO = 16
CONV_ROWS = 32


def _conv_kernel(up_ref, uc_ref, un_ref, w_ref, bdw_ref, g_ref, b_ref, o_ref, buf_ref, acc_ref):
    i = pl.program_id(0)
    last = pl.num_programs(0) - 1
    buf_ref[0:CONV_HALO, :] = jnp.where(i > 0, up_ref[...], 0.0)
    buf_ref[CONV_HALO:CONV_HALO + CONV_T, :] = uc_ref[...]
    buf_ref[CONV_HALO + CONV_T:, :] = jnp.where(i < last, un_ref[...], 0.0)
    off = CONV_HALO - CONV_KSIZE // 2

    span = CONV_ROWS + 2 * CONV_HALO

    def body(r, carry):
        r0 = pl.multiple_of(r * CONV_ROWS, CONV_ROWS)
        for lg in range(CONV_WIDTH // LANES):
            ls = slice(lg * LANES, (lg + 1) * LANES)
            blk = buf_ref[pl.ds(r0, span), ls]
            shifted = [blk] + [pltpu.roll(blk, span - s, axis=0) for s in range(1, SUBLANES)]
            acc = jnp.zeros((CONV_ROWS, LANES), F32)
            for t in range(CONV_KSIZE):
                d = off + t
                a0 = (d // SUBLANES) * SUBLANES
                acc = acc + shifted[d % SUBLANES][a0:a0 + CONV_ROWS, :] * w_ref[t:t + 1, ls]
            acc_ref[pl.ds(r0, CONV_ROWS), ls] = acc
        return carry

    lax.fori_loop(0, CONV_T // CONV_ROWS, body, 0)
    y = _ln(acc_ref[...] + bdw_ref[...]) * g_ref[...] + b_ref[...]
    o_ref[...] = (y * jax.nn.sigmoid(y)).astype(BF16)


def _conv(u, w_dw, b_dw, ln_g, ln_b):
    n = u.shape[0]
    hb = CONV_T // CONV_HALO
    nhb = n // CONV_HALO
    fixed = lambda i: (0, 0)
    return pl.pallas_call(
        _conv_kernel,
        grid=(n // CONV_T,),
        in_specs=[
            pl.BlockSpec((CONV_HALO, CONV_WIDTH), lambda i: (jnp.maximum(i * hb - 1, 0), 0)),
            pl.BlockSpec((CONV_T, CONV_WIDTH), lambda i: (i, 0)),
            pl.BlockSpec((CONV_HALO, CONV_WIDTH), lambda i: (jnp.minimum((i + 1) * hb, nhb - 1), 0)),
            pl.BlockSpec((CONV_KSIZE, CONV_WIDTH), fixed),
            pl.BlockSpec((1, CONV_WIDTH), fixed),
            pl.BlockSpec((1, CONV_WIDTH), fixed),
            pl.BlockSpec((1, CONV_WIDTH), fixed),
        ],
        out_specs=pl.BlockSpec((CONV_T, CONV_WIDTH), lambda i: (i, 0)),
        out_shape=jax.ShapeDtypeStruct((n, CONV_WIDTH), BF16),
        scratch_shapes=[pltpu.VMEM((CONV_T + 2 * CONV_HALO, CONV_WIDTH), F32),
                        pltpu.VMEM((CONV_T, CONV_WIDTH), F32)],
        compiler_params=_cparams(("arbitrary",)),
        name="conv",
    )(u, u, u, w_dw, b_dw, ln_g, ln_b)


ATT_T = 128


def _attn_kernel(sink_ref, q_ref, kp_ref, kc_ref, kn_ref, vp_ref, vc_ref, vn_ref, kx_ref, vx_ref, o_ref):
    i = pl.program_id(0)
    scale = HEAD_DIM ** -0.5
    rows = Q_PER_KV * ATT_T
    qi = lax.broadcasted_iota(jnp.int32, (rows, 3 * ATT_T), 0) & (ATT_T - 1)
    m = lax.broadcasted_iota(jnp.int32, (rows, 3 * ATT_T), 1)
    kpos = (i - 1) * ATT_T + m
    valid = (jnp.abs(m - qi - ATT_T) <= WINDOW) & (kpos >= 0) & (kpos < SEQ)
    hrow = jnp.right_shift(lax.broadcasted_iota(jnp.int32, (rows, 1), 0), ATT_T.bit_length() - 1)
    nt = (((1,), (1,)), ((), ()))
    for g in range(N_KV_HEADS):
        ls = slice(g * HEAD_DIM, (g + 1) * HEAD_DIM)
        qs = jnp.concatenate(
            [q_ref[:, (g * Q_PER_KV + hh) * HEAD_DIM:(g * Q_PER_KV + hh + 1) * HEAD_DIM]
             for hh in range(Q_PER_KV)], axis=0)
        kw = jnp.concatenate([kp_ref[:, ls], kc_ref[:, ls], kn_ref[:, ls]], axis=0)
        vw = jnp.concatenate([vp_ref[:, ls], vc_ref[:, ls], vn_ref[:, ls]], axis=0)
        s_win = lax.dot_general(qs, kw, nt, preferred_element_type=F32) * scale
        s_win = jnp.where(valid, s_win, NEG_INF)
        s_ctx = lax.dot_general(qs, kx_ref[:, ls], nt, preferred_element_type=F32) * scale
        s_sink = jnp.zeros((rows, 1), F32)
        for hh in range(Q_PER_KV):
            s_sink = jnp.where(hrow == hh, sink_ref[g * Q_PER_KV + hh], s_sink)
        mx = jnp.maximum(jnp.maximum(jnp.max(s_win, axis=-1, keepdims=True),
                                     jnp.max(s_ctx, axis=-1, keepdims=True)), s_sink)
        e_win = jnp.exp(s_win - mx)
        e_ctx = jnp.exp(s_ctx - mx)
        den = (jnp.sum(e_win, axis=-1, keepdims=True) + jnp.sum(e_ctx, axis=-1, keepdims=True)
               + jnp.exp(s_sink - mx))
        inv = 1.0 / den
        o = (jnp.dot((e_ctx * inv).astype(BF16), vx_ref[:, ls], preferred_element_type=F32)
             + jnp.dot((e_win * inv).astype(BF16), vw, preferred_element_type=F32))
        for hh in range(Q_PER_KV):
            c0 = (g * Q_PER_KV + hh) * HEAD_DIM
            o_ref[:, c0:c0 + HEAD_DIM] = o[hh * ATT_T:(hh + 1) * ATT_T, :].astype(BF16)


def _attn(sink, q, k, v, kx, vx):
    n = q.shape[0]
    nb = n // ATT_T
    prev = lambda i, s: (jnp.maximum(i - 1, 0), 0)
    cur = lambda i, s: (i, 0)
    nxt = lambda i, s: (jnp.minimum(i + 1, nb - 1), 0)
    fixed = lambda i, s: (0, 0)
    kvspec = lambda im: pl.BlockSpec((ATT_T, KV_WIDTH), im)
    return pl.pallas_call(
        _attn_kernel,
        grid_spec=pltpu.PrefetchScalarGridSpec(
            num_scalar_prefetch=1,
            grid=(nb,),
            in_specs=[
                pl.BlockSpec((ATT_T, ATTN_WIDTH), cur),
                kvspec(prev), kvspec(cur), kvspec(nxt),
                kvspec(prev), kvspec(cur), kvspec(nxt),
                pl.BlockSpec((CTX_LEN, KV_WIDTH), fixed),
                pl.BlockSpec((CTX_LEN, KV_WIDTH), fixed),
            ],
            out_specs=pl.BlockSpec((ATT_T, ATTN_WIDTH), cur),
        ),
        out_shape=jax.ShapeDtypeStruct((n, ATTN_WIDTH), BF16),
        compiler_params=_cparams(("arbitrary",)),
        name="attn",
    )(sink, q, k, k, k, v, v, v, kx, vx)


OUT_TM = 512


def _out_kernel(ac_ref, aa_ref, x_ref, mod_ref, w_ref, b_ref, g1_ref, b1_ref, wr_ref,
                xmid_ref, h2_ref, aff_ref):
    mix = (jnp.dot(ac_ref[...], w_ref[:CONV_WIDTH, :], preferred_element_type=F32)
           + jnp.dot(aa_ref[...], w_ref[CONV_WIDTH:, :], preferred_element_type=F32) + b_ref[...])
    gate1 = mod_ref[2:3, :]
    xmid = _ln(DEEPNORM_ALPHA * x_ref[...] + gate1 * mix) * g1_ref[...] + b1_ref[...]
    xmid_ref[...] = xmid
    h2 = _ln(xmid) * (1.0 + mod_ref[4:5, :]) + mod_ref[3:4, :]
    for kk in range(D_MODEL // LANES):
        hh, jj = divmod(kk, SUBLANES)
        h2_ref[hh, pl.ds(jj, OUT_TM, stride=SUBLANES), :] = h2[:, kk * LANES:(kk + 1) * LANES]
    logits = lax.dot_general(wr_ref[...], h2, (((1,), (1,)), ((), ())),
                             precision=lax.Precision.HIGHEST, preferred_element_type=F32)
    mx = jnp.max(logits, axis=0, keepdims=True)
    ex = jnp.exp(logits - mx)
    aff_ref[...] = ex / jnp.sum(ex, axis=0, keepdims=True)


def _out_proj(a_conv, a_attn, x2, mod6, w_out_bf, b_out, g1, b1, w_router_t):
    n = x2.shape[0]
    row = lambda i: (i, 0)
    fixed = lambda i: (0, 0)
    return pl.pallas_call(
        _out_kernel,
        grid=(n // OUT_TM,),
        in_specs=[
            pl.BlockSpec((OUT_TM, CONV_WIDTH), row),
            pl.BlockSpec((OUT_TM, ATTN_WIDTH), row),
            pl.BlockSpec((OUT_TM, D_MODEL), row),
            pl.BlockSpec((N_MOD, D_MODEL), fixed),
            pl.BlockSpec((D_MODEL, D_MODEL), fixed),
            pl.BlockSpec((1, D_MODEL), fixed),
            pl.BlockSpec((1, D_MODEL), fixed),
            pl.BlockSpec((1, D_MODEL), fixed),
            pl.BlockSpec((N_EXPERTS, D_MODEL), fixed),
        ],
        out_specs=[
            pl.BlockSpec((OUT_TM, D_MODEL), row),
            pl.BlockSpec((H2_HALVES, OUT_TM * SUBLANES, LANES), lambda i: (0, i, 0)),
            pl.BlockSpec((N_EXPERTS, OUT_TM), lambda i: (0, i)),
        ],
        out_shape=[
            jax.ShapeDtypeStruct((n, D_MODEL), F32),
            jax.ShapeDtypeStruct((H2_HALVES, n * SUBLANES, LANES), F32),
            jax.ShapeDtypeStruct((N_EXPERTS, n), F32),
        ],
        compiler_params=_cparams(("arbitrary",)),
        name="out_proj",
    )(a_conv, a_attn, x2, mod6, w_out_bf, b_out, g1, b1, w_router_t)


def _split3(a):
    a1 = a.astype(BF16)
    r = a - a1.astype(F32)
    a2 = r.astype(BF16)
    a3 = (r - a2.astype(F32)).astype(BF16)
    return a1, a2, a3


def _select_kernel(aff_ref, idx_ref, g_ref, base_ref, pos_ref, sel_ref):
    e = pl.program_id(0)

    @pl.when(e == 0)
    def _():
        aff_all = aff_ref[...]

        def count(mask):
            c = jnp.sum(mask.astype(F32), axis=2, keepdims=True)
            return jnp.sum(c, axis=1, keepdims=True)

        def bit_step(k, thr_bits):
            cand = thr_bits | jnp.left_shift(jnp.int32(1), 30 - k)
            ge = aff_all >= pltpu.bitcast(cand, F32)
            return jnp.where(count(ge) >= CAP, cand, thr_bits)

        thr_bits = lax.fori_loop(0, 31, bit_step, jnp.zeros((N_EXPERTS, 1, 1), jnp.int32))
        thr = pltpu.bitcast(thr_bits, F32)
        gt = aff_all > thr
        eq = aff_all == thr
        need = CAP - count(gt)
        eq2 = eq.astype(F32).reshape(N_EXPERTS * N_CHUNKS, CHUNK)
        tri = (lax.broadcasted_iota(jnp.int32, (CHUNK, CHUNK), 0)
               <= lax.broadcasted_iota(jnp.int32, (CHUNK, CHUNK), 1)).astype(BF16)
        incl = jnp.dot(eq2.astype(BF16), tri, preferred_element_type=F32)
        tot = jnp.broadcast_to(incl[:, CHUNK - 1:CHUNK], (N_EXPERTS * N_CHUNKS, LANES))
        rr = lax.broadcasted_iota(jnp.int32, (N_EXPERTS * N_CHUNKS, N_EXPERTS * N_CHUNKS), 0)
        cc = lax.broadcasted_iota(jnp.int32, (N_EXPERTS * N_CHUNKS, N_EXPERTS * N_CHUNKS), 1)
        cshift = N_CHUNKS.bit_length() - 1
        low = ((jnp.right_shift(rr, cshift) == jnp.right_shift(cc, cshift)) & (cc < rr)).astype(BF16)
        before = jnp.dot(low, tot.astype(BF16), preferred_element_type=F32)
        rank = (before + incl - eq2).reshape(N_EXPERTS, N_CHUNKS, CHUNK)
        sel_ref[...] = (gt | (eq & (rank < need))).astype(F32)

    sel = sel_ref[e]
    aff = aff_ref[e]
    tri = (lax.broadcasted_iota(jnp.int32, (CHUNK, CHUNK), 0)
           <= lax.broadcasted_iota(jnp.int32, (CHUNK, CHUNK), 1)).astype(BF16)
    incl = jnp.dot(sel.astype(BF16), tri, preferred_element_type=F32)
    tot = jnp.broadcast_to(incl[:, CHUNK - 1:CHUNK], (N_CHUNKS, LANES))
    pad = lambda a: jnp.concatenate([a, jnp.zeros((LANES - N_CHUNKS, LANES), a.dtype)], axis=0)
    low = (lax.broadcasted_iota(jnp.int32, (N_CHUNKS, LANES), 1)
           < lax.broadcasted_iota(jnp.int32, (N_CHUNKS, LANES), 0)).astype(BF16)
    base = jnp.dot(low, pad(tot.astype(BF16)), preferred_element_type=F32)
    base_ref[0] = base[:, 0:1].astype(jnp.int32)
    pos_ref[0] = jnp.where(sel > 0.0, base + incl - 1.0, -1.0).astype(jnp.int32)

    pick = (lax.broadcasted_iota(jnp.int32, (SUBLANES, LANES), 1) == 0).astype(BF16)
    nt = (((1,), (1,)), ((), ()))
    base_hi = jnp.floor(base * (1.0 / CHUNK))
    base_lo = base - base_hi * CHUNK
    row = lambda a: lax.dot_general(pick, pad(a.astype(BF16)), nt, preferred_element_type=F32)[0:1, :]
    base_row = row(base_hi) * CHUNK + row(base_lo)
    tot_row = row(tot)
    s_col = lax.broadcasted_iota(jnp.int32, (CAP, LANES), 0).astype(F32)
    owner = ((base_row <= s_col) & (s_col < base_row + tot_row)).astype(BF16)

    own = lambda a: jnp.dot(owner, pad(a.astype(BF16)), preferred_element_type=F32)
    incl_s = own(incl)
    base_s = own(base_hi) * CHUNK + own(base_lo)
    cvals = lax.broadcasted_iota(jnp.int32, (N_CHUNKS, LANES), 0).astype(F32)
    chunk_s = own(cvals)
    s_loc = lax.broadcasted_iota(jnp.int32, (CAP, LANES), 0).astype(F32) - base_s
    j_s = jnp.sum((incl_s <= s_loc).astype(F32), axis=1, keepdims=True)
    idx_ref[0] = (chunk_s[:, 0:1] * CHUNK + j_s).astype(jnp.int32)
    a1, a2, a3 = _split3(aff)
    aff_s = (own(a1) + own(a2)) + own(a3)
    lane = lax.broadcasted_iota(jnp.int32, (CAP, LANES), 1).astype(F32)
    g_ref[0] = jnp.sum(jnp.where(lane == j_s, aff_s, 0.0), axis=1, keepdims=True)


def _select(aff3):
    full = lambda e: (0, 0, 0)
    per = lambda e: (e, 0, 0)
    return pl.pallas_call(
        _select_kernel,
        grid=(N_EXPERTS,),
        in_specs=[pl.BlockSpec((N_EXPERTS, N_CHUNKS, CHUNK), full)],
        out_specs=[
            pl.BlockSpec((1, CAP, 1), per),
            pl.BlockSpec((1, CAP, 1), per),
            pl.BlockSpec((1, N_CHUNKS, 1), per),
            pl.BlockSpec((1, N_CHUNKS, CHUNK), per),
        ],
        out_shape=[
            jax.ShapeDtypeStruct((N_EXPERTS, CAP, 1), jnp.int32),
            jax.ShapeDtypeStruct((N_EXPERTS, CAP, 1), F32),
            jax.ShapeDtypeStruct((N_EXPERTS, N_CHUNKS, 1), jnp.int32),
            jax.ShapeDtypeStruct((N_EXPERTS, N_CHUNKS, CHUNK), jnp.int32),
        ],
        scratch_shapes=[pltpu.VMEM((N_EXPERTS, N_CHUNKS, CHUNK), F32)],
        compiler_params=_cparams(("arbitrary",)),
        name="select",
    )(aff3)


FFN_TF = 512
FFN_TN = 256
FFN_GATHER_UNROLL = 8
FFN_VMEM_LIMIT = 60 * 1024 * 1024


def _ffn_kernel(idx_ref, h2_hbm, g_ref, wg_ref, wu_ref, wd_ref, y_ref, xs_ref, xb_ref, act_ref, acc_ref,
                sem):
    e = pl.program_id(0)
    f = pl.program_id(1)
    nf = pl.num_programs(1)

    def start_gather(ee):
        def body(s, carry):
            t8 = pl.multiple_of(idx_ref[ee, s] * SUBLANES, SUBLANES)
            s8 = pl.multiple_of(s * SUBLANES, SUBLANES)
            pltpu.make_async_copy(h2_hbm.at[:, pl.ds(t8, SUBLANES), :],
                                  xs_ref.at[:, pl.ds(s8, SUBLANES), :], sem).start()
            return carry
        lax.fori_loop(0, CAP, body, 0, unroll=FFN_GATHER_UNROLL)

    @pl.when((e == 0) & (f == 0))
    def _():
        start_gather(0)

    @pl.when(f == 0)
    def _():
        pltpu.make_async_copy(h2_hbm.at[:, pl.ds(0, CAP * SUBLANES), :], xs_ref, sem).wait()
        for kk in range(D_MODEL // LANES):
            hh, jj = divmod(kk, SUBLANES)
            xb_ref[:, kk * LANES:(kk + 1) * LANES] = (
                xs_ref[hh, pl.ds(jj, CAP, stride=SUBLANES), :].astype(BF16))
        acc_ref[...] = jnp.zeros_like(acc_ref)

    @pl.when((f == 0) & (e + 1 < pl.num_programs(0)))
    def _():
        start_gather(e + 1)

    for c in range(FFN_TF // FFN_TN):
        cs = slice(c * FFN_TN, (c + 1) * FFN_TN)
        a = jnp.dot(xb_ref[...], wg_ref[0, :, cs].astype(BF16), preferred_element_type=F32)
        u = jnp.dot(xb_ref[...], wu_ref[0, :, cs].astype(BF16), preferred_element_type=F32)
        act_ref[:, cs] = (a * jax.nn.sigmoid(a) * u).astype(BF16)
    for n in range(D_MODEL // FFN_TN):
        ns = slice(n * FFN_TN, (n + 1) * FFN_TN)
        acc_ref[:, ns] += jnp.dot(act_ref[...], wd_ref[0, :, ns].astype(BF16), preferred_element_type=F32)

    @pl.when(f == nf - 1)
    def _():
        y_ref[0] = (acc_ref[...] * g_ref[0]).astype(BF16)


def _ffn(idx, h2, g, w_gate, w_up, w_down):
    return pl.pallas_call(
        _ffn_kernel,
        grid_spec=pltpu.PrefetchScalarGridSpec(
            num_scalar_prefetch=1,
            grid=(N_EXPERTS, D_FF // FFN_TF),
            in_specs=[
                pl.BlockSpec(memory_space=pl.ANY),
                pl.BlockSpec((1, CAP, 1), lambda e, f, idx: (e, 0, 0)),
                pl.BlockSpec((1, D_MODEL, FFN_TF), lambda e, f, idx: (e, 0, f)),
                pl.BlockSpec((1, D_MODEL, FFN_TF), lambda e, f, idx: (e, 0, f)),
                pl.BlockSpec((1, FFN_TF, D_MODEL), lambda e, f, idx: (e, f, 0)),
            ],
            out_specs=pl.BlockSpec((1, CAP, D_MODEL), lambda e, f, idx: (e, 0, 0)),
            scratch_shapes=[
                pltpu.VMEM((H2_HALVES, CAP * SUBLANES, LANES), F32),
                pltpu.VMEM((CAP, D_MODEL), BF16),
                pltpu.VMEM((CAP, FFN_TF), BF16),
                pltpu.VMEM((CAP, D_MODEL), F32),
                pltpu.SemaphoreType.DMA(()),
            ],
        ),
        out_shape=jax.ShapeDtypeStruct((N_EXPERTS, CAP, D_MODEL), BF16),
        compiler_params=_cparams(("arbitrary", "arbitrary"), FFN_VMEM_LIMIT),
        name="ffn",
    )(idx, h2, g, w_gate, w_up, w_down)


CMB_T = CHUNK
CMB_K = 256
CMB_GRAN = 16
CMB_SIZES = (128, 64, 32, 16)
CMB_STAGE = N_EXPERTS * (CMB_T + CMB_GRAN)
assert CMB_STAGE % CMB_K == 0


def _combine_kernel(base_ref, pos_ref, xmid_ref, mod_ref, g2_ref, b2_ref, y_hbm, o_ref,
                    stage_ref, acc_ref, sem):
    i = pl.program_id(0)
    slot = i & 1

    def plan(tile, slot_):
        copies, shifts = [], []
        roff = jnp.int32(0)
        for e in range(N_EXPERTS):
            s0 = e * CAP + base_ref[e * (N_CHUNKS + 1) + tile]
            s1 = e * CAP + base_ref[e * (N_CHUNKS + 1) + tile + 1]
            start = s0 & -CMB_GRAN
            nrows = jnp.where(s1 > s0, (s1 - start + CMB_GRAN - 1) & -CMB_GRAN, 0)
            shifts.append(e * CAP + roff - start)
            done = jnp.int32(0)
            for size in CMB_SIZES:
                pred = (nrows & size) != 0
                cp = pltpu.make_async_copy(
                    y_hbm.at[pl.ds(pl.multiple_of(start + done, CMB_GRAN), size)],
                    stage_ref.at[slot_, pl.ds(pl.multiple_of(roff + done, CMB_GRAN), size)],
                    sem.at[slot_])
                copies.append((pred, cp))
                done = done + jnp.where(pred, size, 0)
            roff = roff + nrows
        return copies, shifts, roff

    def start_all(copies):
        for pred, cp in copies:
            @pl.when(pred)
            def _():
                cp.start()

    @pl.when(i == 0)
    def _():
        stage_ref[...] = jnp.zeros_like(stage_ref)
        start_all(plan(0, 0)[0])

    @pl.when(i + 1 < pl.num_programs(0))
    def _():
        start_all(plan(i + 1, 1 - slot)[0])

    copies, shifts, nstaged = plan(i, slot)
    for pred, cp in copies:
        @pl.when(pred)
        def _():
            cp.wait()

    pos = pos_ref[...]
    rowidx = []
    for e in range(N_EXPERTS):
        p = pos[:, e:e + 1]
        rowidx.append(jnp.where(p >= 0, p + shifts[e], -1))
    acc_ref[...] = jnp.zeros_like(acc_ref)

    def body(k, carry):
        r0 = pl.multiple_of(k * CMB_K, CMB_K)
        lane = lax.broadcasted_iota(jnp.int32, (CMB_T, CMB_K), 1) + r0
        w = jnp.zeros((CMB_T, CMB_K), F32)
        for e in range(N_EXPERTS):
            w = w + (rowidx[e] == lane).astype(F32)
        acc_ref[...] += jnp.dot(w.astype(BF16), stage_ref[slot, pl.ds(r0, CMB_K), :],
                                preferred_element_type=F32)
        return carry

    lax.fori_loop(0, (nstaged + CMB_K - 1) // CMB_K, body, 0)
    gate2 = mod_ref[5:6, :]
    o_ref[...] = _ln(DEEPNORM_ALPHA * xmid_ref[...] + gate2 * acc_ref[...]) * g2_ref[...] + b2_ref[...]


def _combine(base_flat, pos_te, xmid, mod6, g2, b2, y_flat):
    n = xmid.shape[0]
    row = lambda i, b: (i, 0)
    fixed = lambda i, b: (0, 0)
    return pl.pallas_call(
        _combine_kernel,
        grid_spec=pltpu.PrefetchScalarGridSpec(
            num_scalar_prefetch=1,
            grid=(n // CMB_T,),
            in_specs=[
                pl.BlockSpec((CMB_T, N_EXPERTS), row),
                pl.BlockSpec((CMB_T, D_MODEL), row),
                pl.BlockSpec((N_MOD, D_MODEL), fixed),
                pl.BlockSpec((1, D_MODEL), fixed),
                pl.BlockSpec((1, D_MODEL), fixed),
                pl.BlockSpec(memory_space=pl.ANY),
            ],
            out_specs=pl.BlockSpec((CMB_T, D_MODEL), row),
            scratch_shapes=[
                pltpu.VMEM((2, CMB_STAGE, D_MODEL), BF16),
                pltpu.VMEM((CMB_T, D_MODEL), F32),
                pltpu.SemaphoreType.DMA((2,)),
            ],
        ),
        out_shape=jax.ShapeDtypeStruct((n, D_MODEL), F32),
        compiler_params=_cparams(("arbitrary",)),
        name="combine",
    )(base_flat, pos_te, xmid, mod6, g2, b2, y_flat)


def _rope_tables(n):
    rows = n // GRID_W
    inv = ROPE_THETA ** (-jnp.arange(0, ROPE_AXIS_DIM, 2, dtype=F32) / ROPE_AXIS_DIM)
    ang_r = jnp.arange(rows, dtype=F32)[:, None] * inv[None, :]
    ang_c = jnp.arange(GRID_W, dtype=F32)[:, None] * inv[None, :]
    by_row = lambda a: jnp.repeat(a, GRID_W, axis=0)
    by_col = lambda a: jnp.tile(a, (rows, 1))
    cr, sr = by_row(jnp.cos(ang_r)), by_row(jnp.sin(ang_r))
    cc, sc = by_col(jnp.cos(ang_c)), by_col(jnp.sin(ang_c))
    z = jnp.zeros_like(sr)
    cos = jnp.concatenate([cr, cr, cc, cc], axis=1)
    sina = jnp.concatenate([-sr, z, -sc, z], axis=1)
    sinb = jnp.concatenate([z, sr, z, sc], axis=1)
    return cos, sina, sinb


def kernel(x, c, ctx, c_ctx, w_mod, b_mod, w_in, b_in, w_dw, b_dw, conv_ln_g, conv_ln_b, sink,
           w_out, b_out, ln1_g, ln1_b, w_router, w_gate, w_up, w_down, ln2_g, ln2_b):
    assert x.shape == (1, SEQ, D_MODEL) and ctx.shape == (1, CTX_LEN, D_MODEL)
    assert w_mod.shape[0] == DEPTH
    x2 = x[0]
    ctx2 = ctx[0]
    r1 = lambda a: a.reshape(1, -1)

    ct = jnp.stack([c[0], c_ctx], axis=1)
    mod = _mod(ct, w_mod[0], r1(b_mod[0]))
    mod6 = mod[0].reshape(N_MOD, D_MODEL)
    modc6 = mod[1].reshape(N_MOD, D_MODEL)

    w_in_bf = w_in[0].astype(BF16)
    w_out_bf = w_out[0].astype(BF16)
    cos, sina, sinb = _rope_tables(SEQ)
    u, q, k, v = _in_proj(x2, mod6, w_in_bf, r1(b_in[0]), cos, sina, sinb)
    kx, vx = _ctx_kv(ctx2, modc6, w_in_bf, r1(b_in[0]))
    a_conv = _conv(u, w_dw[0], r1(b_dw[0]), r1(conv_ln_g[0]), r1(conv_ln_b[0]))
    a_attn = _attn(sink[0], q, k, v, kx, vx)
    xmid, h2, aff_t = _out_proj(a_conv, a_attn, x2, mod6, w_out_bf, r1(b_out[0]),
                                r1(ln1_g[0]), r1(ln1_b[0]), w_router[0].T)

    idx, g, base, pos = _select(aff_t.reshape(N_EXPERTS, N_CHUNKS, CHUNK))
    y = _ffn(idx.reshape(N_EXPERTS, CAP), h2, g, w_gate[0], w_up[0], w_down[0])

    base_flat = jnp.concatenate(
        [base.reshape(N_EXPERTS, N_CHUNKS), jnp.full((N_EXPERTS, 1), CAP, jnp.int32)], axis=1).reshape(-1)
    pos_te = pos.reshape(N_EXPERTS, SEQ).T
    out = _combine(base_flat, pos_te, xmid, mod6, r1(ln2_g[0]), r1(ln2_b[0]),
                   y.reshape(N_EXPERTS * CAP, D_MODEL))
    return out[None]
```

```python
import numpy as np

import jax
import jax.numpy as jnp
from jax import lax
from jax.experimental import pallas as pl
from jax.experimental.pallas import tpu as pltpu

D_MODEL = 2048
SEQ = 8192
GRID_W = 64
CTX_LEN = 256
HEAD_DIM = 128
N_Q_HEADS = 8
N_KV_HEADS = 2
Q_PER_KV = N_Q_HEADS // N_KV_HEADS
ATTN_WIDTH = N_Q_HEADS * HEAD_DIM
KV_WIDTH = N_KV_HEADS * HEAD_DIM
CONV_WIDTH = D_MODEL - ATTN_WIDTH
CONV_KSIZE = 31
WINDOW = 128
ROPE_THETA = 10000.0
ROPE_AXIS_DIM = HEAD_DIM // 2
N_EXPERTS = 16
EC_CAPACITY = 2
CAP = EC_CAPACITY * SEQ // N_EXPERTS
D_FF = 5632
N_MOD = 6
LN_EPS = 1e-5
NEG_INF = -1e30
DEPTH = 1
DEEPNORM_ALPHA = (2.0 * DEPTH) ** 0.25
Q_START = 2 * CONV_WIDTH
K_START = Q_START + ATTN_WIDTH
V_START = K_START + KV_WIDTH
IN_COLS = V_START + KV_WIDTH

LANES = 128
SUBLANES = 8
H2_HALVES = D_MODEL // (SUBLANES * LANES)
CHUNK = LANES
N_CHUNKS = SEQ // CHUNK
VMEM_LIMIT = 56 * 1024 * 1024

F32 = jnp.float32
BF16 = jnp.bfloat16


def _ln(xv):
    mu = jnp.mean(xv, axis=-1, keepdims=True)
    xc = xv - mu
    var = jnp.mean(xc * xc, axis=-1, keepdims=True)
    return xc * lax.rsqrt(var + LN_EPS)


def _cparams(sem, vmem=VMEM_LIMIT):
    return pltpu.CompilerParams(dimension_semantics=sem, vmem_limit_bytes=vmem)


MOD_TN = 512


def _mod_kernel(ct_ref, w_ref, b_ref, o_ref, s0_ref, s1_ref):
    @pl.when(pl.program_id(0) == 0)
    def _():
        ct = ct_ref[...]
        s = ct * jax.nn.sigmoid(ct)
        s0_ref[...] = jnp.broadcast_to(s[:, 0:1], (D_MODEL, LANES))
        s1_ref[...] = jnp.broadcast_to(s[:, 1:2], (D_MODEL, LANES))

    for j in range(MOD_TN // LANES):
        w = w_ref[:, j * LANES:(j + 1) * LANES]
        r0 = jnp.sum(w * s0_ref[...], axis=0, keepdims=True)
        r1 = jnp.sum(w * s1_ref[...], axis=0, keepdims=True)
        b = b_ref[:, j * LANES:(j + 1) * LANES]
        o_ref[0:1, j * LANES:(j + 1) * LANES] = r0 + b
        o_ref[1:2, j * LANES:(j + 1) * LANES] = r1 + b


def _mod(ct, w_mod, b_mod):
    n_out = N_MOD * D_MODEL
    return pl.pallas_call(
        _mod_kernel,
        grid=(n_out // MOD_TN,),
        in_specs=[
            pl.BlockSpec((D_MODEL, 2), lambda j: (0, 0)),
            pl.BlockSpec((D_MODEL, MOD_TN), lambda j: (0, j)),
            pl.BlockSpec((1, MOD_TN), lambda j: (0, j)),
        ],
        out_specs=pl.BlockSpec((2, MOD_TN), lambda j: (0, j)),
        out_shape=jax.ShapeDtypeStruct((2, n_out), F32),
        scratch_shapes=[pltpu.VMEM((D_MODEL, LANES), F32), pltpu.VMEM((D_MODEL, LANES), F32)],
        compiler_params=_cparams(("arbitrary",)),
        name="mod",
    )(ct, w_mod, b_mod)


IN_TM = 512
IN_TN = 512


def _rope(p, cos, sina, sinb):
    return (p * cos + pltpu.roll(p, HEAD_DIM - ROPE_AXIS_DIM // 2, axis=1) * sina
            + pltpu.roll(p, ROPE_AXIS_DIM // 2, axis=1) * sinb)


def _in_kernel(x_ref, mod_ref, w_ref, b_ref, trow_ref, tcol_ref, u_ref, q_ref, k_ref, v_ref):
    shift = mod_ref[0:1, :]
    scale = mod_ref[1:2, :]
    h = (_ln(x_ref[...]) * (1.0 + scale) + shift).astype(BF16)
    grid_rows = IN_TM // GRID_W

    def table(kind):
        by_row = jnp.concatenate(
            [jnp.broadcast_to(trow_ref[kind, r:r + 1, :], (GRID_W, HEAD_DIM)) for r in range(grid_rows)], axis=0)
        by_col = jnp.concatenate([tcol_ref[kind]] * grid_rows, axis=0)
        return by_row + by_col

    cos, sina, sinb = table(0), table(1), table(2)

    def proj(c0, width):
        return (jnp.dot(h, w_ref[:, c0:c0 + width], preferred_element_type=F32)
                + b_ref[:, c0:c0 + width])

    for j in range(CONV_WIDTH // IN_TN):
        pv = proj(j * IN_TN, IN_TN)
        pg = proj(CONV_WIDTH + j * IN_TN, IN_TN)
        u_ref[:, j * IN_TN:(j + 1) * IN_TN] = pv * jax.nn.sigmoid(pg)
    for j in range(ATTN_WIDTH // IN_TN):
        pq = proj(Q_START + j * IN_TN, IN_TN)
        for hh in range(IN_TN // HEAD_DIM):
            c0 = j * IN_TN + hh * HEAD_DIM
            q_ref[:, c0:c0 + HEAD_DIM] = _rope(
                pq[:, hh * HEAD_DIM:(hh + 1) * HEAD_DIM], cos, sina, sinb).astype(BF16)
    pk = proj(K_START, KV_WIDTH)
    for hh in range(N_KV_HEADS):
        k_ref[:, hh * HEAD_DIM:(hh + 1) * HEAD_DIM] = _rope(
            pk[:, hh * HEAD_DIM:(hh + 1) * HEAD_DIM], cos, sina, sinb).astype(BF16)
    v_ref[...] = proj(V_START, KV_WIDTH).astype(BF16)


def _in_proj(x2, mod6, w_in_bf, b_in, trow, tcol):
    n = x2.shape[0]
    row = lambda i: (i, 0)
    fixed = lambda i: (0, 0)
    return pl.pallas_call(
        _in_kernel,
        grid=(n // IN_TM,),
        in_specs=[
            pl.BlockSpec((IN_TM, D_MODEL), row),
            pl.BlockSpec((N_MOD, D_MODEL), fixed),
            pl.BlockSpec((D_MODEL, IN_COLS), fixed),
            pl.BlockSpec((1, IN_COLS), fixed),
            pl.BlockSpec((3, IN_TM // GRID_W, HEAD_DIM), lambda i: (0, i, 0)),
            pl.BlockSpec((3, GRID_W, HEAD_DIM), lambda i: (0, 0, 0)),
        ],
        out_specs=[
            pl.BlockSpec((IN_TM, CONV_WIDTH), row),
            pl.BlockSpec((IN_TM, ATTN_WIDTH), row),
            pl.BlockSpec((IN_TM, KV_WIDTH), row),
            pl.BlockSpec((IN_TM, KV_WIDTH), row),
        ],
        out_shape=[
            jax.ShapeDtypeStruct((n, CONV_WIDTH), F32),
            jax.ShapeDtypeStruct((n, ATTN_WIDTH), BF16),
            jax.ShapeDtypeStruct((n, KV_WIDTH), BF16),
            jax.ShapeDtypeStruct((n, KV_WIDTH), BF16),
        ],
        compiler_params=_cparams(("arbitrary",)),
        name="in_proj",
    )(x2, mod6, w_in_bf, b_in, trow, tcol)


def _ctx_kernel(x_ref, mod_ref, w_ref, b_ref, kc_ref, vc_ref):
    shift = mod_ref[0:1, :]
    scale = mod_ref[1:2, :]
    h = (_ln(x_ref[...]) * (1.0 + scale) + shift).astype(BF16)
    p = jnp.dot(h, w_ref[...], preferred_element_type=F32) + b_ref[...]
    kc_ref[...] = p[:, :KV_WIDTH].astype(BF16)
    vc_ref[...] = p[:, KV_WIDTH:].astype(BF16)


def _ctx_kv(ctx2, modc6, w_in_bf, b_in):
    kvw = 2 * KV_WIDTH
    fixed = lambda i: (0, 0)
    return pl.pallas_call(
        _ctx_kernel,
        grid=(1,),
        in_specs=[
            pl.BlockSpec((CTX_LEN, D_MODEL), fixed),
            pl.BlockSpec((N_MOD, D_MODEL), fixed),
            pl.BlockSpec((D_MODEL, kvw), lambda i: (0, K_START // kvw)),
            pl.BlockSpec((1, kvw), lambda i: (0, K_START // kvw)),
        ],
        out_specs=[pl.BlockSpec((CTX_LEN, KV_WIDTH), fixed), pl.BlockSpec((CTX_LEN, KV_WIDTH), fixed)],
        out_shape=[jax.ShapeDtypeStruct((CTX_LEN, KV_WIDTH), BF16),
                   jax.ShapeDtypeStruct((CTX_LEN, KV_WIDTH), BF16)],
        compiler_params=_cparams(("arbitrary",)),
        name="ctx_kv",
    )(ctx2, modc6, w_in_bf, b_in)


CONV_T = 512
CONV_HALO = 16
CONV_ROWS = 64


def _conv_kernel(up_ref, uc_ref, un_ref, w_ref, bdw_ref, g_ref, b_ref, o_ref, buf_ref, acc_ref):
    i = pl.program_id(0)
    last = pl.num_programs(0) - 1
    buf_ref[0:CONV_HALO, :] = jnp.where(i > 0, up_ref[...], 0.0)
    buf_ref[CONV_HALO:CONV_HALO + CONV_T, :] = uc_ref[...]
    buf_ref[CONV_HALO + CONV_T:, :] = jnp.where(i < last, un_ref[...], 0.0)
    off = CONV_HALO - CONV_KSIZE // 2

    span = CONV_ROWS + 2 * CONV_HALO

    def body(r, carry):
        r0 = pl.multiple_of(r * CONV_ROWS, CONV_ROWS)
        for lg in range(CONV_WIDTH // LANES):
            ls = slice(lg * LANES, (lg + 1) * LANES)
            blk = buf_ref[pl.ds(r0, span), ls]
            shifted = [blk] + [pltpu.roll(blk, span - s, axis=0) for s in range(1, SUBLANES)]
            acc = jnp.zeros((CONV_ROWS, LANES), F32)
            for t in range(CONV_KSIZE):
                d = off + t
                a0 = (d // SUBLANES) * SUBLANES
                acc = acc + shifted[d % SUBLANES][a0:a0 + CONV_ROWS, :] * w_ref[t:t + 1, ls]
            acc_ref[pl.ds(r0, CONV_ROWS), ls] = acc
        return carry

    lax.fori_loop(0, CONV_T // CONV_ROWS, body, 0)
    y = _ln(acc_ref[...] + bdw_ref[...]) * g_ref[...] + b_ref[...]
    o_ref[...] = (y * jax.nn.sigmoid(y)).astype(BF16)


def _conv(u, w_dw, b_dw, ln_g, ln_b):
    n = u.shape[0]
    hb = CONV_T // CONV_HALO
    nhb = n // CONV_HALO
    fixed = lambda i: (0, 0)
    return pl.pallas_call(
        _conv_kernel,
        grid=(n // CONV_T,),
        in_specs=[
            pl.BlockSpec((CONV_HALO, CONV_WIDTH), lambda i: (jnp.maximum(i * hb - 1, 0), 0)),
            pl.BlockSpec((CONV_T, CONV_WIDTH), lambda i: (i, 0)),
            pl.BlockSpec((CONV_HALO, CONV_WIDTH), lambda i: (jnp.minimum((i + 1) * hb, nhb - 1), 0)),
            pl.BlockSpec((CONV_KSIZE, CONV_WIDTH), fixed),
            pl.BlockSpec((1, CONV_WIDTH), fixed),
            pl.BlockSpec((1, CONV_WIDTH), fixed),
            pl.BlockSpec((1, CONV_WIDTH), fixed),
        ],
        out_specs=pl.BlockSpec((CONV_T, CONV_WIDTH), lambda i: (i, 0)),
        out_shape=jax.ShapeDtypeStruct((n, CONV_WIDTH), BF16),
        scratch_shapes=[pltpu.VMEM((CONV_T + 2 * CONV_HALO, CONV_WIDTH), F32),
                        pltpu.VMEM((CONV_T, CONV_WIDTH), F32)],
        compiler_params=_cparams(("arbitrary",)),
        name="conv",
    )(u, u, u, w_dw, b_dw, ln_g, ln_b)


ATT_T = 128
ATT_NB = 1
assert ATT_T == WINDOW
LOG2E = 1.4426950408889634


def _attn_kernel(sink_ref, q_ref, kp_ref, kc_ref, kn_ref, vp_ref, vc_ref, vn_ref, kx_ref, vx_ref, o_ref,
                 bias_ref):
    i = pl.program_id(0)
    last = pl.num_programs(0) - 1
    scale = HEAD_DIM ** -0.5
    rows = Q_PER_KV * ATT_T

    @pl.when((i <= 1) | (i == last))
    def _():
        qi = lax.broadcasted_iota(jnp.int32, (rows, 3 * ATT_T), 0) & (ATT_T - 1)
        m = lax.broadcasted_iota(jnp.int32, (rows, 3 * ATT_T), 1)
        band = jnp.abs(m - qi - ATT_T) <= WINDOW
        for b in range(ATT_NB):
            kpos = (i * ATT_NB + b - 1) * ATT_T + m
            bias_ref[b] = jnp.where(band & (kpos >= 0) & (kpos < SEQ), 0.0, NEG_INF)

    hrow = jnp.right_shift(lax.broadcasted_iota(jnp.int32, (rows, 1), 0), ATT_T.bit_length() - 1)
    nt = (((1,), (1,)), ((), ()))
    for b in range(ATT_NB):
        rs = slice(b * ATT_T, (b + 1) * ATT_T)
        for g in range(N_KV_HEADS):
            ls = slice(g * HEAD_DIM, (g + 1) * HEAD_DIM)

            def key_blocks(p_ref, c_ref, n_ref):
                blocks = ([p_ref[:, ls]] + [c_ref[j * ATT_T:(j + 1) * ATT_T, ls] for j in range(ATT_NB)]
                          + [n_ref[:, ls]])
                return jnp.concatenate(blocks[b:b + 3], axis=0)

            qs = jnp.concatenate(
                [q_ref[rs, (g * Q_PER_KV + hh) * HEAD_DIM:(g * Q_PER_KV + hh + 1) * HEAD_DIM]
                 for hh in range(Q_PER_KV)], axis=0)
            kw = key_blocks(kp_ref, kc_ref, kn_ref)
            vw = key_blocks(vp_ref, vc_ref, vn_ref)
            s_win = lax.dot_general(qs, kw, nt, preferred_element_type=F32) + bias_ref[b]
            s_ctx = lax.dot_general(qs, kx_ref[:, ls], nt, preferred_element_type=F32)
            s_sink = jnp.zeros((rows, 1), F32)
            for hh in range(Q_PER_KV):
                s_sink = jnp.where(hrow == hh, sink_ref[g * Q_PER_KV + hh], s_sink)
            raw_max = jnp.maximum(jnp.max(s_win, axis=-1, keepdims=True), jnp.max(s_ctx, axis=-1, keepdims=True))
            mx2 = jnp.maximum(raw_max * scale, s_sink) * LOG2E
            e_win = jnp.exp2(s_win * (scale * LOG2E) - mx2)
            e_ctx = jnp.exp2(s_ctx * (scale * LOG2E) - mx2)
            den = (jnp.sum(e_win, axis=-1, keepdims=True) + jnp.sum(e_ctx, axis=-1, keepdims=True)
                   + jnp.exp2(s_sink * LOG2E - mx2))
            o = (jnp.dot(e_ctx.astype(BF16), vx_ref[:, ls], preferred_element_type=F32)
                 + jnp.dot(e_win.astype(BF16), vw, preferred_element_type=F32)) * (1.0 / den)
            for hh in range(Q_PER_KV):
                c0 = (g * Q_PER_KV + hh) * HEAD_DIM
                o_ref[rs, c0:c0 + HEAD_DIM] = o[hh * ATT_T:(hh + 1) * ATT_T, :].astype(BF16)


def _attn(sink, q, k, v, kx, vx):
    n = q.shape[0]
    nb = n // ATT_T
    prev = lambda i, s: (jnp.maximum(i * ATT_NB - 1, 0), 0)
    cur = lambda i, s: (i, 0)
    nxt = lambda i, s: (jnp.minimum((i + 1) * ATT_NB, nb - 1), 0)
    fixed = lambda i, s: (0, 0)
    edge = lambda im: pl.BlockSpec((ATT_T, KV_WIDTH), im)
    own = pl.BlockSpec((ATT_NB * ATT_T, KV_WIDTH), cur)
    return pl.pallas_call(
        _attn_kernel,
        grid_spec=pltpu.PrefetchScalarGridSpec(
            num_scalar_prefetch=1,
            grid=(nb // ATT_NB,),
            in_specs=[
                pl.BlockSpec((ATT_NB * ATT_T, ATTN_WIDTH), cur),
                edge(prev), own, edge(nxt),
                edge(prev), own, edge(nxt),
                pl.BlockSpec((CTX_LEN, KV_WIDTH), fixed),
                pl.BlockSpec((CTX_LEN, KV_WIDTH), fixed),
            ],
            out_specs=pl.BlockSpec((ATT_NB * ATT_T, ATTN_WIDTH), cur),
            scratch_shapes=[pltpu.VMEM((ATT_NB, Q_PER_KV * ATT_T, 3 * ATT_T), F32)],
        ),
        out_shape=jax.ShapeDtypeStruct((n, ATTN_WIDTH), BF16),
        compiler_params=_cparams(("arbitrary",)),
        name="attn",
    )(sink, q, k, k, k, v, v, v, kx, vx)


OUT_TM = 512
OUT_SUB = 128


def _out_kernel(ac_ref, aa_ref, x_ref, mod_ref, w_ref, b_ref, g1_ref, b1_ref, wrh_ref, wrl_ref,
                xmid_ref, h2_ref, aff_ref):
    gate1 = mod_ref[2:3, :]
    for sb in range(OUT_TM // OUT_SUB):
        rs = slice(sb * OUT_SUB, (sb + 1) * OUT_SUB)
        mix = (jnp.dot(ac_ref[rs, :], w_ref[:CONV_WIDTH, :], preferred_element_type=F32)
               + jnp.dot(aa_ref[rs, :], w_ref[CONV_WIDTH:, :], preferred_element_type=F32) + b_ref[...])
        xmid = _ln(DEEPNORM_ALPHA * x_ref[rs, :] + gate1 * mix) * g1_ref[...] + b1_ref[...]
        xmid_ref[rs, :] = xmid
        h2 = _ln(xmid) * (1.0 + mod_ref[4:5, :]) + mod_ref[3:4, :]
        for kk in range(D_MODEL // LANES):
            hh, jj = divmod(kk, SUBLANES)
            h2_ref[hh, pl.ds(sb * OUT_SUB * SUBLANES + jj, OUT_SUB, stride=SUBLANES), :] = (
                h2[:, kk * LANES:(kk + 1) * LANES])
        h_hi = h2.astype(BF16)
        h_lo = (h2 - h_hi.astype(F32)).astype(BF16)
        logits = (jnp.dot(h_hi, wrh_ref[...], preferred_element_type=F32)
                  + jnp.dot(h_lo, wrh_ref[...], preferred_element_type=F32)
                  + jnp.dot(h_hi, wrl_ref[...], preferred_element_type=F32))
        logits = logits.T[:N_EXPERTS, :]
        mx = jnp.max(logits, axis=0, keepdims=True)
        ex = jnp.exp(logits - mx)
        aff_ref[:, rs] = ex / jnp.sum(ex, axis=0, keepdims=True)


def _out_proj(a_conv, a_attn, x2, mod6, w_out_bf, b_out, g1, b1, wr_hi, wr_lo):
    n = x2.shape[0]
    row = lambda i: (i, 0)
    fixed = lambda i: (0, 0)
    return pl.pallas_call(
        _out_kernel,
        grid=(n // OUT_TM,),
        in_specs=[
            pl.BlockSpec((OUT_TM, CONV_WIDTH), row),
            pl.BlockSpec((OUT_TM, ATTN_WIDTH), row),
            pl.BlockSpec((OUT_TM, D_MODEL), row),
            pl.BlockSpec((N_MOD, D_MODEL), fixed),
            pl.BlockSpec((D_MODEL, D_MODEL), fixed),
            pl.BlockSpec((1, D_MODEL), fixed),
            pl.BlockSpec((1, D_MODEL), fixed),
            pl.BlockSpec((1, D_MODEL), fixed),
            pl.BlockSpec((D_MODEL, LANES), fixed),
            pl.BlockSpec((D_MODEL, LANES), fixed),
        ],
        out_specs=[
            pl.BlockSpec((OUT_TM, D_MODEL), row),
            pl.BlockSpec((H2_HALVES, OUT_TM * SUBLANES, LANES), lambda i: (0, i, 0)),
            pl.BlockSpec((N_EXPERTS, OUT_TM), lambda i: (0, i)),
        ],
        out_shape=[
            jax.ShapeDtypeStruct((n, D_MODEL), F32),
            jax.ShapeDtypeStruct((H2_HALVES, n * SUBLANES, LANES), F32),
            jax.ShapeDtypeStruct((N_EXPERTS, n), F32),
        ],
        compiler_params=_cparams(("arbitrary",)),
        name="out_proj",
    )(a_conv, a_attn, x2, mod6, w_out_bf, b_out, g1, b1, wr_hi, wr_lo)


def _split3(a):
    a1 = a.astype(BF16)
    r = a - a1.astype(F32)
    a2 = r.astype(BF16)
    a3 = (r - a2.astype(F32)).astype(BF16)
    return a1, a2, a3


def _select_kernel(aff_ref, idx_ref, g_ref, base_ref, pos_ref, sel_ref):
    e = pl.program_id(0)

    @pl.when(e == 0)
    def _():
        aff_all = aff_ref[...]

        def count(mask):
            c = jnp.sum(mask.astype(F32), axis=2, keepdims=True)
            return jnp.sum(c, axis=1, keepdims=True)

        def bit_step(k, thr_bits):
            cand = thr_bits | jnp.left_shift(jnp.int32(1), 30 - k)
            ge = aff_all >= pltpu.bitcast(cand, F32)
            return jnp.where(count(ge) >= CAP, cand, thr_bits)

        thr_bits = lax.fori_loop(0, 31, bit_step, jnp.zeros((N_EXPERTS, 1, 1), jnp.int32))
        thr = pltpu.bitcast(thr_bits, F32)
        gt = aff_all > thr
        eq = aff_all == thr
        need = CAP - count(gt)
        eq2 = eq.astype(F32).reshape(N_EXPERTS * N_CHUNKS, CHUNK)
        tri = (lax.broadcasted_iota(jnp.int32, (CHUNK, CHUNK), 0)
               <= lax.broadcasted_iota(jnp.int32, (CHUNK, CHUNK), 1)).astype(BF16)
        incl = jnp.dot(eq2.astype(BF16), tri, preferred_element_type=F32)
        tot = jnp.broadcast_to(incl[:, CHUNK - 1:CHUNK], (N_EXPERTS * N_CHUNKS, LANES))
        rr = lax.broadcasted_iota(jnp.int32, (N_EXPERTS * N_CHUNKS, N_EXPERTS * N_CHUNKS), 0)
        cc = lax.broadcasted_iota(jnp.int32, (N_EXPERTS * N_CHUNKS, N_EXPERTS * N_CHUNKS), 1)
        cshift = N_CHUNKS.bit_length() - 1
        low = ((jnp.right_shift(rr, cshift) == jnp.right_shift(cc, cshift)) & (cc < rr)).astype(BF16)
        before = jnp.dot(low, tot.astype(BF16), preferred_element_type=F32)
        rank = (before + incl - eq2).reshape(N_EXPERTS, N_CHUNKS, CHUNK)
        sel_ref[...] = (gt | (eq & (rank < need))).astype(F32)

    sel = sel_ref[e]
    aff = aff_ref[e]
    tri = (lax.broadcasted_iota(jnp.int32, (CHUNK, CHUNK), 0)
           <= lax.broadcasted_iota(jnp.int32, (CHUNK, CHUNK), 1)).astype(BF16)
    incl = jnp.dot(sel.astype(BF16), tri, preferred_element_type=F32)
    tot = jnp.broadcast_to(incl[:, CHUNK - 1:CHUNK], (N_CHUNKS, LANES))
    pad = lambda a: jnp.concatenate([a, jnp.zeros((LANES - N_CHUNKS, LANES), a.dtype)], axis=0)
    low = (lax.broadcasted_iota(jnp.int32, (N_CHUNKS, LANES), 1)
           < lax.broadcasted_iota(jnp.int32, (N_CHUNKS, LANES), 0)).astype(BF16)
    base = jnp.dot(low, pad(tot.astype(BF16)), preferred_element_type=F32)
    base_ref[0] = base[:, 0:1].astype(jnp.int32)
    pos_ref[0] = jnp.where(sel > 0.0, base + incl - 1.0, -1.0).astype(jnp.int32)

    pick = (lax.broadcasted_iota(jnp.int32, (SUBLANES, LANES), 1) == 0).astype(BF16)
    nt = (((1,), (1,)), ((), ()))
    base_hi = jnp.floor(base * (1.0 / CHUNK))
    base_lo = base - base_hi * CHUNK
    row = lambda a: lax.dot_general(pick, pad(a.astype(BF16)), nt, preferred_element_type=F32)[0:1, :]
    base_row = row(base_hi) * CHUNK + row(base_lo)
    tot_row = row(tot)
    s_col = lax.broadcasted_iota(jnp.int32, (CAP, LANES), 0).astype(F32)
    owner = ((base_row <= s_col) & (s_col < base_row + tot_row)).astype(BF16)

    own = lambda a: jnp.dot(owner, pad(a.astype(BF16)), preferred_element_type=F32)
    incl_s = own(incl)
    base_s = own(base_hi) * CHUNK + own(base_lo)
    cvals = lax.broadcasted_iota(jnp.int32, (N_CHUNKS, LANES), 0).astype(F32)
    chunk_s = own(cvals)
    s_loc = lax.broadcasted_iota(jnp.int32, (CAP, LANES), 0).astype(F32) - base_s
    j_s = jnp.sum((incl_s <= s_loc).astype(F32), axis=1, keepdims=True)
    idx_ref[0] = (chunk_s[:, 0:1] * CHUNK + j_s).astype(jnp.int32)
    a1, a2, a3 = _split3(aff)
    aff_s = (own(a1) + own(a2)) + own(a3)
    lane = lax.broadcasted_iota(jnp.int32, (CAP, LANES), 1).astype(F32)
    g_ref[0] = jnp.sum(jnp.where(lane == j_s, aff_s, 0.0), axis=1, keepdims=True)


def _select(aff3):
    full = lambda e: (0, 0, 0)
    per = lambda e: (e, 0, 0)
    return pl.pallas_call(
        _select_kernel,
        grid=(N_EXPERTS,),
        in_specs=[pl.BlockSpec((N_EXPERTS, N_CHUNKS, CHUNK), full)],
        out_specs=[
            pl.BlockSpec((1, CAP, 1), per),
            pl.BlockSpec((1, CAP, 1), per),
            pl.BlockSpec((1, N_CHUNKS, 1), per),
            pl.BlockSpec((1, N_CHUNKS, CHUNK), per),
        ],
        out_shape=[
            jax.ShapeDtypeStruct((N_EXPERTS, CAP, 1), jnp.int32),
            jax.ShapeDtypeStruct((N_EXPERTS, CAP, 1), F32),
            jax.ShapeDtypeStruct((N_EXPERTS, N_CHUNKS, 1), jnp.int32),
            jax.ShapeDtypeStruct((N_EXPERTS, N_CHUNKS, CHUNK), jnp.int32),
        ],
        scratch_shapes=[pltpu.VMEM((N_EXPERTS, N_CHUNKS, CHUNK), F32)],
        compiler_params=_cparams(("arbitrary",)),
        name="select",
    )(aff3)


FFN_TF = 512
FFN_NF = D_FF // FFN_TF
FFN_TN = 256
FFN_GATHER_UNROLL = 8
FFN_VMEM_LIMIT = 60 * 1024 * 1024


def _ffn_kernel(idx_ref, h2_hbm, g_ref, wg_ref, wu_ref, wd_ref, y_ref, xs_ref, xb_ref, act_ref, acc_ref,
                sem):
    e = pl.program_id(0)
    f = pl.program_id(1)
    nf = pl.num_programs(1)

    def start_row(ee, s):
        t8 = pl.multiple_of(idx_ref[ee, s] * SUBLANES, SUBLANES)
        s8 = pl.multiple_of(s * SUBLANES, SUBLANES)
        pltpu.make_async_copy(h2_hbm.at[:, pl.ds(t8, SUBLANES), :],
                              xs_ref.at[:, pl.ds(s8, SUBLANES), :], sem).start()

    def wait_rows():
        pltpu.make_async_copy(h2_hbm.at[:, pl.ds(0, CAP * SUBLANES), :], xs_ref, sem).wait()

    @pl.when((e == 0) & (f == 0))
    def _():
        def body(s, carry):
            start_row(0, s)
            return carry
        lax.fori_loop(0, CAP, body, 0, unroll=FFN_GATHER_UNROLL)

    nxt = jnp.where(e + 1 < N_EXPERTS, e + 1, 0)
    per_step = CAP // FFN_NF

    @pl.when(f == 0)
    def _():
        wait_rows()
        for kk in range(D_MODEL // LANES):
            hh, jj = divmod(kk, SUBLANES)
            xb_ref[:, kk * LANES:(kk + 1) * LANES] = (
                xs_ref[hh, pl.ds(jj, CAP, stride=SUBLANES), :].astype(BF16))
        acc_ref[...] = jnp.zeros_like(acc_ref)
        for s in range(per_step * FFN_NF, CAP):
            start_row(nxt, s)

    for j in range(per_step):
        start_row(nxt, f * per_step + j)

    for c in range(FFN_TF // FFN_TN):
        cs = slice(c * FFN_TN, (c + 1) * FFN_TN)
        a = jnp.dot(xb_ref[...], wg_ref[0, :, cs].astype(BF16), preferred_element_type=F32)
        u = jnp.dot(xb_ref[...], wu_ref[0, :, cs].astype(BF16), preferred_element_type=F32)
        act_ref[:, cs] = (a * jax.nn.sigmoid(a) * u).astype(BF16)
    for n in range(D_MODEL // FFN_TN):
        ns = slice(n * FFN_TN, (n + 1) * FFN_TN)
        acc_ref[:, ns] += jnp.dot(act_ref[...], wd_ref[0, :, ns].astype(BF16), preferred_element_type=F32)

    @pl.when(f == nf - 1)
    def _():
        y_ref[0] = (acc_ref[...] * g_ref[0]).astype(BF16)

    @pl.when((e == N_EXPERTS - 1) & (f == nf - 1))
    def _():
        wait_rows()


def _ffn(idx, h2, g, w_gate, w_up, w_down):
    return pl.pallas_call(
        _ffn_kernel,
        grid_spec=pltpu.PrefetchScalarGridSpec(
            num_scalar_prefetch=1,
            grid=(N_EXPERTS, D_FF // FFN_TF),
            in_specs=[
                pl.BlockSpec(memory_space=pl.ANY),
                pl.BlockSpec((1, CAP, 1), lambda e, f, idx: (e, 0, 0)),
                pl.BlockSpec((1, D_MODEL, FFN_TF), lambda e, f, idx: (e, 0, f)),
                pl.BlockSpec((1, D_MODEL, FFN_TF), lambda e, f, idx: (e, 0, f)),
                pl.BlockSpec((1, FFN_TF, D_MODEL), lambda e, f, idx: (e, f, 0)),
            ],
            out_specs=pl.BlockSpec((1, CAP, D_MODEL), lambda e, f, idx: (e, 0, 0)),
            scratch_shapes=[
                pltpu.VMEM((H2_HALVES, CAP * SUBLANES, LANES), F32),
                pltpu.VMEM((CAP, D_MODEL), BF16),
                pltpu.VMEM((CAP, FFN_TF), BF16),
                pltpu.VMEM((CAP, D_MODEL), F32),
                pltpu.SemaphoreType.DMA(()),
            ],
        ),
        out_shape=jax.ShapeDtypeStruct((N_EXPERTS, CAP, D_MODEL), BF16),
        compiler_params=_cparams(("arbitrary", "arbitrary"), FFN_VMEM_LIMIT),
        name="ffn",
    )(idx, h2, g, w_gate, w_up, w_down)


CMB_T = CHUNK
CMB_K = 256
CMB_GRAN = 16
CMB_SIZES = (128, 64, 32, 16)
CMB_STAGE = N_EXPERTS * (CMB_T + CMB_GRAN)
assert CMB_STAGE % CMB_K == 0


def _combine_kernel(base_ref, pos_ref, xmid_ref, mod_ref, g2_ref, b2_ref, y_hbm, o_ref,
                    stage_ref, acc_ref, sem):
    i = pl.program_id(0)
    slot = i & 1

    def plan(tile, slot_):
        copies, shifts = [], []
        roff = jnp.int32(0)
        for e in range(N_EXPERTS):
            s0 = e * CAP + base_ref[e * (N_CHUNKS + 1) + tile]
            s1 = e * CAP + base_ref[e * (N_CHUNKS + 1) + tile + 1]
            start = s0 & -CMB_GRAN
            nrows = jnp.where(s1 > s0, (s1 - start + CMB_GRAN - 1) & -CMB_GRAN, 0)
            shifts.append(e * CAP + roff - start)
            done = jnp.int32(0)
            for size in CMB_SIZES:
                pred = (nrows & size) != 0
                cp = pltpu.make_async_copy(
                    y_hbm.at[pl.ds(pl.multiple_of(start + done, CMB_GRAN), size)],
                    stage_ref.at[slot_, pl.ds(pl.multiple_of(roff + done, CMB_GRAN), size)],
                    sem.at[slot_])
                copies.append((pred, cp))
                done = done + jnp.where(pred, size, 0)
            roff = roff + nrows
        return copies, shifts, roff

    def start_all(copies):
        for pred, cp in copies:
            @pl.when(pred)
            def _():
                cp.start()

    @pl.when(i == 0)
    def _():
        stage_ref[...] = jnp.zeros_like(stage_ref)
        start_all(plan(0, 0)[0])

    @pl.when(i + 1 < pl.num_programs(0))
    def _():
        start_all(plan(i + 1, 1 - slot)[0])

    copies, shifts, nstaged = plan(i, slot)
    for pred, cp in copies:
        @pl.when(pred)
        def _():
            cp.wait()

    pos = pos_ref[...]
    rowidx = []
    for e in range(N_EXPERTS):
        p = pos[:, e:e + 1]
        rowidx.append(jnp.where(p >= 0, p + shifts[e], -1))
    acc_ref[...] = jnp.zeros_like(acc_ref)

    def body(k, carry):
        r0 = pl.multiple_of(k * CMB_K, CMB_K)
        lane = lax.broadcasted_iota(jnp.int32, (CMB_T, CMB_K), 1) + r0
        w = jnp.zeros((CMB_T, CMB_K), F32)
        for e in range(N_EXPERTS):
            w = w + (rowidx[e] == lane).astype(F32)
        acc_ref[...] += jnp.dot(w.astype(BF16), stage_ref[slot, pl.ds(r0, CMB_K), :],
                                preferred_element_type=F32)
        return carry

    lax.fori_loop(0, (nstaged + CMB_K - 1) // CMB_K, body, 0)
    gate2 = mod_ref[5:6, :]
    o_ref[...] = _ln(DEEPNORM_ALPHA * xmid_ref[...] + gate2 * acc_ref[...]) * g2_ref[...] + b2_ref[...]


def _combine(base_flat, pos_te, xmid, mod6, g2, b2, y_flat):
    n = xmid.shape[0]
    row = lambda i, b: (i, 0)
    fixed = lambda i, b: (0, 0)
    return pl.pallas_call(
        _combine_kernel,
        grid_spec=pltpu.PrefetchScalarGridSpec(
            num_scalar_prefetch=1,
            grid=(n // CMB_T,),
            in_specs=[
                pl.BlockSpec((CMB_T, N_EXPERTS), row),
                pl.BlockSpec((CMB_T, D_MODEL), row),
                pl.BlockSpec((N_MOD, D_MODEL), fixed),
                pl.BlockSpec((1, D_MODEL), fixed),
                pl.BlockSpec((1, D_MODEL), fixed),
                pl.BlockSpec(memory_space=pl.ANY),
            ],
            out_specs=pl.BlockSpec((CMB_T, D_MODEL), row),
            scratch_shapes=[
                pltpu.VMEM((2, CMB_STAGE, D_MODEL), BF16),
                pltpu.VMEM((CMB_T, D_MODEL), F32),
                pltpu.SemaphoreType.DMA((2,)),
            ],
        ),
        out_shape=jax.ShapeDtypeStruct((n, D_MODEL), F32),
        compiler_params=_cparams(("arbitrary",)),
        name="combine",
    )(base_flat, pos_te, xmid, mod6, g2, b2, y_flat)


def _rope_tables(n):
    rows = n // GRID_W
    inv = ROPE_THETA ** (-np.arange(0, ROPE_AXIS_DIM, 2, dtype=np.float64) / ROPE_AXIS_DIM)
    ang_r = np.arange(rows, dtype=np.float64)[:, None] * inv[None, :]
    ang_c = np.arange(GRID_W, dtype=np.float64)[:, None] * inv[None, :]
    zr, zc = np.zeros_like(ang_r), np.zeros_like(ang_c)
    cr, sr, cc, sc = np.cos(ang_r), np.sin(ang_r), np.cos(ang_c), np.sin(ang_c)
    trow = np.stack([np.concatenate(p, axis=1) for p in
                     ([cr, cr, zr, zr], [-sr, zr, zr, zr], [zr, sr, zr, zr])])
    tcol = np.stack([np.concatenate(p, axis=1) for p in
                     ([zc, zc, cc, cc], [zc, zc, -sc, zc], [zc, zc, zc, sc])])
    return jnp.asarray(trow, F32), jnp.asarray(tcol, F32)


def kernel(x, c, ctx, c_ctx, w_mod, b_mod, w_in, b_in, w_dw, b_dw, conv_ln_g, conv_ln_b, sink,
           w_out, b_out, ln1_g, ln1_b, w_router, w_gate, w_up, w_down, ln2_g, ln2_b):
    assert x.shape == (1, SEQ, D_MODEL) and ctx.shape == (1, CTX_LEN, D_MODEL)
    assert w_mod.shape[0] == DEPTH
    x2 = x[0]
    ctx2 = ctx[0]
    r1 = lambda a: a.reshape(1, -1)

    ct = jnp.stack([c[0], c_ctx], axis=1)
    mod = _mod(ct, w_mod[0], r1(b_mod[0]))
    mod6 = mod[0].reshape(N_MOD, D_MODEL)
    modc6 = mod[1].reshape(N_MOD, D_MODEL)

    w_in_bf = w_in[0].astype(BF16)
    w_out_bf = w_out[0].astype(BF16)
    trow, tcol = _rope_tables(SEQ)
    u, q, k, v = _in_proj(x2, mod6, w_in_bf, r1(b_in[0]), trow, tcol)
    kx, vx = _ctx_kv(ctx2, modc6, w_in_bf, r1(b_in[0]))
    a_conv = _conv(u, w_dw[0], r1(b_dw[0]), r1(conv_ln_g[0]), r1(conv_ln_b[0]))
    a_attn = _attn(sink[0], q, k, v, kx, vx)
    wr = jnp.pad(w_router[0], ((0, 0), (0, LANES - N_EXPERTS)))
    wr_hi = wr.astype(BF16)
    wr_lo = (wr - wr_hi.astype(F32)).astype(BF16)
    xmid, h2, aff_t = _out_proj(a_conv, a_attn, x2, mod6, w_out_bf, r1(b_out[0]),
                                r1(ln1_g[0]), r1(ln1_b[0]), wr_hi, wr_lo)

    idx, g, base, pos = _select(aff_t.reshape(N_EXPERTS, N_CHUNKS, CHUNK))
    y = _ffn(idx.reshape(N_EXPERTS, CAP), h2, g, w_gate[0], w_up[0], w_down[0])

    base_flat = jnp.concatenate(
        [base.reshape(N_EXPERTS, N_CHUNKS), jnp.full((N_EXPERTS, 1), CAP, jnp.int32)], axis=1).reshape(-1)
    pos_te = pos.reshape(N_EXPERTS, SEQ).T
    out = _combine(base_flat, pos_te, xmid, mod6, r1(ln2_g[0]), r1(ln2_b[0]),
                   y.reshape(N_EXPERTS * CAP, D_MODEL))
    return out[None]
```

```python
import numpy as np

import jax
import jax.numpy as jnp
from jax import lax
from jax.experimental import pallas as pl
from jax.experimental.pallas import tpu as pltpu

D_MODEL = 2048
SEQ = 8192
GRID_W = 64
CTX_LEN = 256
HEAD_DIM = 128
N_Q_HEADS = 8
N_KV_HEADS = 2
Q_PER_KV = N_Q_HEADS // N_KV_HEADS
ATTN_WIDTH = N_Q_HEADS * HEAD_DIM
KV_WIDTH = N_KV_HEADS * HEAD_DIM
CONV_WIDTH = D_MODEL - ATTN_WIDTH
CONV_KSIZE = 31
WINDOW = 128
ROPE_THETA = 10000.0
ROPE_AXIS_DIM = HEAD_DIM // 2
N_EXPERTS = 16
EC_CAPACITY = 2
CAP = EC_CAPACITY * SEQ // N_EXPERTS
D_FF = 5632
N_MOD = 6
LN_EPS = 1e-5
NEG_INF = -1e30
DEPTH = 1
DEEPNORM_ALPHA = (2.0 * DEPTH) ** 0.25
Q_START = 2 * CONV_WIDTH
K_START = Q_START + ATTN_WIDTH
V_START = K_START + KV_WIDTH
IN_COLS = V_START + KV_WIDTH

LANES = 128
SUBLANES = 8
H2_HALVES = D_MODEL // (SUBLANES * LANES)
CHUNK = LANES
N_CHUNKS = SEQ // CHUNK
VMEM_LIMIT = 56 * 1024 * 1024

F32 = jnp.float32
BF16 = jnp.bfloat16


def _ln(xv):
    mu = jnp.mean(xv, axis=-1, keepdims=True)
    xc = xv - mu
    var = jnp.mean(xc * xc, axis=-1, keepdims=True)
    return xc * lax.rsqrt(var + LN_EPS)


def _cparams(sem, vmem=VMEM_LIMIT):
    return pltpu.CompilerParams(dimension_semantics=sem, vmem_limit_bytes=vmem)


MOD_TN = 1024
MOD_UNROLL = 4


def _mod_kernel(ct_ref, w_ref, b_ref, o_ref, s0_ref, s1_ref):
    @pl.when(pl.program_id(0) == 0)
    def _():
        ct = ct_ref[...]
        s = ct * jax.nn.sigmoid(ct)
        s0_ref[...] = jnp.broadcast_to(s[:, 0:1], (D_MODEL, LANES))
        s1_ref[...] = jnp.broadcast_to(s[:, 1:2], (D_MODEL, LANES))

    def body(kb, acc):
        k0 = pl.multiple_of(kb * SUBLANES, SUBLANES)
        s0 = s0_ref[pl.ds(k0, SUBLANES), :]
        s1 = s1_ref[pl.ds(k0, SUBLANES), :]
        a0, a1 = [], []
        for j in range(MOD_TN // LANES):
            w = w_ref[pl.ds(k0, SUBLANES), j * LANES:(j + 1) * LANES]
            a0.append(acc[0][j] + w * s0)
            a1.append(acc[1][j] + w * s1)
        return tuple(a0), tuple(a1)

    zeros = tuple(jnp.zeros((SUBLANES, LANES), F32) for _ in range(MOD_TN // LANES))
    acc0, acc1 = lax.fori_loop(0, D_MODEL // SUBLANES, body, (zeros, zeros), unroll=MOD_UNROLL)
    for j in range(MOD_TN // LANES):
        b = b_ref[:, j * LANES:(j + 1) * LANES]
        o_ref[0:1, j * LANES:(j + 1) * LANES] = jnp.sum(acc0[j], axis=0, keepdims=True) + b
        o_ref[1:2, j * LANES:(j + 1) * LANES] = jnp.sum(acc1[j], axis=0, keepdims=True) + b


def _mod(ct, w_mod, b_mod):
    n_out = N_MOD * D_MODEL
    return pl.pallas_call(
        _mod_kernel,
        grid=(n_out // MOD_TN,),
        in_specs=[
            pl.BlockSpec((D_MODEL, 2), lambda j: (0, 0)),
            pl.BlockSpec((D_MODEL, MOD_TN), lambda j: (0, j)),
            pl.BlockSpec((1, MOD_TN), lambda j: (0, j)),
        ],
        out_specs=pl.BlockSpec((2, MOD_TN), lambda j: (0, j)),
        out_shape=jax.ShapeDtypeStruct((2, n_out), F32),
        scratch_shapes=[pltpu.VMEM((D_MODEL, LANES), F32), pltpu.VMEM((D_MODEL, LANES), F32)],
        compiler_params=_cparams(("arbitrary",)),
        name="mod",
    )(ct, w_mod, b_mod)


IN_TM = 512
IN_SUB = 256
IN_TN = 512


def _rope(p, cos, sina, sinb):
    return (p * cos + pltpu.roll(p, HEAD_DIM - ROPE_AXIS_DIM // 2, axis=1) * sina
            + pltpu.roll(p, ROPE_AXIS_DIM // 2, axis=1) * sinb)


def _in_kernel(x_ref, mod_ref, w_ref, b_ref, trow_ref, tcol_ref, u_ref, q_ref, k_ref, v_ref):
    shift = mod_ref[0:1, :]
    scale = mod_ref[1:2, :]
    grid_rows = IN_SUB // GRID_W

    for sb in range(IN_TM // IN_SUB):
        rs = slice(sb * IN_SUB, (sb + 1) * IN_SUB)
        h = (_ln(x_ref[rs, :]) * (1.0 + scale) + shift).astype(BF16)

        def table(kind):
            by_row = jnp.concatenate(
                [jnp.broadcast_to(trow_ref[kind, sb * grid_rows + r:sb * grid_rows + r + 1, :],
                                  (GRID_W, HEAD_DIM)) for r in range(grid_rows)], axis=0)
            by_col = jnp.concatenate([tcol_ref[kind]] * grid_rows, axis=0)
            return by_row + by_col

        cos, sina, sinb = table(0), table(1), table(2)

        def proj(c0, width):
            return (jnp.dot(h, w_ref[:, c0:c0 + width], preferred_element_type=F32)
                    + b_ref[:, c0:c0 + width])

        for j in range(CONV_WIDTH // IN_TN):
            pv = proj(j * IN_TN, IN_TN)
            pg = proj(CONV_WIDTH + j * IN_TN, IN_TN)
            u_ref[rs, j * IN_TN:(j + 1) * IN_TN] = pv * jax.nn.sigmoid(pg)
        for j in range(ATTN_WIDTH // IN_TN):
            pq = proj(Q_START + j * IN_TN, IN_TN)
            for hh in range(IN_TN // HEAD_DIM):
                c0 = j * IN_TN + hh * HEAD_DIM
                q_ref[rs, c0:c0 + HEAD_DIM] = _rope(
                    pq[:, hh * HEAD_DIM:(hh + 1) * HEAD_DIM], cos, sina, sinb).astype(BF16)
        pk = proj(K_START, KV_WIDTH)
        for hh in range(N_KV_HEADS):
            k_ref[rs, hh * HEAD_DIM:(hh + 1) * HEAD_DIM] = _rope(
                pk[:, hh * HEAD_DIM:(hh + 1) * HEAD_DIM], cos, sina, sinb).astype(BF16)
        v_ref[rs, :] = proj(V_START, KV_WIDTH).astype(BF16)


def _in_proj(x2, mod6, w_in_bf, b_in, trow, tcol):
    n = x2.shape[0]
    row = lambda i: (i, 0)
    fixed = lambda i: (0, 0)
    return pl.pallas_call(
        _in_kernel,
        grid=(n // IN_TM,),
        in_specs=[
            pl.BlockSpec((IN_TM, D_MODEL), row),
            pl.BlockSpec((N_MOD, D_MODEL), fixed),
            pl.BlockSpec((D_MODEL, IN_COLS), fixed),
            pl.BlockSpec((1, IN_COLS), fixed),
            pl.BlockSpec((3, IN_TM // GRID_W, HEAD_DIM), lambda i: (0, i, 0)),
            pl.BlockSpec((3, GRID_W, HEAD_DIM), lambda i: (0, 0, 0)),
        ],
        out_specs=[
            pl.BlockSpec((IN_TM, CONV_WIDTH), row),
            pl.BlockSpec((IN_TM, ATTN_WIDTH), row),
            pl.BlockSpec((IN_TM, KV_WIDTH), row),
            pl.BlockSpec((IN_TM, KV_WIDTH), row),
        ],
        out_shape=[
            jax.ShapeDtypeStruct((n, CONV_WIDTH), F32),
            jax.ShapeDtypeStruct((n, ATTN_WIDTH), BF16),
            jax.ShapeDtypeStruct((n, KV_WIDTH), BF16),
            jax.ShapeDtypeStruct((n, KV_WIDTH), BF16),
        ],
        compiler_params=_cparams(("arbitrary",)),
        name="in_proj",
    )(x2, mod6, w_in_bf, b_in, trow, tcol)


def _ctx_kernel(x_ref, mod_ref, w_ref, b_ref, kc_ref, vc_ref):
    shift = mod_ref[0:1, :]
    scale = mod_ref[1:2, :]
    h = (_ln(x_ref[...]) * (1.0 + scale) + shift).astype(BF16)
    p = jnp.dot(h, w_ref[...], preferred_element_type=F32) + b_ref[...]
    kc_ref[...] = p[:, :KV_WIDTH].astype(BF16)
    vc_ref[...] = p[:, KV_WIDTH:].astype(BF16)


def _ctx_kv(ctx2, modc6, w_in_bf, b_in):
    kvw = 2 * KV_WIDTH
    fixed = lambda i: (0, 0)
    return pl.pallas_call(
        _ctx_kernel,
        grid=(1,),
        in_specs=[
            pl.BlockSpec((CTX_LEN, D_MODEL), fixed),
            pl.BlockSpec((N_MOD, D_MODEL), fixed),
            pl.BlockSpec((D_MODEL, kvw), lambda i: (0, K_START // kvw)),
            pl.BlockSpec((1, kvw), lambda i: (0, K_START // kvw)),
        ],
        out_specs=[pl.BlockSpec((CTX_LEN, KV_WIDTH), fixed), pl.BlockSpec((CTX_LEN, KV_WIDTH), fixed)],
        out_shape=[jax.ShapeDtypeStruct((CTX_LEN, KV_WIDTH), BF16),
                   jax.ShapeDtypeStruct((CTX_LEN, KV_WIDTH), BF16)],
        compiler_params=_cparams(("arbitrary",)),
        name="ctx_kv",
    )(ctx2, modc6, w_in_bf, b_in)


CONV_T = 512
CONV_HALO = 16
CONV_ROWS = 64


def _conv_kernel(up_ref, uc_ref, un_ref, w_ref, bdw_ref, g_ref, b_ref, o_ref, buf_ref, acc_ref):
    i = pl.program_id(0)
    last = pl.num_programs(0) - 1
    buf_ref[0:CONV_HALO, :] = jnp.where(i > 0, up_ref[...], 0.0)
    buf_ref[CONV_HALO:CONV_HALO + CONV_T, :] = uc_ref[...]
    buf_ref[CONV_HALO + CONV_T:, :] = jnp.where(i < last, un_ref[...], 0.0)
    off = CONV_HALO - CONV_KSIZE // 2

    span = CONV_ROWS + 2 * CONV_HALO

    def body(r, carry):
        r0 = pl.multiple_of(r * CONV_ROWS, CONV_ROWS)
        for lg in range(CONV_WIDTH // LANES):
            ls = slice(lg * LANES, (lg + 1) * LANES)
            blk = buf_ref[pl.ds(r0, span), ls]
            shifted = [blk] + [pltpu.roll(blk, span - s, axis=0) for s in range(1, SUBLANES)]
            acc = jnp.zeros((CONV_ROWS, LANES), F32)
            for t in range(CONV_KSIZE):
                d = off + t
                a0 = (d // SUBLANES) * SUBLANES
                acc = acc + shifted[d % SUBLANES][a0:a0 + CONV_ROWS, :] * w_ref[t:t + 1, ls]
            acc_ref[pl.ds(r0, CONV_ROWS), ls] = acc
        return carry

    lax.fori_loop(0, CONV_T // CONV_ROWS, body, 0)
    y = _ln(acc_ref[...] + bdw_ref[...]) * g_ref[...] + b_ref[...]
    o_ref[...] = (y * jax.nn.sigmoid(y)).astype(BF16)


def _conv(u, w_dw, b_dw, ln_g, ln_b):
    n = u.shape[0]
    hb = CONV_T // CONV_HALO
    nhb = n // CONV_HALO
    fixed = lambda i: (0, 0)
    return pl.pallas_call(
        _conv_kernel,
        grid=(n // CONV_T,),
        in_specs=[
            pl.BlockSpec((CONV_HALO, CONV_WIDTH), lambda i: (jnp.maximum(i * hb - 1, 0), 0)),
            pl.BlockSpec((CONV_T, CONV_WIDTH), lambda i: (i, 0)),
            pl.BlockSpec((CONV_HALO, CONV_WIDTH), lambda i: (jnp.minimum((i + 1) * hb, nhb - 1), 0)),
            pl.BlockSpec((CONV_KSIZE, CONV_WIDTH), fixed),
            pl.BlockSpec((1, CONV_WIDTH), fixed),
            pl.BlockSpec((1, CONV_WIDTH), fixed),
            pl.BlockSpec((1, CONV_WIDTH), fixed),
        ],
        out_specs=pl.BlockSpec((CONV_T, CONV_WIDTH), lambda i: (i, 0)),
        out_shape=jax.ShapeDtypeStruct((n, CONV_WIDTH), BF16),
        scratch_shapes=[pltpu.VMEM((CONV_T + 2 * CONV_HALO, CONV_WIDTH), F32),
                        pltpu.VMEM((CONV_T, CONV_WIDTH), F32)],
        compiler_params=_cparams(("arbitrary",)),
        name="conv",
    )(u, u, u, w_dw, b_dw, ln_g, ln_b)


ATT_T = 128
ATT_NB = 1
assert ATT_T == WINDOW
LOG2E = 1.4426950408889634


def _attn_kernel(sink_ref, q_ref, kp_ref, kc_ref, kn_ref, vp_ref, vc_ref, vn_ref, kx_ref, vx_ref, o_ref,
                 bias_ref):
    i = pl.program_id(0)
    last = pl.num_programs(0) - 1
    scale = HEAD_DIM ** -0.5
    rows = Q_PER_KV * ATT_T

    @pl.when((i <= 1) | (i == last))
    def _():
        qi = lax.broadcasted_iota(jnp.int32, (rows, 3 * ATT_T), 0) & (ATT_T - 1)
        m = lax.broadcasted_iota(jnp.int32, (rows, 3 * ATT_T), 1)
        band = jnp.abs(m - qi - ATT_T) <= WINDOW
        for b in range(ATT_NB):
            kpos = (i * ATT_NB + b - 1) * ATT_T + m
            bias_ref[b] = jnp.where(band & (kpos >= 0) & (kpos < SEQ), 0.0, NEG_INF)

    hrow = jnp.right_shift(lax.broadcasted_iota(jnp.int32, (rows, 1), 0), ATT_T.bit_length() - 1)
    nt = (((1,), (1,)), ((), ()))
    for b in range(ATT_NB):
        rs = slice(b * ATT_T, (b + 1) * ATT_T)
        for g in range(N_KV_HEADS):
            ls = slice(g * HEAD_DIM, (g + 1) * HEAD_DIM)

            def key_blocks(p_ref, c_ref, n_ref):
                blocks = ([p_ref[:, ls]] + [c_ref[j * ATT_T:(j + 1) * ATT_T, ls] for j in range(ATT_NB)]
                          + [n_ref[:, ls]])
                return jnp.concatenate(blocks[b:b + 3], axis=0)

            qs = jnp.concatenate(
                [q_ref[rs, (g * Q_PER_KV + hh) * HEAD_DIM:(g * Q_PER_KV + hh + 1) * HEAD_DIM]
                 for hh in range(Q_PER_KV)], axis=0)
            kw = key_blocks(kp_ref, kc_ref, kn_ref)
            vw = key_blocks(vp_ref, vc_ref, vn_ref)
            s_win = lax.dot_general(qs, kw, nt, preferred_element_type=F32) + bias_ref[b]
            s_ctx = lax.dot_general(qs, kx_ref[:, ls], nt, preferred_element_type=F32)
            s_sink = jnp.zeros((rows, 1), F32)
            for hh in range(Q_PER_KV):
                s_sink = jnp.where(hrow == hh, sink_ref[g * Q_PER_KV + hh], s_sink)
            raw_max = jnp.maximum(jnp.max(s_win, axis=-1, keepdims=True), jnp.max(s_ctx, axis=-1, keepdims=True))
            mx2 = jnp.maximum(raw_max * scale, s_sink) * LOG2E
            e_win = jnp.exp2(s_win * (scale * LOG2E) - mx2)
            e_ctx = jnp.exp2(s_ctx * (scale * LOG2E) - mx2)
            den = (jnp.sum(e_win, axis=-1, keepdims=True) + jnp.sum(e_ctx, axis=-1, keepdims=True)
                   + jnp.exp2(s_sink * LOG2E - mx2))
            o = (jnp.dot(e_ctx.astype(BF16), vx_ref[:, ls], preferred_element_type=F32)
                 + jnp.dot(e_win.astype(BF16), vw, preferred_element_type=F32)) * (1.0 / den)
            for hh in range(Q_PER_KV):
                c0 = (g * Q_PER_KV + hh) * HEAD_DIM
                o_ref[rs, c0:c0 + HEAD_DIM] = o[hh * ATT_T:(hh + 1) * ATT_T, :].astype(BF16)


def _attn(sink, q, k, v, kx, vx):
    n = q.shape[0]
    nb = n // ATT_T
    prev = lambda i, s: (jnp.maximum(i * ATT_NB - 1, 0), 0)
    cur = lambda i, s: (i, 0)
    nxt = lambda i, s: (jnp.minimum((i + 1) * ATT_NB, nb - 1), 0)
    fixed = lambda i, s: (0, 0)
    edge = lambda im: pl.BlockSpec((ATT_T, KV_WIDTH), im)
    own = pl.BlockSpec((ATT_NB * ATT_T, KV_WIDTH), cur)
    return pl.pallas_call(
        _attn_kernel,
        grid_spec=pltpu.PrefetchScalarGridSpec(
            num_scalar_prefetch=1,
            grid=(nb // ATT_NB,),
            in_specs=[
                pl.BlockSpec((ATT_NB * ATT_T, ATTN_WIDTH), cur),
                edge(prev), own, edge(nxt),
                edge(prev), own, edge(nxt),
                pl.BlockSpec((CTX_LEN, KV_WIDTH), fixed),
                pl.BlockSpec((CTX_LEN, KV_WIDTH), fixed),
            ],
            out_specs=pl.BlockSpec((ATT_NB * ATT_T, ATTN_WIDTH), cur),
            scratch_shapes=[pltpu.VMEM((ATT_NB, Q_PER_KV * ATT_T, 3 * ATT_T), F32)],
        ),
        out_shape=jax.ShapeDtypeStruct((n, ATTN_WIDTH), BF16),
        compiler_params=_cparams(("arbitrary",)),
        name="attn",
    )(sink, q, k, k, k, v, v, v, kx, vx)


OUT_TM = 512
OUT_SUB = 128


def _out_kernel(ac_ref, aa_ref, x_ref, mod_ref, w_ref, b_ref, g1_ref, b1_ref, wrh_ref, wrl_ref,
                xmid_ref, h2_ref, aff_ref):
    gate1 = mod_ref[2:3, :]
    for sb in range(OUT_TM // OUT_SUB):
        rs = slice(sb * OUT_SUB, (sb + 1) * OUT_SUB)
        mix = (jnp.dot(ac_ref[rs, :], w_ref[:CONV_WIDTH, :], preferred_element_type=F32)
               + jnp.dot(aa_ref[rs, :], w_ref[CONV_WIDTH:, :], preferred_element_type=F32) + b_ref[...])
        xmid = _ln(DEEPNORM_ALPHA * x_ref[rs, :] + gate1 * mix) * g1_ref[...] + b1_ref[...]
        xmid_ref[rs, :] = xmid
        h2 = _ln(xmid) * (1.0 + mod_ref[4:5, :]) + mod_ref[3:4, :]
        for kk in range(D_MODEL // LANES):
            hh, jj = divmod(kk, SUBLANES)
            h2_ref[hh, pl.ds(sb * OUT_SUB * SUBLANES + jj, OUT_SUB, stride=SUBLANES), :] = (
                h2[:, kk * LANES:(kk + 1) * LANES])
        h_hi = h2.astype(BF16)
        h_lo = (h2 - h_hi.astype(F32)).astype(BF16)
        logits = (jnp.dot(h_hi, wrh_ref[...], preferred_element_type=F32)
                  + jnp.dot(h_lo, wrh_ref[...], preferred_element_type=F32)
                  + jnp.dot(h_hi, wrl_ref[...], preferred_element_type=F32))
        logits = logits.T[:N_EXPERTS, :]
        mx = jnp.max(logits, axis=0, keepdims=True)
        ex = jnp.exp(logits - mx)
        aff_ref[:, rs] = ex / jnp.sum(ex, axis=0, keepdims=True)


def _out_proj(a_conv, a_attn, x2, mod6, w_out_bf, b_out, g1, b1, wr_hi, wr_lo):
    n = x2.shape[0]
    row = lambda i: (i, 0)
    fixed = lambda i: (0, 0)
    return pl.pallas_call(
        _out_kernel,
        grid=(n // OUT_TM,),
        in_specs=[
            pl.BlockSpec((OUT_TM, CONV_WIDTH), row),
            pl.BlockSpec((OUT_TM, ATTN_WIDTH), row),
            pl.BlockSpec((OUT_TM, D_MODEL), row),
            pl.BlockSpec((N_MOD, D_MODEL), fixed),
            pl.BlockSpec((D_MODEL, D_MODEL), fixed),
            pl.BlockSpec((1, D_MODEL), fixed),
            pl.BlockSpec((1, D_MODEL), fixed),
            pl.BlockSpec((1, D_MODEL), fixed),
            pl.BlockSpec((D_MODEL, LANES), fixed),
            pl.BlockSpec((D_MODEL, LANES), fixed),
        ],
        out_specs=[
            pl.BlockSpec((OUT_TM, D_MODEL), row),
            pl.BlockSpec((H2_HALVES, OUT_TM * SUBLANES, LANES), lambda i: (0, i, 0)),
            pl.BlockSpec((N_EXPERTS, OUT_TM), lambda i: (0, i)),
        ],
        out_shape=[
            jax.ShapeDtypeStruct((n, D_MODEL), F32),
            jax.ShapeDtypeStruct((H2_HALVES, n * SUBLANES, LANES), F32),
            jax.ShapeDtypeStruct((N_EXPERTS, n), F32),
        ],
        compiler_params=_cparams(("arbitrary",)),
        name="out_proj",
    )(a_conv, a_attn, x2, mod6, w_out_bf, b_out, g1, b1, wr_hi, wr_lo)


def _split3(a):
    a1 = a.astype(BF16)
    r = a - a1.astype(F32)
    a2 = r.astype(BF16)
    a3 = (r - a2.astype(F32)).astype(BF16)
    return a1, a2, a3


def _select_kernel(aff_ref, idx_ref, g_ref, base_ref, pos_ref, sel_ref):
    e = pl.program_id(0)

    @pl.when(e == 0)
    def _():
        aff_all = aff_ref[...]

        def count(mask):
            c = jnp.sum(mask.astype(F32), axis=2, keepdims=True)
            return jnp.sum(c, axis=1, keepdims=True)

        def bit_step(k, thr_bits):
            cand = thr_bits | jnp.left_shift(jnp.int32(1), 30 - k)
            ge = aff_all >= pltpu.bitcast(cand, F32)
            return jnp.where(count(ge) >= CAP, cand, thr_bits)

        thr_bits = lax.fori_loop(0, 31, bit_step, jnp.zeros((N_EXPERTS, 1, 1), jnp.int32))
        thr = pltpu.bitcast(thr_bits, F32)
        gt = aff_all > thr
        eq = aff_all == thr
        need = CAP - count(gt)
        eq2 = eq.astype(F32).reshape(N_EXPERTS * N_CHUNKS, CHUNK)
        tri = (lax.broadcasted_iota(jnp.int32, (CHUNK, CHUNK), 0)
               <= lax.broadcasted_iota(jnp.int32, (CHUNK, CHUNK), 1)).astype(BF16)
        incl = jnp.dot(eq2.astype(BF16), tri, preferred_element_type=F32)
        tot = jnp.broadcast_to(incl[:, CHUNK - 1:CHUNK], (N_EXPERTS * N_CHUNKS, LANES))
        rr = lax.broadcasted_iota(jnp.int32, (N_EXPERTS * N_CHUNKS, N_EXPERTS * N_CHUNKS), 0)
        cc = lax.broadcasted_iota(jnp.int32, (N_EXPERTS * N_CHUNKS, N_EXPERTS * N_CHUNKS), 1)
        cshift = N_CHUNKS.bit_length() - 1
        low = ((jnp.right_shift(rr, cshift) == jnp.right_shift(cc, cshift)) & (cc < rr)).astype(BF16)
        before = jnp.dot(low, tot.astype(BF16), preferred_element_type=F32)
        rank = (before + incl - eq2).reshape(N_EXPERTS, N_CHUNKS, CHUNK)
        sel_ref[...] = (gt | (eq & (rank < need))).astype(F32)

    sel = sel_ref[e]
    aff = aff_ref[e]
    tri = (lax.broadcasted_iota(jnp.int32, (CHUNK, CHUNK), 0)
           <= lax.broadcasted_iota(jnp.int32, (CHUNK, CHUNK), 1)).astype(BF16)
    incl = jnp.dot(sel.astype(BF16), tri, preferred_element_type=F32)
    tot = jnp.broadcast_to(incl[:, CHUNK - 1:CHUNK], (N_CHUNKS, LANES))
    pad = lambda a: jnp.concatenate([a, jnp.zeros((LANES - N_CHUNKS, LANES), a.dtype)], axis=0)
    low = (lax.broadcasted_iota(jnp.int32, (N_CHUNKS, LANES), 1)
           < lax.broadcasted_iota(jnp.int32, (N_CHUNKS, LANES), 0)).astype(BF16)
    base = jnp.dot(low, pad(tot.astype(BF16)), preferred_element_type=F32)
    base_ref[0] = base[:, 0:1].astype(jnp.int32)
    pos_ref[0] = jnp.where(sel > 0.0, base + incl - 1.0, -1.0).astype(jnp.int32)

    pick = (lax.broadcasted_iota(jnp.int32, (SUBLANES, LANES), 1) == 0).astype(BF16)
    nt = (((1,), (1,)), ((), ()))
    base_hi = jnp.floor(base * (1.0 / CHUNK))
    base_lo = base - base_hi * CHUNK
    row = lambda a: lax.dot_general(pick, pad(a.astype(BF16)), nt, preferred_element_type=F32)[0:1, :]
    base_row = row(base_hi) * CHUNK + row(base_lo)
    tot_row = row(tot)
    s_col = lax.broadcasted_iota(jnp.int32, (CAP, LANES), 0).astype(F32)
    owner = ((base_row <= s_col) & (s_col < base_row + tot_row)).astype(BF16)

    own = lambda a: jnp.dot(owner, pad(a.astype(BF16)), preferred_element_type=F32)
    incl_s = own(incl)
    base_s = own(base_hi) * CHUNK + own(base_lo)
    cvals = lax.broadcasted_iota(jnp.int32, (N_CHUNKS, LANES), 0).astype(F32)
    chunk_s = own(cvals)
    s_loc = lax.broadcasted_iota(jnp.int32, (CAP, LANES), 0).astype(F32) - base_s
    j_s = jnp.sum((incl_s <= s_loc).astype(F32), axis=1, keepdims=True)
    idx_ref[0] = (chunk_s[:, 0:1] * CHUNK + j_s).astype(jnp.int32)
    a1, a2, a3 = _split3(aff)
    aff_s = (own(a1) + own(a2)) + own(a3)
    lane = lax.broadcasted_iota(jnp.int32, (CAP, LANES), 1).astype(F32)
    g_ref[0] = jnp.sum(jnp.where(lane == j_s, aff_s, 0.0), axis=1, keepdims=True)


def _select(aff3):
    full = lambda e: (0, 0, 0)
    per = lambda e: (e, 0, 0)
    return pl.pallas_call(
        _select_kernel,
        grid=(N_EXPERTS,),
        in_specs=[pl.BlockSpec((N_EXPERTS, N_CHUNKS, CHUNK), full)],
        out_specs=[
            pl.BlockSpec((1, CAP, 1), per),
            pl.BlockSpec((1, CAP, 1), per),
            pl.BlockSpec((1, N_CHUNKS, 1), per),
            pl.BlockSpec((1, N_CHUNKS, CHUNK), per),
        ],
        out_shape=[
            jax.ShapeDtypeStruct((N_EXPERTS, CAP, 1), jnp.int32),
            jax.ShapeDtypeStruct((N_EXPERTS, CAP, 1), F32),
            jax.ShapeDtypeStruct((N_EXPERTS, N_CHUNKS, 1), jnp.int32),
            jax.ShapeDtypeStruct((N_EXPERTS, N_CHUNKS, CHUNK), jnp.int32),
        ],
        scratch_shapes=[pltpu.VMEM((N_EXPERTS, N_CHUNKS, CHUNK), F32)],
        compiler_params=_cparams(("arbitrary",)),
        name="select",
    )(aff3)


FFN_TF = 512
FFN_NF = D_FF // FFN_TF
FFN_TN = 256
FFN_GATHER_UNROLL = 8
FFN_VMEM_LIMIT = 60 * 1024 * 1024


def _ffn_kernel(idx_ref, h2_hbm, g_ref, wg_ref, wu_ref, wd_ref, y_ref, xs_ref, xb_ref, act_ref, acc_ref,
                sem):
    e = pl.program_id(0)
    f = pl.program_id(1)
    nf = pl.num_programs(1)

    def start_row(ee, s):
        t8 = pl.multiple_of(idx_ref[ee, s] * SUBLANES, SUBLANES)
        s8 = pl.multiple_of(s * SUBLANES, SUBLANES)
        pltpu.make_async_copy(h2_hbm.at[:, pl.ds(t8, SUBLANES), :],
                              xs_ref.at[:, pl.ds(s8, SUBLANES), :], sem).start()

    def wait_rows():
        pltpu.make_async_copy(h2_hbm.at[:, pl.ds(0, CAP * SUBLANES), :], xs_ref, sem).wait()

    @pl.when((e == 0) & (f == 0))
    def _():
        def body(s, carry):
            start_row(0, s)
            return carry
        lax.fori_loop(0, CAP, body, 0, unroll=FFN_GATHER_UNROLL)

    nxt = jnp.where(e + 1 < N_EXPERTS, e + 1, 0)
    per_step = CAP // FFN_NF

    @pl.when(f == 0)
    def _():
        wait_rows()
        for kk in range(D_MODEL // LANES):
            hh, jj = divmod(kk, SUBLANES)
            xb_ref[:, kk * LANES:(kk + 1) * LANES] = (
                xs_ref[hh, pl.ds(jj, CAP, stride=SUBLANES), :].astype(BF16))
        acc_ref[...] = jnp.zeros_like(acc_ref)
        for s in range(per_step * FFN_NF, CAP):
            start_row(nxt, s)

    for j in range(per_step):
        start_row(nxt, f * per_step + j)

    for c in range(FFN_TF // FFN_TN):
        cs = slice(c * FFN_TN, (c + 1) * FFN_TN)
        a = jnp.dot(xb_ref[...], wg_ref[0, :, cs].astype(BF16), preferred_element_type=F32)
        u = jnp.dot(xb_ref[...], wu_ref[0, :, cs].astype(BF16), preferred_element_type=F32)
        act_ref[:, cs] = (a * jax.nn.sigmoid(a) * u).astype(BF16)
    for n in range(D_MODEL // FFN_TN):
        ns = slice(n * FFN_TN, (n + 1) * FFN_TN)
        acc_ref[:, ns] += jnp.dot(act_ref[...], wd_ref[0, :, ns].astype(BF16), preferred_element_type=F32)

    @pl.when(f == nf - 1)
    def _():
        y_ref[0] = (acc_ref[...] * g_ref[0]).astype(BF16)

    @pl.when((e == N_EXPERTS - 1) & (f == nf - 1))
    def _():
        wait_rows()


def _ffn(idx, h2, g, w_gate, w_up, w_down):
    return pl.pallas_call(
        _ffn_kernel,
        grid_spec=pltpu.PrefetchScalarGridSpec(
            num_scalar_prefetch=1,
            grid=(N_EXPERTS, D_FF // FFN_TF),
            in_specs=[
                pl.BlockSpec(memory_space=pl.ANY),
                pl.BlockSpec((1, CAP, 1), lambda e, f, idx: (e, 0, 0)),
                pl.BlockSpec((1, D_MODEL, FFN_TF), lambda e, f, idx: (e, 0, f)),
                pl.BlockSpec((1, D_MODEL, FFN_TF), lambda e, f, idx: (e, 0, f)),
                pl.BlockSpec((1, FFN_TF, D_MODEL), lambda e, f, idx: (e, f, 0)),
            ],
            out_specs=pl.BlockSpec((1, CAP, D_MODEL), lambda e, f, idx: (e, 0, 0)),
            scratch_shapes=[
                pltpu.VMEM((H2_HALVES, CAP * SUBLANES, LANES), F32),
                pltpu.VMEM((CAP, D_MODEL), BF16),
                pltpu.VMEM((CAP, FFN_TF), BF16),
                pltpu.VMEM((CAP, D_MODEL), F32),
                pltpu.SemaphoreType.DMA(()),
            ],
        ),
        out_shape=jax.ShapeDtypeStruct((N_EXPERTS, CAP, D_MODEL), BF16),
        compiler_params=_cparams(("arbitrary", "arbitrary"), FFN_VMEM_LIMIT),
        name="ffn",
    )(idx, h2, g, w_gate, w_up, w_down)


CMB_T = CHUNK
CMB_K = 256
CMB_GRAN = 16
CMB_FIRST = 48
CMB_MAIN = N_EXPERTS * CMB_FIRST
CMB_SIZES = (64, 32, 16)
CMB_OVF = N_EXPERTS * sum(CMB_SIZES)
assert CMB_MAIN % CMB_K == 0 and CMB_OVF % CMB_K == 0
assert CMB_T + CMB_GRAN <= CMB_FIRST + sum(CMB_SIZES)


def _combine_kernel(base_ref, pos_ref, xmid_ref, mod_ref, g2_ref, b2_ref, y_hbm, o_ref,
                    stage_ref, ovf_ref, spread_ref, rcol_ref, acc_ref, sem, ovf_sem):
    i = pl.program_id(0)
    slot = i & 1
    y_rows = N_EXPERTS * CAP

    def windows(tile):
        starts, extra = [], []
        for e in range(N_EXPERTS):
            s0 = e * CAP + base_ref[e * (N_CHUNKS + 1) + tile]
            s1 = e * CAP + base_ref[e * (N_CHUNKS + 1) + tile + 1]
            start = jnp.minimum(s0 & -CMB_GRAN, y_rows - CMB_FIRST)
            starts.append(start)
            extra.append(jnp.maximum(s1 - (start + CMB_FIRST) + CMB_GRAN - 1, 0) & -CMB_GRAN)
        return starts, extra

    def stage_main(tile, slot_):
        for e, start in enumerate(windows(tile)[0]):
            pltpu.make_async_copy(
                y_hbm.at[pl.ds(pl.multiple_of(start, CMB_GRAN), CMB_FIRST)],
                stage_ref.at[slot_, pl.ds(e * CMB_FIRST, CMB_FIRST)], sem.at[slot_]).start()

    @pl.when(i == 0)
    def _():
        stage_ref[...] = jnp.zeros_like(stage_ref)
        ovf_ref[...] = jnp.zeros_like(ovf_ref)
        ee = lax.broadcasted_iota(jnp.int32, (LANES, CMB_MAIN), 0)
        cc = lax.broadcasted_iota(jnp.int32, (LANES, CMB_MAIN), 1)
        lo = ee * CMB_FIRST
        spread_ref[...] = ((cc >= lo) & (cc < lo + CMB_FIRST)).astype(F32).astype(BF16)
        c1 = lax.broadcasted_iota(jnp.int32, (SUBLANES, CMB_MAIN), 1)
        owner = jnp.zeros((SUBLANES, CMB_MAIN), jnp.int32)
        for e in range(1, N_EXPERTS):
            owner = owner + (c1 >= e * CMB_FIRST).astype(jnp.int32)
        rcol_ref[...] = (c1 - owner * CMB_FIRST).astype(F32)
        stage_main(0, 0)

    @pl.when(i + 1 < pl.num_programs(0))
    def _():
        stage_main(i + 1, 1 - slot)

    starts, extra = windows(i)
    pltpu.make_async_copy(y_hbm.at[pl.ds(0, CMB_MAIN)], stage_ref.at[slot], sem.at[slot]).wait()

    lane = lax.broadcasted_iota(jnp.int32, (1, LANES), 1)
    shift = jnp.zeros((1, LANES), jnp.int32)
    for e in range(N_EXPERTS):
        shift = jnp.where(lane == e, e * CAP - starts[e], shift)
    pos = pos_ref[...]
    rel = jnp.where(pos >= 0, pos + shift, -1)
    spread = jnp.dot(rel.astype(F32).astype(BF16), spread_ref[...], preferred_element_type=F32)
    for k in range(CMB_MAIN // CMB_K):
        ks = slice(k * CMB_K, (k + 1) * CMB_K)
        w = (spread[:, ks] == rcol_ref[0:1, ks]).astype(F32).astype(BF16)
        part = jnp.dot(w, stage_ref[slot, ks, :], preferred_element_type=F32)
        if k == 0:
            acc_ref[...] = part
        else:
            acc_ref[...] += part

    n_extra = extra[0]
    for e in range(1, N_EXPERTS):
        n_extra = n_extra + extra[e]

    @pl.when(n_extra > 0)
    def _():
        copies, rowidx = [], []
        ooff = jnp.int32(0)
        for e in range(N_EXPERTS):
            done = jnp.int32(0)
            for size in CMB_SIZES:
                pred = (extra[e] & size) != 0
                src = pl.multiple_of(starts[e] + CMB_FIRST + done, CMB_GRAN)
                dst = pl.multiple_of(ooff + done, CMB_GRAN)
                copies.append((pred, pltpu.make_async_copy(
                    y_hbm.at[pl.ds(src, size)], ovf_ref.at[pl.ds(dst, size)], ovf_sem)))
                done = done + jnp.where(pred, size, 0)
            r = rel[:, e:e + 1]
            rowidx.append(jnp.where(r >= CMB_FIRST, r - CMB_FIRST + ooff, -1))
            ooff = ooff + extra[e]
        for pred, cp in copies:
            @pl.when(pred)
            def _():
                cp.start()
        for pred, cp in copies:
            @pl.when(pred)
            def _():
                cp.wait()

        def body(k, carry):
            r0 = pl.multiple_of(k * CMB_K, CMB_K)
            col = lax.broadcasted_iota(jnp.int32, (CMB_T, CMB_K), 1) + r0
            w = jnp.zeros((CMB_T, CMB_K), F32)
            for e in range(N_EXPERTS):
                w = w + (rowidx[e] == col).astype(F32)
            acc_ref[...] += jnp.dot(w.astype(BF16), ovf_ref[pl.ds(r0, CMB_K), :],
                                    preferred_element_type=F32)
            return carry

        lax.fori_loop(0, (n_extra + CMB_K - 1) // CMB_K, body, 0)

    gate2 = mod_ref[5:6, :]
    o_ref[...] = _ln(DEEPNORM_ALPHA * xmid_ref[...] + gate2 * acc_ref[...]) * g2_ref[...] + b2_ref[...]


def _combine(base_flat, pos_te, xmid, mod6, g2, b2, y_flat):
    n = xmid.shape[0]
    row = lambda i, b: (i, 0)
    fixed = lambda i, b: (0, 0)
    return pl.pallas_call(
        _combine_kernel,
        grid_spec=pltpu.PrefetchScalarGridSpec(
            num_scalar_prefetch=1,
            grid=(n // CMB_T,),
            in_specs=[
                pl.BlockSpec((CMB_T, LANES), row),
                pl.BlockSpec((CMB_T, D_MODEL), row),
                pl.BlockSpec((N_MOD, D_MODEL), fixed),
                pl.BlockSpec((1, D_MODEL), fixed),
                pl.BlockSpec((1, D_MODEL), fixed),
                pl.BlockSpec(memory_space=pl.ANY),
            ],
            out_specs=pl.BlockSpec((CMB_T, D_MODEL), row),
            scratch_shapes=[
                pltpu.VMEM((2, CMB_MAIN, D_MODEL), BF16),
                pltpu.VMEM((CMB_OVF, D_MODEL), BF16),
                pltpu.VMEM((LANES, CMB_MAIN), BF16),
                pltpu.VMEM((SUBLANES, CMB_MAIN), F32),
                pltpu.VMEM((CMB_T, D_MODEL), F32),
                pltpu.SemaphoreType.DMA((2,)),
                pltpu.SemaphoreType.DMA(()),
            ],
        ),
        out_shape=jax.ShapeDtypeStruct((n, D_MODEL), F32),
        compiler_params=_cparams(("arbitrary",)),
        name="combine",
    )(base_flat, pos_te, xmid, mod6, g2, b2, y_flat)


def _rope_tables(n):
    rows = n // GRID_W
    inv = ROPE_THETA ** (-np.arange(0, ROPE_AXIS_DIM, 2, dtype=np.float64) / ROPE_AXIS_DIM)
    ang_r = np.arange(rows, dtype=np.float64)[:, None] * inv[None, :]
    ang_c = np.arange(GRID_W, dtype=np.float64)[:, None] * inv[None, :]
    zr, zc = np.zeros_like(ang_r), np.zeros_like(ang_c)
    cr, sr, cc, sc = np.cos(ang_r), np.sin(ang_r), np.cos(ang_c), np.sin(ang_c)
    trow = np.stack([np.concatenate(p, axis=1) for p in
                     ([cr, cr, zr, zr], [-sr, zr, zr, zr], [zr, sr, zr, zr])])
    tcol = np.stack([np.concatenate(p, axis=1) for p in
                     ([zc, zc, cc, cc], [zc, zc, -sc, zc], [zc, zc, zc, sc])])
    return jnp.asarray(trow, F32), jnp.asarray(tcol, F32)


def kernel(x, c, ctx, c_ctx, w_mod, b_mod, w_in, b_in, w_dw, b_dw, conv_ln_g, conv_ln_b, sink,
           w_out, b_out, ln1_g, ln1_b, w_router, w_gate, w_up, w_down, ln2_g, ln2_b):
    assert x.shape == (1, SEQ, D_MODEL) and ctx.shape == (1, CTX_LEN, D_MODEL)
    assert w_mod.shape[0] == DEPTH
    x2 = x[0]
    ctx2 = ctx[0]
    r1 = lambda a: a.reshape(1, -1)

    ct = jnp.stack([c[0], c_ctx], axis=1)
    mod = _mod(ct, w_mod[0], r1(b_mod[0]))
    mod6 = mod[0].reshape(N_MOD, D_MODEL)
    modc6 = mod[1].reshape(N_MOD, D_MODEL)

    w_in_bf = w_in[0].astype(BF16)
    w_out_bf = w_out[0].astype(BF16)
    trow, tcol = _rope_tables(SEQ)
    u, q, k, v = _in_proj(x2, mod6, w_in_bf, r1(b_in[0]), trow, tcol)
    kx, vx = _ctx_kv(ctx2, modc6, w_in_bf, r1(b_in[0]))
    a_conv = _conv(u, w_dw[0], r1(b_dw[0]), r1(conv_ln_g[0]), r1(conv_ln_b[0]))
    a_attn = _attn(sink[0], q, k, v, kx, vx)
    wr = jnp.pad(w_router[0], ((0, 0), (0, LANES - N_EXPERTS)))
    wr_hi = wr.astype(BF16)
    wr_lo = (wr - wr_hi.astype(F32)).astype(BF16)
    xmid, h2, aff_t = _out_proj(a_conv, a_attn, x2, mod6, w_out_bf, r1(b_out[0]),
                                r1(ln1_g[0]), r1(ln1_b[0]), wr_hi, wr_lo)

    idx, g, base, pos = _select(aff_t.reshape(N_EXPERTS, N_CHUNKS, CHUNK))
    y = _ffn(idx.reshape(N_EXPERTS, CAP), h2, g, w_gate[0], w_up[0], w_down[0])

    base_flat = jnp.concatenate(
        [base.reshape(N_EXPERTS, N_CHUNKS), jnp.full((N_EXPERTS, 1), CAP, jnp.int32)], axis=1).reshape(-1)
    pos_te = jnp.pad(pos.reshape(N_EXPERTS, SEQ).T, ((0, 0), (0, LANES - N_EXPERTS)),
                     constant_values=-1)
    out = _combine(base_flat, pos_te, xmid, mod6, r1(ln2_g[0]), r1(ln2_b[0]),
                   y.reshape(N_EXPERTS * CAP, D_MODEL))
    return out[None]
```

```python
import numpy as np

import jax
import jax.numpy as jnp
from jax import lax
from jax.experimental import pallas as pl
from jax.experimental.pallas import tpu as pltpu

D_MODEL = 2048
SEQ = 8192
GRID_W = 64
CTX_LEN = 256
HEAD_DIM = 128
N_Q_HEADS = 8
N_KV_HEADS = 2
Q_PER_KV = N_Q_HEADS // N_KV_HEADS
ATTN_WIDTH = N_Q_HEADS * HEAD_DIM
KV_WIDTH = N_KV_HEADS * HEAD_DIM
CONV_WIDTH = D_MODEL - ATTN_WIDTH
CONV_KSIZE = 31
WINDOW = 128
ROPE_THETA = 10000.0
ROPE_AXIS_DIM = HEAD_DIM // 2
N_EXPERTS = 16
EC_CAPACITY = 2
CAP = EC_CAPACITY * SEQ // N_EXPERTS
D_FF = 5632
N_MOD = 6
LN_EPS = 1e-5
NEG_INF = -1e30
DEPTH = 1
DEEPNORM_ALPHA = (2.0 * DEPTH) ** 0.25
Q_START = 2 * CONV_WIDTH
K_START = Q_START + ATTN_WIDTH
V_START = K_START + KV_WIDTH
IN_COLS = V_START + KV_WIDTH

LANES = 128
SUBLANES = 8
H2_HALVES = D_MODEL // (SUBLANES * LANES)
CHUNK = LANES
N_CHUNKS = SEQ // CHUNK
VMEM_LIMIT = 56 * 1024 * 1024

F32 = jnp.float32
BF16 = jnp.bfloat16


def _ln(xv):
    mu = jnp.mean(xv, axis=-1, keepdims=True)
    xc = xv - mu
    var = jnp.mean(xc * xc, axis=-1, keepdims=True)
    return xc * lax.rsqrt(var + LN_EPS)


def _cparams(sem, vmem=VMEM_LIMIT):
    return pltpu.CompilerParams(dimension_semantics=sem, vmem_limit_bytes=vmem)


MOD_TN = 1024
MOD_UNROLL = 4


def _mod_kernel(ct_ref, w_ref, b_ref, o_ref, s0_ref, s1_ref):
    @pl.when(pl.program_id(0) == 0)
    def _():
        ct = ct_ref[...]
        s = ct * jax.nn.sigmoid(ct)
        s0_ref[...] = jnp.broadcast_to(s[:, 0:1], (D_MODEL, LANES))
        s1_ref[...] = jnp.broadcast_to(s[:, 1:2], (D_MODEL, LANES))

    def body(kb, acc):
        k0 = pl.multiple_of(kb * SUBLANES, SUBLANES)
        s0 = s0_ref[pl.ds(k0, SUBLANES), :]
        s1 = s1_ref[pl.ds(k0, SUBLANES), :]
        a0, a1 = [], []
        for j in range(MOD_TN // LANES):
            w = w_ref[pl.ds(k0, SUBLANES), j * LANES:(j + 1) * LANES]
            a0.append(acc[0][j] + w * s0)
            a1.append(acc[1][j] + w * s1)
        return tuple(a0), tuple(a1)

    zeros = tuple(jnp.zeros((SUBLANES, LANES), F32) for _ in range(MOD_TN // LANES))
    acc0, acc1 = lax.fori_loop(0, D_MODEL // SUBLANES, body, (zeros, zeros), unroll=MOD_UNROLL)
    for j in range(MOD_TN // LANES):
        b = b_ref[:, j * LANES:(j + 1) * LANES]
        o_ref[0:1, j * LANES:(j + 1) * LANES] = jnp.sum(acc0[j], axis=0, keepdims=True) + b
        o_ref[1:2, j * LANES:(j + 1) * LANES] = jnp.sum(acc1[j], axis=0, keepdims=True) + b


def _mod(ct, w_mod, b_mod):
    n_out = N_MOD * D_MODEL
    return pl.pallas_call(
        _mod_kernel,
        grid=(n_out // MOD_TN,),
        in_specs=[
            pl.BlockSpec((D_MODEL, 2), lambda j: (0, 0)),
            pl.BlockSpec((D_MODEL, MOD_TN), lambda j: (0, j)),
            pl.BlockSpec((1, MOD_TN), lambda j: (0, j)),
        ],
        out_specs=pl.BlockSpec((2, MOD_TN), lambda j: (0, j)),
        out_shape=jax.ShapeDtypeStruct((2, n_out), F32),
        scratch_shapes=[pltpu.VMEM((D_MODEL, LANES), F32), pltpu.VMEM((D_MODEL, LANES), F32)],
        compiler_params=_cparams(("arbitrary",)),
        name="mod",
    )(ct, w_mod, b_mod)


IN_TM = 512
IN_SUB = 256
IN_TN = 512


def _rope(p, cos, sina, sinb):
    return (p * cos + pltpu.roll(p, HEAD_DIM - ROPE_AXIS_DIM // 2, axis=1) * sina
            + pltpu.roll(p, ROPE_AXIS_DIM // 2, axis=1) * sinb)


def _in_kernel(x_ref, mod_ref, w_ref, b_ref, trow_ref, tcol_ref, u_ref, q_ref, k_ref, v_ref):
    shift = mod_ref[0:1, :]
    scale = mod_ref[1:2, :]
    grid_rows = IN_SUB // GRID_W

    for sb in range(IN_TM // IN_SUB):
        rs = slice(sb * IN_SUB, (sb + 1) * IN_SUB)
        h = (_ln(x_ref[rs, :]) * (1.0 + scale) + shift).astype(BF16)

        def table(kind):
            by_row = jnp.concatenate(
                [jnp.broadcast_to(trow_ref[kind, sb * grid_rows + r:sb * grid_rows + r + 1, :],
                                  (GRID_W, HEAD_DIM)) for r in range(grid_rows)], axis=0)
            by_col = jnp.concatenate([tcol_ref[kind]] * grid_rows, axis=0)
            return by_row + by_col

        cos, sina, sinb = table(0), table(1), table(2)

        def proj(c0, width):
            return (jnp.dot(h, w_ref[:, c0:c0 + width], preferred_element_type=F32)
                    + b_ref[:, c0:c0 + width])

        for j in range(CONV_WIDTH // IN_TN):
            pv = proj(j * IN_TN, IN_TN)
            pg = proj(CONV_WIDTH + j * IN_TN, IN_TN)
            u_ref[rs, j * IN_TN:(j + 1) * IN_TN] = pv * jax.nn.sigmoid(pg)
        for j in range(ATTN_WIDTH // IN_TN):
            pq = proj(Q_START + j * IN_TN, IN_TN)
            for hh in range(IN_TN // HEAD_DIM):
                c0 = j * IN_TN + hh * HEAD_DIM
                q_ref[rs, c0:c0 + HEAD_DIM] = _rope(
                    pq[:, hh * HEAD_DIM:(hh + 1) * HEAD_DIM], cos, sina, sinb).astype(BF16)
        pk = proj(K_START, KV_WIDTH)
        for hh in range(N_KV_HEADS):
            k_ref[rs, hh * HEAD_DIM:(hh + 1) * HEAD_DIM] = _rope(
                pk[:, hh * HEAD_DIM:(hh + 1) * HEAD_DIM], cos, sina, sinb).astype(BF16)
        v_ref[rs, :] = proj(V_START, KV_WIDTH).astype(BF16)


def _in_proj(x2, mod6, w_in_bf, b_in, trow, tcol):
    n = x2.shape[0]
    row = lambda i: (i, 0)
    fixed = lambda i: (0, 0)
    return pl.pallas_call(
        _in_kernel,
        grid=(n // IN_TM,),
        in_specs=[
            pl.BlockSpec((IN_TM, D_MODEL), row),
            pl.BlockSpec((N_MOD, D_MODEL), fixed),
            pl.BlockSpec((D_MODEL, IN_COLS), fixed),
            pl.BlockSpec((1, IN_COLS), fixed),
            pl.BlockSpec((3, IN_TM // GRID_W, HEAD_DIM), lambda i: (0, i, 0)),
            pl.BlockSpec((3, GRID_W, HEAD_DIM), lambda i: (0, 0, 0)),
        ],
        out_specs=[
            pl.BlockSpec((IN_TM, CONV_WIDTH), row),
            pl.BlockSpec((IN_TM, ATTN_WIDTH), row),
            pl.BlockSpec((IN_TM, KV_WIDTH), row),
            pl.BlockSpec((IN_TM, KV_WIDTH), row),
        ],
        out_shape=[
            jax.ShapeDtypeStruct((n, CONV_WIDTH), F32),
            jax.ShapeDtypeStruct((n, ATTN_WIDTH), BF16),
            jax.ShapeDtypeStruct((n, KV_WIDTH), BF16),
            jax.ShapeDtypeStruct((n, KV_WIDTH), BF16),
        ],
        compiler_params=_cparams(("arbitrary",)),
        name="in_proj",
    )(x2, mod6, w_in_bf, b_in, trow, tcol)


def _ctx_kernel(x_ref, mod_ref, w_ref, b_ref, kc_ref, vc_ref):
    shift = mod_ref[0:1, :]
    scale = mod_ref[1:2, :]
    h = (_ln(x_ref[...]) * (1.0 + scale) + shift).astype(BF16)
    p = jnp.dot(h, w_ref[...], preferred_element_type=F32) + b_ref[...]
    kc_ref[...] = p[:, :KV_WIDTH].astype(BF16)
    vc_ref[...] = p[:, KV_WIDTH:].astype(BF16)


def _ctx_kv(ctx2, modc6, w_in_bf, b_in):
    kvw = 2 * KV_WIDTH
    fixed = lambda i: (0, 0)
    return pl.pallas_call(
        _ctx_kernel,
        grid=(1,),
        in_specs=[
            pl.BlockSpec((CTX_LEN, D_MODEL), fixed),
            pl.BlockSpec((N_MOD, D_MODEL), fixed),
            pl.BlockSpec((D_MODEL, kvw), lambda i: (0, K_START // kvw)),
            pl.BlockSpec((1, kvw), lambda i: (0, K_START // kvw)),
        ],
        out_specs=[pl.BlockSpec((CTX_LEN, KV_WIDTH), fixed), pl.BlockSpec((CTX_LEN, KV_WIDTH), fixed)],
        out_shape=[jax.ShapeDtypeStruct((CTX_LEN, KV_WIDTH), BF16),
                   jax.ShapeDtypeStruct((CTX_LEN, KV_WIDTH), BF16)],
        compiler_params=_cparams(("arbitrary",)),
        name="ctx_kv",
    )(ctx2, modc6, w_in_bf, b_in)


CONV_T = 512
CONV_HALO = 16
CONV_ROWS = 64


def _conv_kernel(up_ref, uc_ref, un_ref, w_ref, bdw_ref, g_ref, b_ref, o_ref, buf_ref, acc_ref):
    i = pl.program_id(0)
    last = pl.num_programs(0) - 1
    buf_ref[0:CONV_HALO, :] = jnp.where(i > 0, up_ref[...], 0.0)
    buf_ref[CONV_HALO:CONV_HALO + CONV_T, :] = uc_ref[...]
    buf_ref[CONV_HALO + CONV_T:, :] = jnp.where(i < last, un_ref[...], 0.0)
    off = CONV_HALO - CONV_KSIZE // 2

    span = CONV_ROWS + 2 * CONV_HALO

    def body(r, carry):
        r0 = pl.multiple_of(r * CONV_ROWS, CONV_ROWS)
        for lg in range(CONV_WIDTH // LANES):
            ls = slice(lg * LANES, (lg + 1) * LANES)
            blk = buf_ref[pl.ds(r0, span), ls]
            shifted = [blk] + [pltpu.roll(blk, span - s, axis=0) for s in range(1, SUBLANES)]
            acc = jnp.zeros((CONV_ROWS, LANES), F32)
            for t in range(CONV_KSIZE):
                d = off + t
                a0 = (d // SUBLANES) * SUBLANES
                acc = acc + shifted[d % SUBLANES][a0:a0 + CONV_ROWS, :] * w_ref[t:t + 1, ls]
            acc_ref[pl.ds(r0, CONV_ROWS), ls] = acc
        return carry

    lax.fori_loop(0, CONV_T // CONV_ROWS, body, 0)
    y = _ln(acc_ref[...] + bdw_ref[...]) * g_ref[...] + b_ref[...]
    o_ref[...] = (y * jax.nn.sigmoid(y)).astype(BF16)


def _conv(u, w_dw, b_dw, ln_g, ln_b):
    n = u.shape[0]
    hb = CONV_T // CONV_HALO
    nhb = n // CONV_HALO
    fixed = lambda i: (0, 0)
    return pl.pallas_call(
        _conv_kernel,
        grid=(n // CONV_T,),
        in_specs=[
            pl.BlockSpec((CONV_HALO, CONV_WIDTH), lambda i: (jnp.maximum(i * hb - 1, 0), 0)),
            pl.BlockSpec((CONV_T, CONV_WIDTH), lambda i: (i, 0)),
            pl.BlockSpec((CONV_HALO, CONV_WIDTH), lambda i: (jnp.minimum((i + 1) * hb, nhb - 1), 0)),
            pl.BlockSpec((CONV_KSIZE, CONV_WIDTH), fixed),
            pl.BlockSpec((1, CONV_WIDTH), fixed),
            pl.BlockSpec((1, CONV_WIDTH), fixed),
            pl.BlockSpec((1, CONV_WIDTH), fixed),
        ],
        out_specs=pl.BlockSpec((CONV_T, CONV_WIDTH), lambda i: (i, 0)),
        out_shape=jax.ShapeDtypeStruct((n, CONV_WIDTH), BF16),
        scratch_shapes=[pltpu.VMEM((CONV_T + 2 * CONV_HALO, CONV_WIDTH), F32),
                        pltpu.VMEM((CONV_T, CONV_WIDTH), F32)],
        compiler_params=_cparams(("arbitrary",)),
        name="conv",
    )(u, u, u, w_dw, b_dw, ln_g, ln_b)


ATT_T = 128
ATT_NB = 2
assert ATT_T == WINDOW
LOG2E = 1.4426950408889634


def _attn_kernel(sink_ref, q_ref, kp_ref, kc_ref, kn_ref, vp_ref, vc_ref, vn_ref, kx_ref, vx_ref, o_ref,
                 bias_ref):
    i = pl.program_id(0)
    last = pl.num_programs(0) - 1
    scale = HEAD_DIM ** -0.5
    rows = Q_PER_KV * ATT_T

    @pl.when((i <= 1) | (i == last))
    def _():
        qi = lax.broadcasted_iota(jnp.int32, (rows, 3 * ATT_T), 0) & (ATT_T - 1)
        m = lax.broadcasted_iota(jnp.int32, (rows, 3 * ATT_T), 1)
        band = jnp.abs(m - qi - ATT_T) <= WINDOW
        for b in range(ATT_NB):
            kpos = (i * ATT_NB + b - 1) * ATT_T + m
            bias_ref[b] = jnp.where(band & (kpos >= 0) & (kpos < SEQ), 0.0, NEG_INF)

    hrow = jnp.right_shift(lax.broadcasted_iota(jnp.int32, (rows, 1), 0), ATT_T.bit_length() - 1)
    nt = (((1,), (1,)), ((), ()))
    def key_blocks(b, ls, p_ref, c_ref, n_ref):
        blocks = ([p_ref[:, ls]] + [c_ref[j * ATT_T:(j + 1) * ATT_T, ls] for j in range(ATT_NB)]
                  + [n_ref[:, ls]])
        return jnp.concatenate(blocks[b:b + 3], axis=0)

    def scores(b, g):
        rs = slice(b * ATT_T, (b + 1) * ATT_T)
        ls = slice(g * HEAD_DIM, (g + 1) * HEAD_DIM)
        qs = jnp.concatenate(
            [q_ref[rs, (g * Q_PER_KV + hh) * HEAD_DIM:(g * Q_PER_KV + hh + 1) * HEAD_DIM]
             for hh in range(Q_PER_KV)], axis=0)
        kw = key_blocks(b, ls, kp_ref, kc_ref, kn_ref)
        s_win = lax.dot_general(qs, kw, nt, preferred_element_type=F32) + bias_ref[b]
        s_ctx = lax.dot_general(qs, kx_ref[:, ls], nt, preferred_element_type=F32)
        return s_win, s_ctx

    def weights(g, s_win, s_ctx):
        s_sink = jnp.zeros((rows, 1), F32)
        for hh in range(Q_PER_KV):
            s_sink = jnp.where(hrow == hh, sink_ref[g * Q_PER_KV + hh], s_sink)
        raw_max = jnp.maximum(jnp.max(s_win, axis=-1, keepdims=True), jnp.max(s_ctx, axis=-1, keepdims=True))
        mx2 = jnp.maximum(raw_max * scale, s_sink) * LOG2E
        e_win = jnp.exp2(s_win * (scale * LOG2E) - mx2)
        e_ctx = jnp.exp2(s_ctx * (scale * LOG2E) - mx2)
        den = (jnp.sum(e_win, axis=-1, keepdims=True) + jnp.sum(e_ctx, axis=-1, keepdims=True)
               + jnp.exp2(s_sink * LOG2E - mx2))
        return e_win.astype(BF16), e_ctx.astype(BF16), den

    def values(b, g, e_win, e_ctx, den):
        rs = slice(b * ATT_T, (b + 1) * ATT_T)
        ls = slice(g * HEAD_DIM, (g + 1) * HEAD_DIM)
        vw = key_blocks(b, ls, vp_ref, vc_ref, vn_ref)
        o = (jnp.dot(e_ctx, vx_ref[:, ls], preferred_element_type=F32)
             + jnp.dot(e_win, vw, preferred_element_type=F32)) * (1.0 / den)
        for hh in range(Q_PER_KV):
            c0 = (g * Q_PER_KV + hh) * HEAD_DIM
            o_ref[rs, c0:c0 + HEAD_DIM] = o[hh * ATT_T:(hh + 1) * ATT_T, :].astype(BF16)

    chains = [(b, g) for b in range(ATT_NB) for g in range(N_KV_HEADS)]
    s_next = scores(*chains[0])
    for c, (b, g) in enumerate(chains):
        s_cur = s_next
        if c + 1 < len(chains):
            s_next = scores(*chains[c + 1])
        values(b, g, *weights(g, *s_cur))


def _attn(sink, q, k, v, kx, vx):
    n = q.shape[0]
    nb = n // ATT_T
    prev = lambda i, s: (jnp.maximum(i * ATT_NB - 1, 0), 0)
    cur = lambda i, s: (i, 0)
    nxt = lambda i, s: (jnp.minimum((i + 1) * ATT_NB, nb - 1), 0)
    fixed = lambda i, s: (0, 0)
    edge = lambda im: pl.BlockSpec((ATT_T, KV_WIDTH), im)
    own = pl.BlockSpec((ATT_NB * ATT_T, KV_WIDTH), cur)
    return pl.pallas_call(
        _attn_kernel,
        grid_spec=pltpu.PrefetchScalarGridSpec(
            num_scalar_prefetch=1,
            grid=(nb // ATT_NB,),
            in_specs=[
                pl.BlockSpec((ATT_NB * ATT_T, ATTN_WIDTH), cur),
                edge(prev), own, edge(nxt),
                edge(prev), own, edge(nxt),
                pl.BlockSpec((CTX_LEN, KV_WIDTH), fixed),
                pl.BlockSpec((CTX_LEN, KV_WIDTH), fixed),
            ],
            out_specs=pl.BlockSpec((ATT_NB * ATT_T, ATTN_WIDTH), cur),
            scratch_shapes=[pltpu.VMEM((ATT_NB, Q_PER_KV * ATT_T, 3 * ATT_T), F32)],
        ),
        out_shape=jax.ShapeDtypeStruct((n, ATTN_WIDTH), BF16),
        compiler_params=_cparams(("arbitrary",)),
        name="attn",
    )(sink, q, k, k, k, v, v, v, kx, vx)


OUT_TM = 512
OUT_SUB = 128


def _out_kernel(ac_ref, aa_ref, x_ref, mod_ref, w_ref, b_ref, g1_ref, b1_ref, wrh_ref, wrl_ref,
                xmid_ref, h2_ref, aff_ref):
    gate1 = mod_ref[2:3, :]
    n_sub = OUT_TM // OUT_SUB

    def mix_of(sb):
        rs = slice(sb * OUT_SUB, (sb + 1) * OUT_SUB)
        return (jnp.dot(ac_ref[rs, :], w_ref[:CONV_WIDTH, :], preferred_element_type=F32)
                + jnp.dot(aa_ref[rs, :], w_ref[CONV_WIDTH:, :], preferred_element_type=F32) + b_ref[...])

    mix_next = mix_of(0)
    for sb in range(n_sub):
        rs = slice(sb * OUT_SUB, (sb + 1) * OUT_SUB)
        mix = mix_next
        if sb + 1 < n_sub:
            mix_next = mix_of(sb + 1)
        xmid = _ln(DEEPNORM_ALPHA * x_ref[rs, :] + gate1 * mix) * g1_ref[...] + b1_ref[...]
        xmid_ref[rs, :] = xmid
        h2 = _ln(xmid) * (1.0 + mod_ref[4:5, :]) + mod_ref[3:4, :]
        for kk in range(D_MODEL // LANES):
            hh, jj = divmod(kk, SUBLANES)
            h2_ref[hh, pl.ds(sb * OUT_SUB * SUBLANES + jj, OUT_SUB, stride=SUBLANES), :] = (
                h2[:, kk * LANES:(kk + 1) * LANES])
        h_hi = h2.astype(BF16)
        h_lo = (h2 - h_hi.astype(F32)).astype(BF16)
        logits = (jnp.dot(h_hi, wrh_ref[...], preferred_element_type=F32)
                  + jnp.dot(h_lo, wrh_ref[...], preferred_element_type=F32)
                  + jnp.dot(h_hi, wrl_ref[...], preferred_element_type=F32))
        logits = logits.T[:N_EXPERTS, :]
        mx = jnp.max(logits, axis=0, keepdims=True)
        ex = jnp.exp(logits - mx)
        aff_ref[:, rs] = ex / jnp.sum(ex, axis=0, keepdims=True)


def _out_proj(a_conv, a_attn, x2, mod6, w_out_bf, b_out, g1, b1, wr_hi, wr_lo):
    n = x2.shape[0]
    row = lambda i: (i, 0)
    fixed = lambda i: (0, 0)
    return pl.pallas_call(
        _out_kernel,
        grid=(n // OUT_TM,),
        in_specs=[
            pl.BlockSpec((OUT_TM, CONV_WIDTH), row),
            pl.BlockSpec((OUT_TM, ATTN_WIDTH), row),
            pl.BlockSpec((OUT_TM, D_MODEL), row),
            pl.BlockSpec((N_MOD, D_MODEL), fixed),
            pl.BlockSpec((D_MODEL, D_MODEL), fixed),
            pl.BlockSpec((1, D_MODEL), fixed),
            pl.BlockSpec((1, D_MODEL), fixed),
            pl.BlockSpec((1, D_MODEL), fixed),
            pl.BlockSpec((D_MODEL, LANES), fixed),
            pl.BlockSpec((D_MODEL, LANES), fixed),
        ],
        out_specs=[
            pl.BlockSpec((OUT_TM, D_MODEL), row),
            pl.BlockSpec((H2_HALVES, OUT_TM * SUBLANES, LANES), lambda i: (0, i, 0)),
            pl.BlockSpec((N_EXPERTS, OUT_TM), lambda i: (0, i)),
        ],
        out_shape=[
            jax.ShapeDtypeStruct((n, D_MODEL), F32),
            jax.ShapeDtypeStruct((H2_HALVES, n * SUBLANES, LANES), F32),
            jax.ShapeDtypeStruct((N_EXPERTS, n), F32),
        ],
        compiler_params=_cparams(("arbitrary",)),
        name="out_proj",
    )(a_conv, a_attn, x2, mod6, w_out_bf, b_out, g1, b1, wr_hi, wr_lo)


def _split3(a):
    a1 = a.astype(BF16)
    r = a - a1.astype(F32)
    a2 = r.astype(BF16)
    a3 = (r - a2.astype(F32)).astype(BF16)
    return a1, a2, a3


def _select_kernel(aff_ref, idx_ref, g_ref, base_ref, pos_ref, sel_ref):
    e = pl.program_id(0)

    @pl.when(e == 0)
    def _():
        aff_all = aff_ref[...]

        def count(mask):
            c = jnp.sum(mask.astype(F32), axis=2, keepdims=True)
            return jnp.sum(c, axis=1, keepdims=True)

        def bit_step(k, thr_bits):
            cand = thr_bits | jnp.left_shift(jnp.int32(1), 30 - k)
            ge = aff_all >= pltpu.bitcast(cand, F32)
            return jnp.where(count(ge) >= CAP, cand, thr_bits)

        thr_bits = lax.fori_loop(0, 31, bit_step, jnp.zeros((N_EXPERTS, 1, 1), jnp.int32))
        thr = pltpu.bitcast(thr_bits, F32)
        gt = aff_all > thr
        eq = aff_all == thr
        need = CAP - count(gt)
        eq2 = eq.astype(F32).reshape(N_EXPERTS * N_CHUNKS, CHUNK)
        tri = (lax.broadcasted_iota(jnp.int32, (CHUNK, CHUNK), 0)
               <= lax.broadcasted_iota(jnp.int32, (CHUNK, CHUNK), 1)).astype(BF16)
        incl = jnp.dot(eq2.astype(BF16), tri, preferred_element_type=F32)
        tot = jnp.broadcast_to(incl[:, CHUNK - 1:CHUNK], (N_EXPERTS * N_CHUNKS, LANES))
        rr = lax.broadcasted_iota(jnp.int32, (N_EXPERTS * N_CHUNKS, N_EXPERTS * N_CHUNKS), 0)
        cc = lax.broadcasted_iota(jnp.int32, (N_EXPERTS * N_CHUNKS, N_EXPERTS * N_CHUNKS), 1)
        cshift = N_CHUNKS.bit_length() - 1
        low = ((jnp.right_shift(rr, cshift) == jnp.right_shift(cc, cshift)) & (cc < rr)).astype(BF16)
        before = jnp.dot(low, tot.astype(BF16), preferred_element_type=F32)
        rank = (before + incl - eq2).reshape(N_EXPERTS, N_CHUNKS, CHUNK)
        sel_ref[...] = (gt | (eq & (rank < need))).astype(F32)

    sel = sel_ref[e]
    aff = aff_ref[e]
    tri = (lax.broadcasted_iota(jnp.int32, (CHUNK, CHUNK), 0)
           <= lax.broadcasted_iota(jnp.int32, (CHUNK, CHUNK), 1)).astype(BF16)
    incl = jnp.dot(sel.astype(BF16), tri, preferred_element_type=F32)
    tot = jnp.broadcast_to(incl[:, CHUNK - 1:CHUNK], (N_CHUNKS, LANES))
    pad = lambda a: jnp.concatenate([a, jnp.zeros((LANES - N_CHUNKS, LANES), a.dtype)], axis=0)
    low = (lax.broadcasted_iota(jnp.int32, (N_CHUNKS, LANES), 1)
           < lax.broadcasted_iota(jnp.int32, (N_CHUNKS, LANES), 0)).astype(BF16)
    base = jnp.dot(low, pad(tot.astype(BF16)), preferred_element_type=F32)
    base_ref[0] = base[:, 0:1].astype(jnp.int32)
    pos_ref[0] = jnp.where(sel > 0.0, base + incl - 1.0, -1.0).astype(jnp.int32)

    pick = (lax.broadcasted_iota(jnp.int32, (SUBLANES, LANES), 1) == 0).astype(BF16)
    nt = (((1,), (1,)), ((), ()))
    base_hi = jnp.floor(base * (1.0 / CHUNK))
    base_lo = base - base_hi * CHUNK
    row = lambda a: lax.dot_general(pick, pad(a.astype(BF16)), nt, preferred_element_type=F32)[0:1, :]
    base_row = row(base_hi) * CHUNK + row(base_lo)
    tot_row = row(tot)
    s_col = lax.broadcasted_iota(jnp.int32, (CAP, LANES), 0).astype(F32)
    owner = ((base_row <= s_col) & (s_col < base_row + tot_row)).astype(BF16)

    own = lambda a: jnp.dot(owner, pad(a.astype(BF16)), preferred_element_type=F32)
    incl_s = own(incl)
    base_s = own(base_hi) * CHUNK + own(base_lo)
    cvals = lax.broadcasted_iota(jnp.int32, (N_CHUNKS, LANES), 0).astype(F32)
    chunk_s = own(cvals)
    s_loc = lax.broadcasted_iota(jnp.int32, (CAP, LANES), 0).astype(F32) - base_s
    j_s = jnp.sum((incl_s <= s_loc).astype(F32), axis=1, keepdims=True)
    idx_ref[0] = (chunk_s[:, 0:1] * CHUNK + j_s).astype(jnp.int32)
    a1, a2, a3 = _split3(aff)
    aff_s = (own(a1) + own(a2)) + own(a3)
    lane = lax.broadcasted_iota(jnp.int32, (CAP, LANES), 1).astype(F32)
    g_ref[0] = jnp.sum(jnp.where(lane == j_s, aff_s, 0.0), axis=1, keepdims=True)


def _select(aff3):
    full = lambda e: (0, 0, 0)
    per = lambda e: (e, 0, 0)
    return pl.pallas_call(
        _select_kernel,
        grid=(N_EXPERTS,),
        in_specs=[pl.BlockSpec((N_EXPERTS, N_CHUNKS, CHUNK), full)],
        out_specs=[
            pl.BlockSpec((1, CAP, 1), per),
            pl.BlockSpec((1, CAP, 1), per),
            pl.BlockSpec((1, N_CHUNKS, 1), per),
            pl.BlockSpec((1, N_CHUNKS, CHUNK), per),
        ],
        out_shape=[
            jax.ShapeDtypeStruct((N_EXPERTS, CAP, 1), jnp.int32),
            jax.ShapeDtypeStruct((N_EXPERTS, CAP, 1), F32),
            jax.ShapeDtypeStruct((N_EXPERTS, N_CHUNKS, 1), jnp.int32),
            jax.ShapeDtypeStruct((N_EXPERTS, N_CHUNKS, CHUNK), jnp.int32),
        ],
        scratch_shapes=[pltpu.VMEM((N_EXPERTS, N_CHUNKS, CHUNK), F32)],
        compiler_params=_cparams(("arbitrary",)),
        name="select",
    )(aff3)


FFN_TF = 512
FFN_NF = D_FF // FFN_TF
FFN_TN = 256
FFN_ROWS = 96
FFN_XROWS = FFN_NF * FFN_ROWS
assert FFN_XROWS >= CAP and FFN_ROWS % 16 == 0
FFN_GATHER_UNROLL = 8
FFN_VMEM_LIMIT = 60 * 1024 * 1024


def _ffn_kernel(idx_ref, h2_hbm, g_ref, wg_ref, wu_ref, wd_ref, y_ref, xs_ref, xb_ref, act_ref, acc_ref,
                sem):
    e = pl.program_id(0)
    f = pl.program_id(1)
    nf = pl.num_programs(1)
    ring_rows = FFN_ROWS * SUBLANES

    def start_row(ee, chunk, j):
        s = jnp.minimum(chunk * FFN_ROWS + j, CAP - 1)
        t8 = pl.multiple_of(idx_ref[ee, s] * SUBLANES, SUBLANES)
        d8 = pl.multiple_of((chunk & 1) * ring_rows + j * SUBLANES, SUBLANES)
        pltpu.make_async_copy(h2_hbm.at[:, pl.ds(t8, SUBLANES), :],
                              xs_ref.at[:, pl.ds(d8, SUBLANES), :], sem.at[chunk & 1]).start()

    def wait_chunk(chunk):
        r0 = pl.multiple_of((chunk & 1) * ring_rows, ring_rows)
        pltpu.make_async_copy(h2_hbm.at[:, pl.ds(0, ring_rows), :],
                              xs_ref.at[:, pl.ds(r0, ring_rows), :], sem.at[chunk & 1]).wait()

    def convert_chunk(chunk, dst):
        r0 = pl.multiple_of((chunk & 1) * ring_rows, ring_rows)
        row0 = pl.multiple_of(chunk * FFN_ROWS, FFN_ROWS)
        for kk in range(D_MODEL // LANES):
            hh, jj = divmod(kk, SUBLANES)
            xb_ref[dst, pl.ds(row0, FFN_ROWS), kk * LANES:(kk + 1) * LANES] = (
                xs_ref[hh, pl.ds(r0 + jj, FFN_ROWS, stride=SUBLANES), :].astype(BF16))

    cur = e & 1

    @pl.when((e == 0) & (f == 0))
    def _():
        def issue(chunk):
            def body(j, carry):
                start_row(0, chunk, j)
                return carry
            lax.fori_loop(0, FFN_ROWS, body, 0, unroll=FFN_GATHER_UNROLL)

        def body(chunk, carry):
            issue(chunk + 1)
            wait_chunk(chunk)
            convert_chunk(chunk, 0)
            return carry

        issue(0)
        lax.fori_loop(0, FFN_NF - 1, body, 0)

    nxt = jnp.where(e + 1 < N_EXPERTS, e + 1, 0)
    done = jnp.where(f == 0, FFN_NF - 1, f - 1)
    wait_chunk(done)
    convert_chunk(done, jnp.where(f == 0, cur, 1 - cur))
    for j in range(FFN_ROWS):
        start_row(nxt, f, j)

    for c in range(FFN_TF // FFN_TN):
        cs = slice(c * FFN_TN, (c + 1) * FFN_TN)
        a = jnp.dot(xb_ref[cur, :CAP, :], wg_ref[0, :, cs].astype(BF16), preferred_element_type=F32)
        u = jnp.dot(xb_ref[cur, :CAP, :], wu_ref[0, :, cs].astype(BF16), preferred_element_type=F32)
        act_ref[:, cs] = (a * jax.nn.sigmoid(a) * u).astype(BF16)
    first = f == 0
    for n in range(D_MODEL // FFN_TN):
        ns = slice(n * FFN_TN, (n + 1) * FFN_TN)
        part = jnp.dot(act_ref[...], wd_ref[0, :, ns].astype(BF16), preferred_element_type=F32)
        acc_ref[:, ns] = jnp.where(first, 0.0, acc_ref[:, ns]) + part

    @pl.when(f == nf - 1)
    def _():
        y_ref[0] = (acc_ref[...] * g_ref[0]).astype(BF16)

    @pl.when((e == N_EXPERTS - 1) & (f == nf - 1))
    def _():
        wait_chunk(FFN_NF - 1)


def _ffn(idx, h2, g, w_gate, w_up, w_down):
    return pl.pallas_call(
        _ffn_kernel,
        grid_spec=pltpu.PrefetchScalarGridSpec(
            num_scalar_prefetch=1,
            grid=(N_EXPERTS, D_FF // FFN_TF),
            in_specs=[
                pl.BlockSpec(memory_space=pl.ANY),
                pl.BlockSpec((1, CAP, 1), lambda e, f, idx: (e, 0, 0)),
                pl.BlockSpec((1, D_MODEL, FFN_TF), lambda e, f, idx: (e, 0, f)),
                pl.BlockSpec((1, D_MODEL, FFN_TF), lambda e, f, idx: (e, 0, f)),
                pl.BlockSpec((1, FFN_TF, D_MODEL), lambda e, f, idx: (e, f, 0)),
            ],
            out_specs=pl.BlockSpec((1, CAP, D_MODEL), lambda e, f, idx: (e, 0, 0)),
            scratch_shapes=[
                pltpu.VMEM((H2_HALVES, 2 * FFN_ROWS * SUBLANES, LANES), F32),
                pltpu.VMEM((2, FFN_XROWS, D_MODEL), BF16),
                pltpu.VMEM((CAP, FFN_TF), BF16),
                pltpu.VMEM((CAP, D_MODEL), F32),
                pltpu.SemaphoreType.DMA((2,)),
            ],
        ),
        out_shape=jax.ShapeDtypeStruct((N_EXPERTS, CAP, D_MODEL), BF16),
        compiler_params=_cparams(("arbitrary", "arbitrary"), FFN_VMEM_LIMIT),
        name="ffn",
    )(idx, h2, g, w_gate, w_up, w_down)


CMB_T = CHUNK
CMB_K = 256
CMB_GRAN = 16
CMB_FIRST = 48
CMB_MAIN = N_EXPERTS * CMB_FIRST
CMB_SIZES = (64, 32, 16)
CMB_OVF = N_EXPERTS * sum(CMB_SIZES)
assert CMB_MAIN % CMB_K == 0 and CMB_OVF % CMB_K == 0
assert CMB_T + CMB_GRAN <= CMB_FIRST + sum(CMB_SIZES)


def _combine_kernel(base_ref, pos_ref, xmid_ref, mod_ref, g2_ref, b2_ref, y_hbm, o_ref,
                    stage_ref, ovf_ref, spread_ref, rcol_ref, acc_ref, sem, ovf_sem):
    i = pl.program_id(0)
    slot = i & 1
    y_rows = N_EXPERTS * CAP

    def windows(tile):
        starts, extra = [], []
        for e in range(N_EXPERTS):
            s0 = e * CAP + base_ref[e * (N_CHUNKS + 1) + tile]
            s1 = e * CAP + base_ref[e * (N_CHUNKS + 1) + tile + 1]
            start = jnp.minimum(s0 & -CMB_GRAN, y_rows - CMB_FIRST)
            starts.append(start)
            extra.append(jnp.maximum(s1 - (start + CMB_FIRST) + CMB_GRAN - 1, 0) & -CMB_GRAN)
        return starts, extra

    def stage_main(tile, slot_):
        for e, start in enumerate(windows(tile)[0]):
            pltpu.make_async_copy(
                y_hbm.at[pl.ds(pl.multiple_of(start, CMB_GRAN), CMB_FIRST)],
                stage_ref.at[slot_, pl.ds(e * CMB_FIRST, CMB_FIRST)], sem.at[slot_]).start()

    @pl.when(i == 0)
    def _():
        stage_ref[...] = jnp.zeros_like(stage_ref)
        ovf_ref[...] = jnp.zeros_like(ovf_ref)
        ee = lax.broadcasted_iota(jnp.int32, (LANES, CMB_MAIN), 0)
        cc = lax.broadcasted_iota(jnp.int32, (LANES, CMB_MAIN), 1)
        lo = ee * CMB_FIRST
        spread_ref[...] = ((cc >= lo) & (cc < lo + CMB_FIRST)).astype(F32).astype(BF16)
        c1 = lax.broadcasted_iota(jnp.int32, (SUBLANES, CMB_MAIN), 1)
        owner = jnp.zeros((SUBLANES, CMB_MAIN), jnp.int32)
        for e in range(1, N_EXPERTS):
            owner = owner + (c1 >= e * CMB_FIRST).astype(jnp.int32)
        rcol_ref[...] = (c1 - owner * CMB_FIRST).astype(F32)
        stage_main(0, 0)

    @pl.when(i + 1 < pl.num_programs(0))
    def _():
        stage_main(i + 1, 1 - slot)

    starts, extra = windows(i)
    pltpu.make_async_copy(y_hbm.at[pl.ds(0, CMB_MAIN)], stage_ref.at[slot], sem.at[slot]).wait()

    lane = lax.broadcasted_iota(jnp.int32, (1, LANES), 1)
    shift = jnp.zeros((1, LANES), jnp.int32)
    for e in range(N_EXPERTS):
        shift = jnp.where(lane == e, e * CAP - starts[e], shift)
    pos = pos_ref[...]
    rel = jnp.where(pos >= 0, pos + shift, -1)
    spread = jnp.dot(rel.astype(F32).astype(BF16), spread_ref[...], preferred_element_type=F32)
    for k in range(CMB_MAIN // CMB_K):
        ks = slice(k * CMB_K, (k + 1) * CMB_K)
        w = (spread[:, ks] == rcol_ref[0:1, ks]).astype(F32).astype(BF16)
        part = jnp.dot(w, stage_ref[slot, ks, :], preferred_element_type=F32)
        if k == 0:
            acc_ref[...] = part
        else:
            acc_ref[...] += part

    n_extra = extra[0]
    for e in range(1, N_EXPERTS):
        n_extra = n_extra + extra[e]

    @pl.when(n_extra > 0)
    def _():
        copies, rowidx = [], []
        ooff = jnp.int32(0)
        for e in range(N_EXPERTS):
            done = jnp.int32(0)
            for size in CMB_SIZES:
                pred = (extra[e] & size) != 0
                src = pl.multiple_of(starts[e] + CMB_FIRST + done, CMB_GRAN)
                dst = pl.multiple_of(ooff + done, CMB_GRAN)
                copies.append((pred, pltpu.make_async_copy(
                    y_hbm.at[pl.ds(src, size)], ovf_ref.at[pl.ds(dst, size)], ovf_sem)))
                done = done + jnp.where(pred, size, 0)
            r = rel[:, e:e + 1]
            rowidx.append(jnp.where(r >= CMB_FIRST, r - CMB_FIRST + ooff, -1))
            ooff = ooff + extra[e]
        for pred, cp in copies:
            @pl.when(pred)
            def _():
                cp.start()
        for pred, cp in copies:
            @pl.when(pred)
            def _():
                cp.wait()

        def body(k, carry):
            r0 = pl.multiple_of(k * CMB_K, CMB_K)
            col = lax.broadcasted_iota(jnp.int32, (CMB_T, CMB_K), 1) + r0
            w = jnp.zeros((CMB_T, CMB_K), F32)
            for e in range(N_EXPERTS):
                w = w + (rowidx[e] == col).astype(F32)
            acc_ref[...] += jnp.dot(w.astype(BF16), ovf_ref[pl.ds(r0, CMB_K), :],
                                    preferred_element_type=F32)
            return carry

        lax.fori_loop(0, (n_extra + CMB_K - 1) // CMB_K, body, 0)

    gate2 = mod_ref[5:6, :]
    o_ref[...] = _ln(DEEPNORM_ALPHA * xmid_ref[...] + gate2 * acc_ref[...]) * g2_ref[...] + b2_ref[...]


def _combine(base_flat, pos_te, xmid, mod6, g2, b2, y_flat):
    n = xmid.shape[0]
    row = lambda i, b: (i, 0)
    fixed = lambda i, b: (0, 0)
    return pl.pallas_call(
        _combine_kernel,
        grid_spec=pltpu.PrefetchScalarGridSpec(
            num_scalar_prefetch=1,
            grid=(n // CMB_T,),
            in_specs=[
                pl.BlockSpec((CMB_T, LANES), row),
                pl.BlockSpec((CMB_T, D_MODEL), row),
                pl.BlockSpec((N_MOD, D_MODEL), fixed),
                pl.BlockSpec((1, D_MODEL), fixed),
                pl.BlockSpec((1, D_MODEL), fixed),
                pl.BlockSpec(memory_space=pl.ANY),
            ],
            out_specs=pl.BlockSpec((CMB_T, D_MODEL), row),
            scratch_shapes=[
                pltpu.VMEM((2, CMB_MAIN, D_MODEL), BF16),
                pltpu.VMEM((CMB_OVF, D_MODEL), BF16),
                pltpu.VMEM((LANES, CMB_MAIN), BF16),
                pltpu.VMEM((SUBLANES, CMB_MAIN), F32),
                pltpu.VMEM((CMB_T, D_MODEL), F32),
                pltpu.SemaphoreType.DMA((2,)),
                pltpu.SemaphoreType.DMA(()),
            ],
        ),
        out_shape=jax.ShapeDtypeStruct((n, D_MODEL), F32),
        compiler_params=_cparams(("arbitrary",)),
        name="combine",
    )(base_flat, pos_te, xmid, mod6, g2, b2, y_flat)


def _rope_tables(n):
    rows = n // GRID_W
    inv = ROPE_THETA ** (-np.arange(0, ROPE_AXIS_DIM, 2, dtype=np.float64) / ROPE_AXIS_DIM)
    ang_r = np.arange(rows, dtype=np.float64)[:, None] * inv[None, :]
    ang_c = np.arange(GRID_W, dtype=np.float64)[:, None] * inv[None, :]
    zr, zc = np.zeros_like(ang_r), np.zeros_like(ang_c)
    cr, sr, cc, sc = np.cos(ang_r), np.sin(ang_r), np.cos(ang_c), np.sin(ang_c)
    trow = np.stack([np.concatenate(p, axis=1) for p in
                     ([cr, cr, zr, zr], [-sr, zr, zr, zr], [zr, sr, zr, zr])])
    tcol = np.stack([np.concatenate(p, axis=1) for p in
                     ([zc, zc, cc, cc], [zc, zc, -sc, zc], [zc, zc, zc, sc])])
    return jnp.asarray(trow, F32), jnp.asarray(tcol, F32)


def kernel(x, c, ctx, c_ctx, w_mod, b_mod, w_in, b_in, w_dw, b_dw, conv_ln_g, conv_ln_b, sink,
           w_out, b_out, ln1_g, ln1_b, w_router, w_gate, w_up, w_down, ln2_g, ln2_b):
    assert x.shape == (1, SEQ, D_MODEL) and ctx.shape == (1, CTX_LEN, D_MODEL)
    assert w_mod.shape[0] == DEPTH
    x2 = x[0]
    ctx2 = ctx[0]
    r1 = lambda a: a.reshape(1, -1)

    ct = jnp.stack([c[0], c_ctx], axis=1)
    mod = _mod(ct, w_mod[0], r1(b_mod[0]))
    mod6 = mod[0].reshape(N_MOD, D_MODEL)
    modc6 = mod[1].reshape(N_MOD, D_MODEL)

    w_in_bf = w_in[0].astype(BF16)
    w_out_bf = w_out[0].astype(BF16)
    trow, tcol = _rope_tables(SEQ)
    u, q, k, v = _in_proj(x2, mod6, w_in_bf, r1(b_in[0]), trow, tcol)
    kx, vx = _ctx_kv(ctx2, modc6, w_in_bf, r1(b_in[0]))
    a_conv = _conv(u, w_dw[0], r1(b_dw[0]), r1(conv_ln_g[0]), r1(conv_ln_b[0]))
    a_attn = _attn(sink[0], q, k, v, kx, vx)
    wr = jnp.pad(w_router[0], ((0, 0), (0, LANES - N_EXPERTS)))
    wr_hi = wr.astype(BF16)
    wr_lo = (wr - wr_hi.astype(F32)).astype(BF16)
    xmid, h2, aff_t = _out_proj(a_conv, a_attn, x2, mod6, w_out_bf, r1(b_out[0]),
                                r1(ln1_g[0]), r1(ln1_b[0]), wr_hi, wr_lo)

    idx, g, base, pos = _select(aff_t.reshape(N_EXPERTS, N_CHUNKS, CHUNK))
    y = _ffn(idx.reshape(N_EXPERTS, CAP), h2, g, w_gate[0], w_up[0], w_down[0])

    base_flat = jnp.concatenate(
        [base.reshape(N_EXPERTS, N_CHUNKS), jnp.full((N_EXPERTS, 1), CAP, jnp.int32)], axis=1).reshape(-1)
    pos_te = jnp.pad(pos.reshape(N_EXPERTS, SEQ).T, ((0, 0), (0, LANES - N_EXPERTS)),
                     constant_values=-1)
    out = _combine(base_flat, pos_te, xmid, mod6, r1(ln2_g[0]), r1(ln2_b[0]),
                   y.reshape(N_EXPERTS * CAP, D_MODEL))
    return out[None]
```

```python
import numpy as np

import jax
import jax.numpy as jnp
from jax import lax
from jax.experimental import pallas as pl
from jax.experimental.pallas import tpu as pltpu

D_MODEL = 2048
SEQ = 8192
GRID_W = 64
CTX_LEN = 256
HEAD_DIM = 128
N_Q_HEADS = 8
N_KV_HEADS = 2
Q_PER_KV = N_Q_HEADS // N_KV_HEADS
ATTN_WIDTH = N_Q_HEADS * HEAD_DIM
KV_WIDTH = N_KV_HEADS * HEAD_DIM
CONV_WIDTH = D_MODEL - ATTN_WIDTH
CONV_KSIZE = 31
WINDOW = 128
ROPE_THETA = 10000.0
ROPE_AXIS_DIM = HEAD_DIM // 2
N_EXPERTS = 16
EC_CAPACITY = 2
CAP = EC_CAPACITY * SEQ // N_EXPERTS
D_FF = 5632
N_MOD = 6
LN_EPS = 1e-5
NEG_INF = -1e30
DEPTH = 1
DEEPNORM_ALPHA = (2.0 * DEPTH) ** 0.25
Q_START = 2 * CONV_WIDTH
K_START = Q_START + ATTN_WIDTH
V_START = K_START + KV_WIDTH
IN_COLS = V_START + KV_WIDTH

LANES = 128
SUBLANES = 8
H2_HALVES = D_MODEL // (SUBLANES * LANES)
CHUNK = LANES
N_CHUNKS = SEQ // CHUNK
VMEM_LIMIT = 56 * 1024 * 1024

F32 = jnp.float32
BF16 = jnp.bfloat16


def _ln(xv):
    mu = jnp.mean(xv, axis=-1, keepdims=True)
    xc = xv - mu
    var = jnp.mean(xc * xc, axis=-1, keepdims=True)
    return xc * lax.rsqrt(var + LN_EPS)


def _cparams(sem, vmem=VMEM_LIMIT):
    return pltpu.CompilerParams(dimension_semantics=sem, vmem_limit_bytes=vmem)


MOD_TN = 1024
MOD_UNROLL = 4


def _mod_kernel(ct_ref, w_ref, b_ref, o_ref, s0_ref, s1_ref):
    @pl.when(pl.program_id(0) == 0)
    def _():
        ct = ct_ref[...]
        s = ct * jax.nn.sigmoid(ct)
        s0_ref[...] = jnp.broadcast_to(s[:, 0:1], (D_MODEL, LANES))
        s1_ref[...] = jnp.broadcast_to(s[:, 1:2], (D_MODEL, LANES))

    def body(kb, acc):
        k0 = pl.multiple_of(kb * SUBLANES, SUBLANES)
        s0 = s0_ref[pl.ds(k0, SUBLANES), :]
        s1 = s1_ref[pl.ds(k0, SUBLANES), :]
        a0, a1 = [], []
        for j in range(MOD_TN // LANES):
            w = w_ref[pl.ds(k0, SUBLANES), j * LANES:(j + 1) * LANES]
            a0.append(acc[0][j] + w * s0)
            a1.append(acc[1][j] + w * s1)
        return tuple(a0), tuple(a1)

    zeros = tuple(jnp.zeros((SUBLANES, LANES), F32) for _ in range(MOD_TN // LANES))
    acc0, acc1 = lax.fori_loop(0, D_MODEL // SUBLANES, body, (zeros, zeros), unroll=MOD_UNROLL)
    for j in range(MOD_TN // LANES):
        b = b_ref[:, j * LANES:(j + 1) * LANES]
        o_ref[0:1, j * LANES:(j + 1) * LANES] = jnp.sum(acc0[j], axis=0, keepdims=True) + b
        o_ref[1:2, j * LANES:(j + 1) * LANES] = jnp.sum(acc1[j], axis=0, keepdims=True) + b


def _mod(ct, w_mod, b_mod):
    n_out = N_MOD * D_MODEL
    return pl.pallas_call(
        _mod_kernel,
        grid=(n_out // MOD_TN,),
        in_specs=[
            pl.BlockSpec((D_MODEL, 2), lambda j: (0, 0)),
            pl.BlockSpec((D_MODEL, MOD_TN), lambda j: (0, j)),
            pl.BlockSpec((1, MOD_TN), lambda j: (0, j)),
        ],
        out_specs=pl.BlockSpec((2, MOD_TN), lambda j: (0, j)),
        out_shape=jax.ShapeDtypeStruct((2, n_out), F32),
        scratch_shapes=[pltpu.VMEM((D_MODEL, LANES), F32), pltpu.VMEM((D_MODEL, LANES), F32)],
        compiler_params=_cparams(("arbitrary",)),
        name="mod",
    )(ct, w_mod, b_mod)


IN_TM = 512
IN_SUB = 256
IN_TN = 512


def _rope(p, cos, sina, sinb):
    return (p * cos + pltpu.roll(p, HEAD_DIM - ROPE_AXIS_DIM // 2, axis=1) * sina
            + pltpu.roll(p, ROPE_AXIS_DIM // 2, axis=1) * sinb)


def _in_kernel(x_ref, mod_ref, w_ref, b_ref, trow_ref, tcol_ref, u_ref, q_ref, k_ref, v_ref):
    shift = mod_ref[0:1, :]
    scale = mod_ref[1:2, :]
    grid_rows = IN_SUB // GRID_W

    for sb in range(IN_TM // IN_SUB):
        rs = slice(sb * IN_SUB, (sb + 1) * IN_SUB)
        h = (_ln(x_ref[rs, :]) * (1.0 + scale) + shift).astype(BF16)

        def table(kind):
            by_row = jnp.concatenate(
                [jnp.broadcast_to(trow_ref[kind, sb * grid_rows + r:sb * grid_rows + r + 1, :],
                                  (GRID_W, HEAD_DIM)) for r in range(grid_rows)], axis=0)
            by_col = jnp.concatenate([tcol_ref[kind]] * grid_rows, axis=0)
            return by_row + by_col

        cos, sina, sinb = table(0), table(1), table(2)

        def proj(c0, width):
            return (jnp.dot(h, w_ref[:, c0:c0 + width], preferred_element_type=F32)
                    + b_ref[:, c0:c0 + width])

        for j in range(CONV_WIDTH // IN_TN):
            pv = proj(j * IN_TN, IN_TN)
            pg = proj(CONV_WIDTH + j * IN_TN, IN_TN)
            u_ref[rs, j * IN_TN:(j + 1) * IN_TN] = pv * jax.nn.sigmoid(pg)
        for j in range(ATTN_WIDTH // IN_TN):
            pq = proj(Q_START + j * IN_TN, IN_TN)
            for hh in range(IN_TN // HEAD_DIM):
                c0 = j * IN_TN + hh * HEAD_DIM
                q_ref[rs, c0:c0 + HEAD_DIM] = _rope(
                    pq[:, hh * HEAD_DIM:(hh + 1) * HEAD_DIM], cos, sina, sinb).astype(BF16)
        pk = proj(K_START, KV_WIDTH)
        for hh in range(N_KV_HEADS):
            k_ref[rs, hh * HEAD_DIM:(hh + 1) * HEAD_DIM] = _rope(
                pk[:, hh * HEAD_DIM:(hh + 1) * HEAD_DIM], cos, sina, sinb).astype(BF16)
        v_ref[rs, :] = proj(V_START, KV_WIDTH).astype(BF16)


def _in_proj(x2, mod6, w_in_bf, b_in, trow, tcol):
    n = x2.shape[0]
    row = lambda i: (i, 0)
    fixed = lambda i: (0, 0)
    return pl.pallas_call(
        _in_kernel,
        grid=(n // IN_TM,),
        in_specs=[
            pl.BlockSpec((IN_TM, D_MODEL), row),
            pl.BlockSpec((N_MOD, D_MODEL), fixed),
            pl.BlockSpec((D_MODEL, IN_COLS), fixed),
            pl.BlockSpec((1, IN_COLS), fixed),
            pl.BlockSpec((3, IN_TM // GRID_W, HEAD_DIM), lambda i: (0, i, 0)),
            pl.BlockSpec((3, GRID_W, HEAD_DIM), lambda i: (0, 0, 0)),
        ],
        out_specs=[
            pl.BlockSpec((IN_TM, CONV_WIDTH), row),
            pl.BlockSpec((IN_TM, ATTN_WIDTH), row),
            pl.BlockSpec((IN_TM, KV_WIDTH), row),
            pl.BlockSpec((IN_TM, KV_WIDTH), row),
        ],
        out_shape=[
            jax.ShapeDtypeStruct((n, CONV_WIDTH), F32),
            jax.ShapeDtypeStruct((n, ATTN_WIDTH), BF16),
            jax.ShapeDtypeStruct((n, KV_WIDTH), BF16),
            jax.ShapeDtypeStruct((n, KV_WIDTH), BF16),
        ],
        compiler_params=_cparams(("arbitrary",)),
        name="in_proj",
    )(x2, mod6, w_in_bf, b_in, trow, tcol)


def _ctx_kernel(x_ref, mod_ref, w_ref, b_ref, kc_ref, vc_ref):
    shift = mod_ref[0:1, :]
    scale = mod_ref[1:2, :]
    h = (_ln(x_ref[...]) * (1.0 + scale) + shift).astype(BF16)
    p = jnp.dot(h, w_ref[...], preferred_element_type=F32) + b_ref[...]
    kc_ref[...] = p[:, :KV_WIDTH].astype(BF16)
    vc_ref[...] = p[:, KV_WIDTH:].astype(BF16)


def _ctx_kv(ctx2, modc6, w_in_bf, b_in):
    kvw = 2 * KV_WIDTH
    fixed = lambda i: (0, 0)
    return pl.pallas_call(
        _ctx_kernel,
        grid=(1,),
        in_specs=[
            pl.BlockSpec((CTX_LEN, D_MODEL), fixed),
            pl.BlockSpec((N_MOD, D_MODEL), fixed),
            pl.BlockSpec((D_MODEL, kvw), lambda i: (0, K_START // kvw)),
            pl.BlockSpec((1, kvw), lambda i: (0, K_START // kvw)),
        ],
        out_specs=[pl.BlockSpec((CTX_LEN, KV_WIDTH), fixed), pl.BlockSpec((CTX_LEN, KV_WIDTH), fixed)],
        out_shape=[jax.ShapeDtypeStruct((CTX_LEN, KV_WIDTH), BF16),
                   jax.ShapeDtypeStruct((CTX_LEN, KV_WIDTH), BF16)],
        compiler_params=_cparams(("arbitrary",)),
        name="ctx_kv",
    )(ctx2, modc6, w_in_bf, b_in)


CONV_T = 512
CONV_HALO = 16
CONV_ROWS = 64


def _conv_kernel(up_ref, uc_ref, un_ref, w_ref, bdw_ref, g_ref, b_ref, o_ref, buf_ref, acc_ref):
    i = pl.program_id(0)
    last = pl.num_programs(0) - 1
    buf_ref[0:CONV_HALO, :] = jnp.where(i > 0, up_ref[...], 0.0)
    buf_ref[CONV_HALO:CONV_HALO + CONV_T, :] = uc_ref[...]
    buf_ref[CONV_HALO + CONV_T:, :] = jnp.where(i < last, un_ref[...], 0.0)
    off = CONV_HALO - CONV_KSIZE // 2

    span = CONV_ROWS + 2 * CONV_HALO

    def body(r, carry):
        r0 = pl.multiple_of(r * CONV_ROWS, CONV_ROWS)
        for lg in range(CONV_WIDTH // LANES):
            ls = slice(lg * LANES, (lg + 1) * LANES)
            blk = buf_ref[pl.ds(r0, span), ls]
            shifted = [blk] + [pltpu.roll(blk, span - s, axis=0) for s in range(1, SUBLANES)]
            acc = jnp.zeros((CONV_ROWS, LANES), F32)
            for t in range(CONV_KSIZE):
                d = off + t
                a0 = (d // SUBLANES) * SUBLANES
                acc = acc + shifted[d % SUBLANES][a0:a0 + CONV_ROWS, :] * w_ref[t:t + 1, ls]
            acc_ref[pl.ds(r0, CONV_ROWS), ls] = acc
        return carry

    lax.fori_loop(0, CONV_T // CONV_ROWS, body, 0)
    y = _ln(acc_ref[...] + bdw_ref[...]) * g_ref[...] + b_ref[...]
    o_ref[...] = (y * jax.nn.sigmoid(y)).astype(BF16)


def _conv(u, w_dw, b_dw, ln_g, ln_b):
    n = u.shape[0]
    hb = CONV_T // CONV_HALO
    nhb = n // CONV_HALO
    fixed = lambda i: (0, 0)
    return pl.pallas_call(
        _conv_kernel,
        grid=(n // CONV_T,),
        in_specs=[
            pl.BlockSpec((CONV_HALO, CONV_WIDTH), lambda i: (jnp.maximum(i * hb - 1, 0), 0)),
            pl.BlockSpec((CONV_T, CONV_WIDTH), lambda i: (i, 0)),
            pl.BlockSpec((CONV_HALO, CONV_WIDTH), lambda i: (jnp.minimum((i + 1) * hb, nhb - 1), 0)),
            pl.BlockSpec((CONV_KSIZE, CONV_WIDTH), fixed),
            pl.BlockSpec((1, CONV_WIDTH), fixed),
            pl.BlockSpec((1, CONV_WIDTH), fixed),
            pl.BlockSpec((1, CONV_WIDTH), fixed),
        ],
        out_specs=pl.BlockSpec((CONV_T, CONV_WIDTH), lambda i: (i, 0)),
        out_shape=jax.ShapeDtypeStruct((n, CONV_WIDTH), BF16),
        scratch_shapes=[pltpu.VMEM((CONV_T + 2 * CONV_HALO, CONV_WIDTH), F32),
                        pltpu.VMEM((CONV_T, CONV_WIDTH), F32)],
        compiler_params=_cparams(("arbitrary",)),
        name="conv",
    )(u, u, u, w_dw, b_dw, ln_g, ln_b)


ATT_T = 128
ATT_NB = 2
assert ATT_T == WINDOW
LOG2E = 1.4426950408889634


def _attn_kernel(sink_ref, q_ref, kp_ref, kc_ref, kn_ref, vp_ref, vc_ref, vn_ref, kx_ref, vx_ref, o_ref,
                 bias_ref):
    i = pl.program_id(0)
    last = pl.num_programs(0) - 1
    scale = HEAD_DIM ** -0.5
    rows = Q_PER_KV * ATT_T

    @pl.when((i <= 1) | (i == last))
    def _():
        qi = lax.broadcasted_iota(jnp.int32, (rows, 3 * ATT_T), 0) & (ATT_T - 1)
        m = lax.broadcasted_iota(jnp.int32, (rows, 3 * ATT_T), 1)
        band = jnp.abs(m - qi - ATT_T) <= WINDOW
        for b in range(ATT_NB):
            kpos = (i * ATT_NB + b - 1) * ATT_T + m
            bias_ref[b] = jnp.where(band & (kpos >= 0) & (kpos < SEQ), 0.0, NEG_INF)

    hrow = jnp.right_shift(lax.broadcasted_iota(jnp.int32, (rows, 1), 0), ATT_T.bit_length() - 1)
    nt = (((1,), (1,)), ((), ()))
    def key_blocks(b, ls, p_ref, c_ref, n_ref):
        blocks = ([p_ref[:, ls]] + [c_ref[j * ATT_T:(j + 1) * ATT_T, ls] for j in range(ATT_NB)]
                  + [n_ref[:, ls]])
        return jnp.concatenate(blocks[b:b + 3], axis=0)

    def scores(b, g):
        rs = slice(b * ATT_T, (b + 1) * ATT_T)
        ls = slice(g * HEAD_DIM, (g + 1) * HEAD_DIM)
        qs = jnp.concatenate(
            [q_ref[rs, (g * Q_PER_KV + hh) * HEAD_DIM:(g * Q_PER_KV + hh + 1) * HEAD_DIM]
             for hh in range(Q_PER_KV)], axis=0)
        kw = key_blocks(b, ls, kp_ref, kc_ref, kn_ref)
        s_win = lax.dot_general(qs, kw, nt, preferred_element_type=F32) + bias_ref[b]
        s_ctx = lax.dot_general(qs, kx_ref[:, ls], nt, preferred_element_type=F32)
        return s_win, s_ctx

    def weights(g, s_win, s_ctx):
        s_sink = jnp.zeros((rows, 1), F32)
        for hh in range(Q_PER_KV):
            s_sink = jnp.where(hrow == hh, sink_ref[g * Q_PER_KV + hh], s_sink)
        raw_max = jnp.maximum(jnp.max(s_win, axis=-1, keepdims=True), jnp.max(s_ctx, axis=-1, keepdims=True))
        mx2 = jnp.maximum(raw_max * scale, s_sink) * LOG2E
        e_win = jnp.exp2(s_win * (scale * LOG2E) - mx2)
        e_ctx = jnp.exp2(s_ctx * (scale * LOG2E) - mx2)
        den = (jnp.sum(e_win, axis=-1, keepdims=True) + jnp.sum(e_ctx, axis=-1, keepdims=True)
               + jnp.exp2(s_sink * LOG2E - mx2))
        return e_win.astype(BF16), e_ctx.astype(BF16), den

    def values(b, g, e_win, e_ctx, den):
        rs = slice(b * ATT_T, (b + 1) * ATT_T)
        ls = slice(g * HEAD_DIM, (g + 1) * HEAD_DIM)
        vw = key_blocks(b, ls, vp_ref, vc_ref, vn_ref)
        o = (jnp.dot(e_ctx, vx_ref[:, ls], preferred_element_type=F32)
             + jnp.dot(e_win, vw, preferred_element_type=F32)) * (1.0 / den)
        for hh in range(Q_PER_KV):
            c0 = (g * Q_PER_KV + hh) * HEAD_DIM
            o_ref[rs, c0:c0 + HEAD_DIM] = o[hh * ATT_T:(hh + 1) * ATT_T, :].astype(BF16)

    chains = [(b, g) for b in range(ATT_NB) for g in range(N_KV_HEADS)]
    s_next = scores(*chains[0])
    for c, (b, g) in enumerate(chains):
        s_cur = s_next
        if c + 1 < len(chains):
            s_next = scores(*chains[c + 1])
        values(b, g, *weights(g, *s_cur))


def _attn(sink, q, k, v, kx, vx):
    n = q.shape[0]
    nb = n // ATT_T
    prev = lambda i, s: (jnp.maximum(i * ATT_NB - 1, 0), 0)
    cur = lambda i, s: (i, 0)
    nxt = lambda i, s: (jnp.minimum((i + 1) * ATT_NB, nb - 1), 0)
    fixed = lambda i, s: (0, 0)
    edge = lambda im: pl.BlockSpec((ATT_T, KV_WIDTH), im)
    own = pl.BlockSpec((ATT_NB * ATT_T, KV_WIDTH), cur)
    return pl.pallas_call(
        _attn_kernel,
        grid_spec=pltpu.PrefetchScalarGridSpec(
            num_scalar_prefetch=1,
            grid=(nb // ATT_NB,),
            in_specs=[
                pl.BlockSpec((ATT_NB * ATT_T, ATTN_WIDTH), cur),
                edge(prev), own, edge(nxt),
                edge(prev), own, edge(nxt),
                pl.BlockSpec((CTX_LEN, KV_WIDTH), fixed),
                pl.BlockSpec((CTX_LEN, KV_WIDTH), fixed),
            ],
            out_specs=pl.BlockSpec((ATT_NB * ATT_T, ATTN_WIDTH), cur),
            scratch_shapes=[pltpu.VMEM((ATT_NB, Q_PER_KV * ATT_T, 3 * ATT_T), F32)],
        ),
        out_shape=jax.ShapeDtypeStruct((n, ATTN_WIDTH), BF16),
        compiler_params=_cparams(("arbitrary",)),
        name="attn",
    )(sink, q, k, k, k, v, v, v, kx, vx)


OUT_TM = 512
OUT_SUB = 128


def _out_kernel(ac_ref, aa_ref, x_ref, mod_ref, w_ref, b_ref, g1_ref, b1_ref, wrh_ref, wrl_ref,
                xmid_ref, h2_ref, aff_ref):
    gate1 = mod_ref[2:3, :]
    n_sub = OUT_TM // OUT_SUB

    def mix_of(sb):
        rs = slice(sb * OUT_SUB, (sb + 1) * OUT_SUB)
        return (jnp.dot(ac_ref[rs, :], w_ref[:CONV_WIDTH, :], preferred_element_type=F32)
                + jnp.dot(aa_ref[rs, :], w_ref[CONV_WIDTH:, :], preferred_element_type=F32) + b_ref[...])

    mix_next = mix_of(0)
    for sb in range(n_sub):
        rs = slice(sb * OUT_SUB, (sb + 1) * OUT_SUB)
        mix = mix_next
        if sb + 1 < n_sub:
            mix_next = mix_of(sb + 1)
        xmid = _ln(DEEPNORM_ALPHA * x_ref[rs, :] + gate1 * mix) * g1_ref[...] + b1_ref[...]
        xmid_ref[rs, :] = xmid
        h2 = _ln(xmid) * (1.0 + mod_ref[4:5, :]) + mod_ref[3:4, :]
        for kk in range(D_MODEL // LANES):
            hh, jj = divmod(kk, SUBLANES)
            h2_ref[hh, pl.ds(sb * OUT_SUB * SUBLANES + jj, OUT_SUB, stride=SUBLANES), :] = (
                h2[:, kk * LANES:(kk + 1) * LANES])
        h_hi = h2.astype(BF16)
        h_lo = (h2 - h_hi.astype(F32)).astype(BF16)
        logits = (jnp.dot(h_hi, wrh_ref[...], preferred_element_type=F32)
                  + jnp.dot(h_lo, wrh_ref[...], preferred_element_type=F32)
                  + jnp.dot(h_hi, wrl_ref[...], preferred_element_type=F32))
        logits = logits.T[:N_EXPERTS, :]
        mx = jnp.max(logits, axis=0, keepdims=True)
        ex = jnp.exp(logits - mx)
        aff_ref[:, rs] = ex / jnp.sum(ex, axis=0, keepdims=True)


def _out_proj(a_conv, a_attn, x2, mod6, w_out_bf, b_out, g1, b1, wr_hi, wr_lo):
    n = x2.shape[0]
    row = lambda i: (i, 0)
    fixed = lambda i: (0, 0)
    return pl.pallas_call(
        _out_kernel,
        grid=(n // OUT_TM,),
        in_specs=[
            pl.BlockSpec((OUT_TM, CONV_WIDTH), row),
            pl.BlockSpec((OUT_TM, ATTN_WIDTH), row),
            pl.BlockSpec((OUT_TM, D_MODEL), row),
            pl.BlockSpec((N_MOD, D_MODEL), fixed),
            pl.BlockSpec((D_MODEL, D_MODEL), fixed),
            pl.BlockSpec((1, D_MODEL), fixed),
            pl.BlockSpec((1, D_MODEL), fixed),
            pl.BlockSpec((1, D_MODEL), fixed),
            pl.BlockSpec((D_MODEL, LANES), fixed),
            pl.BlockSpec((D_MODEL, LANES), fixed),
        ],
        out_specs=[
            pl.BlockSpec((OUT_TM, D_MODEL), row),
            pl.BlockSpec((H2_HALVES, OUT_TM * SUBLANES, LANES), lambda i: (0, i, 0)),
            pl.BlockSpec((N_EXPERTS, OUT_TM), lambda i: (0, i)),
        ],
        out_shape=[
            jax.ShapeDtypeStruct((n, D_MODEL), F32),
            jax.ShapeDtypeStruct((H2_HALVES, n * SUBLANES, LANES), F32),
            jax.ShapeDtypeStruct((N_EXPERTS, n), F32),
        ],
        compiler_params=_cparams(("arbitrary",)),
        name="out_proj",
    )(a_conv, a_attn, x2, mod6, w_out_bf, b_out, g1, b1, wr_hi, wr_lo)


def _split3(a):
    a1 = a.astype(BF16)
    r = a - a1.astype(F32)
    a2 = r.astype(BF16)
    a3 = (r - a2.astype(F32)).astype(BF16)
    return a1, a2, a3


def _select_kernel(aff_ref, idx_ref, g_ref, base_ref, pos_ref, sel_ref):
    e = pl.program_id(0)

    @pl.when(e == 0)
    def _():
        aff_all = aff_ref[...]

        def count(mask):
            c = jnp.sum(mask.astype(F32), axis=1, keepdims=True)
            return jnp.sum(c, axis=2, keepdims=True)

        def bit_step(k, thr_bits):
            cand = thr_bits | jnp.left_shift(jnp.int32(1), 30 - k)
            ge = aff_all >= pltpu.bitcast(cand, F32)
            return jnp.where(count(ge) >= CAP, cand, thr_bits)

        thr_bits = lax.fori_loop(0, 31, bit_step, jnp.zeros((N_EXPERTS, 1, 1), jnp.int32))
        thr = pltpu.bitcast(thr_bits, F32)
        gt = aff_all > thr
        eq = aff_all == thr
        need = CAP - count(gt)
        eq2 = eq.astype(F32).reshape(N_EXPERTS * N_CHUNKS, CHUNK)
        tri = (lax.broadcasted_iota(jnp.int32, (CHUNK, CHUNK), 0)
               <= lax.broadcasted_iota(jnp.int32, (CHUNK, CHUNK), 1)).astype(BF16)
        incl = jnp.dot(eq2.astype(BF16), tri, preferred_element_type=F32)
        tot = jnp.broadcast_to(incl[:, CHUNK - 1:CHUNK], (N_EXPERTS * N_CHUNKS, LANES))
        rr = lax.broadcasted_iota(jnp.int32, (N_EXPERTS * N_CHUNKS, N_EXPERTS * N_CHUNKS), 0)
        cc = lax.broadcasted_iota(jnp.int32, (N_EXPERTS * N_CHUNKS, N_EXPERTS * N_CHUNKS), 1)
        cshift = N_CHUNKS.bit_length() - 1
        low = ((jnp.right_shift(rr, cshift) == jnp.right_shift(cc, cshift)) & (cc < rr)).astype(BF16)
        before = jnp.dot(low, tot.astype(BF16), preferred_element_type=F32)
        rank = (before + incl - eq2).reshape(N_EXPERTS, N_CHUNKS, CHUNK)
        sel_ref[...] = (gt | (eq & (rank < need))).astype(F32)

    sel = sel_ref[e]
    aff = aff_ref[e]
    tri = (lax.broadcasted_iota(jnp.int32, (CHUNK, CHUNK), 0)
           <= lax.broadcasted_iota(jnp.int32, (CHUNK, CHUNK), 1)).astype(BF16)
    incl = jnp.dot(sel.astype(BF16), tri, preferred_element_type=F32)
    tot = jnp.broadcast_to(incl[:, CHUNK - 1:CHUNK], (N_CHUNKS, LANES))
    pad = lambda a: jnp.concatenate([a, jnp.zeros((LANES - N_CHUNKS, LANES), a.dtype)], axis=0)
    low = (lax.broadcasted_iota(jnp.int32, (N_CHUNKS, LANES), 1)
           < lax.broadcasted_iota(jnp.int32, (N_CHUNKS, LANES), 0)).astype(BF16)
    base = jnp.dot(low, pad(tot.astype(BF16)), preferred_element_type=F32)
    base_ref[0] = base[:, 0:1].astype(jnp.int32)
    pos_ref[0] = jnp.where(sel > 0.0, base + incl - 1.0, -1.0).astype(jnp.int32)

    pick = (lax.broadcasted_iota(jnp.int32, (SUBLANES, LANES), 1) == 0).astype(BF16)
    nt = (((1,), (1,)), ((), ()))
    base_hi = jnp.floor(base * (1.0 / CHUNK))
    base_lo = base - base_hi * CHUNK
    row = lambda a: lax.dot_general(pick, pad(a.astype(BF16)), nt, preferred_element_type=F32)[0:1, :]
    base_row = row(base_hi) * CHUNK + row(base_lo)
    tot_row = row(tot)
    s_col = lax.broadcasted_iota(jnp.int32, (CAP, LANES), 0).astype(F32)
    owner = ((base_row <= s_col) & (s_col < base_row + tot_row)).astype(BF16)

    own = lambda a: jnp.dot(owner, pad(a.astype(BF16)), preferred_element_type=F32)
    incl_s = own(incl)
    base_s = own(base_hi) * CHUNK + own(base_lo)
    cvals = lax.broadcasted_iota(jnp.int32, (N_CHUNKS, LANES), 0).astype(F32)
    chunk_s = own(cvals)
    s_loc = lax.broadcasted_iota(jnp.int32, (CAP, LANES), 0).astype(F32) - base_s
    j_s = jnp.sum((incl_s <= s_loc).astype(F32), axis=1, keepdims=True)
    idx_ref[0] = (chunk_s[:, 0:1] * CHUNK + j_s).astype(jnp.int32)
    a1, a2, a3 = _split3(aff)
    aff_s = (own(a1) + own(a2)) + own(a3)
    lane = lax.broadcasted_iota(jnp.int32, (CAP, LANES), 1).astype(F32)
    g_ref[0] = jnp.sum(jnp.where(lane == j_s, aff_s, 0.0), axis=1, keepdims=True)


def _select(aff3):
    full = lambda e: (0, 0, 0)
    per = lambda e: (e, 0, 0)
    return pl.pallas_call(
        _select_kernel,
        grid=(N_EXPERTS,),
        in_specs=[pl.BlockSpec((N_EXPERTS, N_CHUNKS, CHUNK), full)],
        out_specs=[
            pl.BlockSpec((1, CAP, 1), per),
            pl.BlockSpec((1, CAP, 1), per),
            pl.BlockSpec((1, N_CHUNKS, 1), per),
            pl.BlockSpec((1, N_CHUNKS, CHUNK), per),
        ],
        out_shape=[
            jax.ShapeDtypeStruct((N_EXPERTS, CAP, 1), jnp.int32),
            jax.ShapeDtypeStruct((N_EXPERTS, CAP, 1), F32),
            jax.ShapeDtypeStruct((N_EXPERTS, N_CHUNKS, 1), jnp.int32),
            jax.ShapeDtypeStruct((N_EXPERTS, N_CHUNKS, CHUNK), jnp.int32),
        ],
        scratch_shapes=[pltpu.VMEM((N_EXPERTS, N_CHUNKS, CHUNK), F32)],
        compiler_params=_cparams(("arbitrary",)),
        name="select",
    )(aff3)


FFN_TF = 512
FFN_NF = D_FF // FFN_TF
FFN_TN = 256
FFN_GATHER_UNROLL = 8
FFN_VMEM_LIMIT = 60 * 1024 * 1024


def _ffn_kernel(idx_ref, h2_hbm, g_ref, wg_ref, wu_ref, wd_ref, y_ref, xs_ref, xb_ref, act_ref, acc_ref,
                sem):
    e = pl.program_id(0)
    f = pl.program_id(1)
    nf = pl.num_programs(1)

    def start_row(ee, s):
        t8 = pl.multiple_of(idx_ref[ee, s] * SUBLANES, SUBLANES)
        s8 = pl.multiple_of(s * SUBLANES, SUBLANES)
        pltpu.make_async_copy(h2_hbm.at[:, pl.ds(t8, SUBLANES), :],
                              xs_ref.at[:, pl.ds(s8, SUBLANES), :], sem).start()

    def wait_rows():
        pltpu.make_async_copy(h2_hbm.at[:, pl.ds(0, CAP * SUBLANES), :], xs_ref, sem).wait()

    @pl.when((e == 0) & (f == 0))
    def _():
        def body(s, carry):
            start_row(0, s)
            return carry
        lax.fori_loop(0, CAP, body, 0, unroll=FFN_GATHER_UNROLL)

    nxt = jnp.where(e + 1 < N_EXPERTS, e + 1, 0)
    per_step = CAP // FFN_NF

    @pl.when(f == 0)
    def _():
        wait_rows()
        for kk in range(D_MODEL // LANES):
            hh, jj = divmod(kk, SUBLANES)
            xb_ref[:, kk * LANES:(kk + 1) * LANES] = (
                xs_ref[hh, pl.ds(jj, CAP, stride=SUBLANES), :].astype(BF16))
        acc_ref[...] = jnp.zeros_like(acc_ref)
        for s in range(per_step * FFN_NF, CAP):
            start_row(nxt, s)

    for j in range(per_step):
        start_row(nxt, f * per_step + j)

    for c in range(FFN_TF // FFN_TN):
        cs = slice(c * FFN_TN, (c + 1) * FFN_TN)
        a = jnp.dot(xb_ref[...], wg_ref[0, :, cs].astype(BF16), preferred_element_type=F32)
        u = jnp.dot(xb_ref[...], wu_ref[0, :, cs].astype(BF16), preferred_element_type=F32)
        act_ref[:, cs] = (a * jax.nn.sigmoid(a) * u).astype(BF16)
    for n in range(D_MODEL // FFN_TN):
        ns = slice(n * FFN_TN, (n + 1) * FFN_TN)
        acc_ref[:, ns] += jnp.dot(act_ref[...], wd_ref[0, :, ns].astype(BF16), preferred_element_type=F32)

    @pl.when(f == nf - 1)
    def _():
        y_ref[0] = (acc_ref[...] * g_ref[0]).astype(BF16)

    @pl.when((e == N_EXPERTS - 1) & (f == nf - 1))
    def _():
        wait_rows()


def _ffn(idx, h2, g, w_gate, w_up, w_down):
    return pl.pallas_call(
        _ffn_kernel,
        grid_spec=pltpu.PrefetchScalarGridSpec(
            num_scalar_prefetch=1,
            grid=(N_EXPERTS, D_FF // FFN_TF),
            in_specs=[
                pl.BlockSpec(memory_space=pl.ANY),
                pl.BlockSpec((1, CAP, 1), lambda e, f, idx: (e, 0, 0)),
                pl.BlockSpec((1, D_MODEL, FFN_TF), lambda e, f, idx: (e, 0, f)),
                pl.BlockSpec((1, D_MODEL, FFN_TF), lambda e, f, idx: (e, 0, f)),
                pl.BlockSpec((1, FFN_TF, D_MODEL), lambda e, f, idx: (e, f, 0)),
            ],
            out_specs=pl.BlockSpec((1, CAP, D_MODEL), lambda e, f, idx: (e, 0, 0)),
            scratch_shapes=[
                pltpu.VMEM((H2_HALVES, CAP * SUBLANES, LANES), F32),
                pltpu.VMEM((CAP, D_MODEL), BF16),
                pltpu.VMEM((CAP, FFN_TF), BF16),
                pltpu.VMEM((CAP, D_MODEL), F32),
                pltpu.SemaphoreType.DMA(()),
            ],
        ),
        out_shape=jax.ShapeDtypeStruct((N_EXPERTS, CAP, D_MODEL), BF16),
        compiler_params=_cparams(("arbitrary", "arbitrary"), FFN_VMEM_LIMIT),
        name="ffn",
    )(idx, h2, g, w_gate, w_up, w_down)


CMB_CPT = 2
CMB_T = CMB_CPT * CHUNK
CMB_K = 256
CMB_GRAN = 16
CMB_FIRST = 64
CMB_MAIN = N_EXPERTS * CMB_FIRST
CMB_SIZES = (128, 64, 32, 16)
CMB_OVF = N_EXPERTS * sum(CMB_SIZES)
assert CMB_MAIN % CMB_K == 0 and CMB_OVF % CMB_K == 0
assert CMB_T + CMB_GRAN <= CMB_FIRST + sum(CMB_SIZES)


def _combine_kernel(base_ref, pos_ref, xmid_ref, mod_ref, g2_ref, b2_ref, y_hbm, o_ref,
                    stage_ref, ovf_ref, spread_ref, rcol_ref, acc_ref, sem, ovf_sem):
    i = pl.program_id(0)
    slot = i & 1
    y_rows = N_EXPERTS * CAP

    def windows(tile):
        starts, extra = [], []
        for e in range(N_EXPERTS):
            s0 = e * CAP + base_ref[e * (N_CHUNKS + 1) + tile * CMB_CPT]
            s1 = e * CAP + base_ref[e * (N_CHUNKS + 1) + (tile + 1) * CMB_CPT]
            start = jnp.minimum(s0 & -CMB_GRAN, y_rows - CMB_FIRST)
            starts.append(start)
            extra.append(jnp.maximum(s1 - (start + CMB_FIRST) + CMB_GRAN - 1, 0) & -CMB_GRAN)
        return starts, extra

    def stage_main(tile, slot_):
        for e, start in enumerate(windows(tile)[0]):
            pltpu.make_async_copy(
                y_hbm.at[pl.ds(pl.multiple_of(start, CMB_GRAN), CMB_FIRST)],
                stage_ref.at[slot_, pl.ds(e * CMB_FIRST, CMB_FIRST)], sem.at[slot_]).start()

    @pl.when(i == 0)
    def _():
        stage_ref[...] = jnp.zeros_like(stage_ref)
        ovf_ref[...] = jnp.zeros_like(ovf_ref)
        ee = lax.broadcasted_iota(jnp.int32, (LANES, CMB_MAIN), 0)
        cc = lax.broadcasted_iota(jnp.int32, (LANES, CMB_MAIN), 1)
        lo = ee * CMB_FIRST
        spread_ref[...] = ((cc >= lo) & (cc < lo + CMB_FIRST)).astype(F32).astype(BF16)
        c1 = lax.broadcasted_iota(jnp.int32, (SUBLANES, CMB_MAIN), 1)
        owner = jnp.zeros((SUBLANES, CMB_MAIN), jnp.int32)
        for e in range(1, N_EXPERTS):
            owner = owner + (c1 >= e * CMB_FIRST).astype(jnp.int32)
        rcol_ref[...] = (c1 - owner * CMB_FIRST).astype(F32)
        stage_main(0, 0)

    @pl.when(i + 1 < pl.num_programs(0))
    def _():
        stage_main(i + 1, 1 - slot)

    starts, extra = windows(i)
    pltpu.make_async_copy(y_hbm.at[pl.ds(0, CMB_MAIN)], stage_ref.at[slot], sem.at[slot]).wait()

    lane = lax.broadcasted_iota(jnp.int32, (1, LANES), 1)
    shift = jnp.zeros((1, LANES), jnp.int32)
    for e in range(N_EXPERTS):
        shift = jnp.where(lane == e, e * CAP - starts[e], shift)
    pos = pos_ref[...]
    rel = jnp.where(pos >= 0, pos + shift, -1)
    spread = jnp.dot(rel.astype(F32).astype(BF16), spread_ref[...], preferred_element_type=F32)
    for k in range(CMB_MAIN // CMB_K):
        ks = slice(k * CMB_K, (k + 1) * CMB_K)
        w = (spread[:, ks] == rcol_ref[0:1, ks]).astype(F32).astype(BF16)
        part = jnp.dot(w, stage_ref[slot, ks, :], preferred_element_type=F32)
        if k == 0:
            acc_ref[...] = part
        else:
            acc_ref[...] += part

    n_extra = extra[0]
    for e in range(1, N_EXPERTS):
        n_extra = n_extra + extra[e]

    @pl.when(n_extra > 0)
    def _():
        copies, rowidx = [], []
        ooff = jnp.int32(0)
        for e in range(N_EXPERTS):
            done = jnp.int32(0)
            for size in CMB_SIZES:
                pred = (extra[e] & size) != 0
                src = pl.multiple_of(starts[e] + CMB_FIRST + done, CMB_GRAN)
                dst = pl.multiple_of(ooff + done, CMB_GRAN)
                copies.append((pred, pltpu.make_async_copy(
                    y_hbm.at[pl.ds(src, size)], ovf_ref.at[pl.ds(dst, size)], ovf_sem)))
                done = done + jnp.where(pred, size, 0)
            r = rel[:, e:e + 1]
            rowidx.append(jnp.where(r >= CMB_FIRST, r - CMB_FIRST + ooff, -1))
            ooff = ooff + extra[e]
        for pred, cp in copies:
            @pl.when(pred)
            def _():
                cp.start()
        for pred, cp in copies:
            @pl.when(pred)
            def _():
                cp.wait()

        def body(k, carry):
            r0 = pl.multiple_of(k * CMB_K, CMB_K)
            col = lax.broadcasted_iota(jnp.int32, (CMB_T, CMB_K), 1) + r0
            w = jnp.zeros((CMB_T, CMB_K), F32)
            for e in range(N_EXPERTS):
                w = w + (rowidx[e] == col).astype(F32)
            acc_ref[...] += jnp.dot(w.astype(BF16), ovf_ref[pl.ds(r0, CMB_K), :],
                                    preferred_element_type=F32)
            return carry

        lax.fori_loop(0, (n_extra + CMB_K - 1) // CMB_K, body, 0)

    gate2 = mod_ref[5:6, :]
    o_ref[...] = _ln(DEEPNORM_ALPHA * xmid_ref[...] + gate2 * acc_ref[...]) * g2_ref[...] + b2_ref[...]


def _combine(base_flat, pos_te, xmid, mod6, g2, b2, y_flat):
    n = xmid.shape[0]
    row = lambda i, b: (i, 0)
    fixed = lambda i, b: (0, 0)
    return pl.pallas_call(
        _combine_kernel,
        grid_spec=pltpu.PrefetchScalarGridSpec(
            num_scalar_prefetch=1,
            grid=(n // CMB_T,),
            in_specs=[
                pl.BlockSpec((CMB_T, LANES), row),
                pl.BlockSpec((CMB_T, D_MODEL), row),
                pl.BlockSpec((N_MOD, D_MODEL), fixed),
                pl.BlockSpec((1, D_MODEL), fixed),
                pl.BlockSpec((1, D_MODEL), fixed),
                pl.BlockSpec(memory_space=pl.ANY),
            ],
            out_specs=pl.BlockSpec((CMB_T, D_MODEL), row),
            scratch_shapes=[
                pltpu.VMEM((2, CMB_MAIN, D_MODEL), BF16),
                pltpu.VMEM((CMB_OVF, D_MODEL), BF16),
                pltpu.VMEM((LANES, CMB_MAIN), BF16),
                pltpu.VMEM((SUBLANES, CMB_MAIN), F32),
                pltpu.VMEM((CMB_T, D_MODEL), F32),
                pltpu.SemaphoreType.DMA((2,)),
                pltpu.SemaphoreType.DMA(()),
            ],
        ),
        out_shape=jax.ShapeDtypeStruct((n, D_MODEL), F32),
        compiler_params=_cparams(("arbitrary",)),
        name="combine",
    )(base_flat, pos_te, xmid, mod6, g2, b2, y_flat)


def _rope_tables(n):
    rows = n // GRID_W
    inv = ROPE_THETA ** (-np.arange(0, ROPE_AXIS_DIM, 2, dtype=np.float64) / ROPE_AXIS_DIM)
    ang_r = np.arange(rows, dtype=np.float64)[:, None] * inv[None, :]
    ang_c = np.arange(GRID_W, dtype=np.float64)[:, None] * inv[None, :]
    zr, zc = np.zeros_like(ang_r), np.zeros_like(ang_c)
    cr, sr, cc, sc = np.cos(ang_r), np.sin(ang_r), np.cos(ang_c), np.sin(ang_c)
    trow = np.stack([np.concatenate(p, axis=1) for p in
                     ([cr, cr, zr, zr], [-sr, zr, zr, zr], [zr, sr, zr, zr])])
    tcol = np.stack([np.concatenate(p, axis=1) for p in
                     ([zc, zc, cc, cc], [zc, zc, -sc, zc], [zc, zc, zc, sc])])
    return jnp.asarray(trow, F32), jnp.asarray(tcol, F32)


def kernel(x, c, ctx, c_ctx, w_mod, b_mod, w_in, b_in, w_dw, b_dw, conv_ln_g, conv_ln_b, sink,
           w_out, b_out, ln1_g, ln1_b, w_router, w_gate, w_up, w_down, ln2_g, ln2_b):
    assert x.shape == (1, SEQ, D_MODEL) and ctx.shape == (1, CTX_LEN, D_MODEL)
    assert w_mod.shape[0] == DEPTH
    x2 = x[0]
    ctx2 = ctx[0]
    r1 = lambda a: a.reshape(1, -1)

    ct = jnp.stack([c[0], c_ctx], axis=1)
    mod = _mod(ct, w_mod[0], r1(b_mod[0]))
    mod6 = mod[0].reshape(N_MOD, D_MODEL)
    modc6 = mod[1].reshape(N_MOD, D_MODEL)

    w_in_bf = w_in[0].astype(BF16)
    w_out_bf = w_out[0].astype(BF16)
    trow, tcol = _rope_tables(SEQ)
    u, q, k, v = _in_proj(x2, mod6, w_in_bf, r1(b_in[0]), trow, tcol)
    kx, vx = _ctx_kv(ctx2, modc6, w_in_bf, r1(b_in[0]))
    a_conv = _conv(u, w_dw[0], r1(b_dw[0]), r1(conv_ln_g[0]), r1(conv_ln_b[0]))
    a_attn = _attn(sink[0], q, k, v, kx, vx)
    wr = jnp.pad(w_router[0], ((0, 0), (0, LANES - N_EXPERTS)))
    wr_hi = wr.astype(BF16)
    wr_lo = (wr - wr_hi.astype(F32)).astype(BF16)
    xmid, h2, aff_t = _out_proj(a_conv, a_attn, x2, mod6, w_out_bf, r1(b_out[0]),
                                r1(ln1_g[0]), r1(ln1_b[0]), wr_hi, wr_lo)

    idx, g, base, pos = _select(aff_t.reshape(N_EXPERTS, N_CHUNKS, CHUNK))
    y = _ffn(idx.reshape(N_EXPERTS, CAP), h2, g, w_gate[0], w_up[0], w_down[0])

    base_flat = jnp.concatenate(
        [base.reshape(N_EXPERTS, N_CHUNKS), jnp.full((N_EXPERTS, 1), CAP, jnp.int32)], axis=1).reshape(-1)
    pos_te = jnp.pad(pos.reshape(N_EXPERTS, SEQ).T, ((0, 0), (0, LANES - N_EXPERTS)),
                     constant_values=-1)
    out = _combine(base_flat, pos_te, xmid, mod6, r1(ln2_g[0]), r1(ln2_b[0]),
                   y.reshape(N_EXPERTS * CAP, D_MODEL))
    return out[None]
```

```python
import numpy as np

import jax
import jax.numpy as jnp
from jax import lax
from jax.experimental import pallas as pl
from jax.experimental.pallas import tpu as pltpu

D_MODEL = 2048
SEQ = 8192
GRID_W = 64
CTX_LEN = 256
HEAD_DIM = 128
N_Q_HEADS = 8
N_KV_HEADS = 2
Q_PER_KV = N_Q_HEADS // N_KV_HEADS
ATTN_WIDTH = N_Q_HEADS * HEAD_DIM
KV_WIDTH = N_KV_HEADS * HEAD_DIM
CONV_WIDTH = D_MODEL - ATTN_WIDTH
CONV_KSIZE = 31
WINDOW = 128
ROPE_THETA = 10000.0
ROPE_AXIS_DIM = HEAD_DIM // 2
N_EXPERTS = 16
EC_CAPACITY = 2
CAP = EC_CAPACITY * SEQ // N_EXPERTS
D_FF = 5632
N_MOD = 6
LN_EPS = 1e-5
NEG_INF = -1e30
DEPTH = 1
DEEPNORM_ALPHA = (2.0 * DEPTH) ** 0.25
Q_START = 2 * CONV_WIDTH
K_START = Q_START + ATTN_WIDTH
V_START = K_START + KV_WIDTH
IN_COLS = V_START + KV_WIDTH

LANES = 128
SUBLANES = 8
H2_HALVES = D_MODEL // (SUBLANES * LANES)
CHUNK = LANES
N_CHUNKS = SEQ // CHUNK
VMEM_LIMIT = 56 * 1024 * 1024

F32 = jnp.float32
BF16 = jnp.bfloat16


def _ln(xv):
    mu = jnp.mean(xv, axis=-1, keepdims=True)
    xc = xv - mu
    var = jnp.mean(xc * xc, axis=-1, keepdims=True)
    return xc * lax.rsqrt(var + LN_EPS)


def _cparams(sem, vmem=VMEM_LIMIT):
    return pltpu.CompilerParams(dimension_semantics=sem, vmem_limit_bytes=vmem)


MOD_TN = 1024
MOD_UNROLL = 4


def _mod_kernel(ct_ref, w_ref, b_ref, o_ref, s0_ref, s1_ref):
    @pl.when(pl.program_id(0) == 0)
    def _():
        ct = ct_ref[...]
        s = ct * jax.nn.sigmoid(ct)
        s0_ref[...] = jnp.broadcast_to(s[:, 0:1], (D_MODEL, LANES))
        s1_ref[...] = jnp.broadcast_to(s[:, 1:2], (D_MODEL, LANES))

    def body(kb, acc):
        k0 = pl.multiple_of(kb * SUBLANES, SUBLANES)
        s0 = s0_ref[pl.ds(k0, SUBLANES), :]
        s1 = s1_ref[pl.ds(k0, SUBLANES), :]
        a0, a1 = [], []
        for j in range(MOD_TN // LANES):
            w = w_ref[pl.ds(k0, SUBLANES), j * LANES:(j + 1) * LANES]
            a0.append(acc[0][j] + w * s0)
            a1.append(acc[1][j] + w * s1)
        return tuple(a0), tuple(a1)

    zeros = tuple(jnp.zeros((SUBLANES, LANES), F32) for _ in range(MOD_TN // LANES))
    acc0, acc1 = lax.fori_loop(0, D_MODEL // SUBLANES, body, (zeros, zeros), unroll=MOD_UNROLL)
    for j in range(MOD_TN // LANES):
        b = b_ref[:, j * LANES:(j + 1) * LANES]
        o_ref[0:1, j * LANES:(j + 1) * LANES] = jnp.sum(acc0[j], axis=0, keepdims=True) + b
        o_ref[1:2, j * LANES:(j + 1) * LANES] = jnp.sum(acc1[j], axis=0, keepdims=True) + b


def _mod(ct, w_mod, b_mod):
    n_out = N_MOD * D_MODEL
    return pl.pallas_call(
        _mod_kernel,
        grid=(n_out // MOD_TN,),
        in_specs=[
            pl.BlockSpec((D_MODEL, 2), lambda j: (0, 0)),
            pl.BlockSpec((D_MODEL, MOD_TN), lambda j: (0, j)),
            pl.BlockSpec((1, MOD_TN), lambda j: (0, j)),
        ],
        out_specs=pl.BlockSpec((2, MOD_TN), lambda j: (0, j)),
        out_shape=jax.ShapeDtypeStruct((2, n_out), F32),
        scratch_shapes=[pltpu.VMEM((D_MODEL, LANES), F32), pltpu.VMEM((D_MODEL, LANES), F32)],
        compiler_params=_cparams(("arbitrary",)),
        name="mod",
    )(ct, w_mod, b_mod)


IN_TM = 512
IN_SUB = 256
IN_TN = 512


def _rope(p, cos, sina, sinb):
    return (p * cos + pltpu.roll(p, HEAD_DIM - ROPE_AXIS_DIM // 2, axis=1) * sina
            + pltpu.roll(p, ROPE_AXIS_DIM // 2, axis=1) * sinb)


def _in_kernel(x_ref, mod_ref, w_ref, b_ref, trow_ref, tcol_ref, u_ref, q_ref, k_ref, v_ref):
    shift = mod_ref[0:1, :]
    scale = mod_ref[1:2, :]
    grid_rows = IN_SUB // GRID_W

    for sb in range(IN_TM // IN_SUB):
        rs = slice(sb * IN_SUB, (sb + 1) * IN_SUB)
        h = (_ln(x_ref[rs, :]) * (1.0 + scale) + shift).astype(BF16)

        def table(kind):
            by_row = jnp.concatenate(
                [jnp.broadcast_to(trow_ref[kind, sb * grid_rows + r:sb * grid_rows + r + 1, :],
                                  (GRID_W, HEAD_DIM)) for r in range(grid_rows)], axis=0)
            by_col = jnp.concatenate([tcol_ref[kind]] * grid_rows, axis=0)
            return by_row + by_col

        cos, sina, sinb = table(0), table(1), table(2)

        def proj(c0, width):
            return (jnp.dot(h, w_ref[:, c0:c0 + width], preferred_element_type=F32)
                    + b_ref[:, c0:c0 + width])

        for j in range(CONV_WIDTH // IN_TN):
            pv = proj(j * IN_TN, IN_TN)
            pg = proj(CONV_WIDTH + j * IN_TN, IN_TN)
            u_ref[rs, j * IN_TN:(j + 1) * IN_TN] = pv * jax.nn.sigmoid(pg)
        for j in range(ATTN_WIDTH // IN_TN):
            pq = proj(Q_START + j * IN_TN, IN_TN)
            for hh in range(IN_TN // HEAD_DIM):
                c0 = j * IN_TN + hh * HEAD_DIM
                q_ref[rs, c0:c0 + HEAD_DIM] = _rope(
                    pq[:, hh * HEAD_DIM:(hh + 1) * HEAD_DIM], cos, sina, sinb).astype(BF16)
        pk = proj(K_START, KV_WIDTH)
        for hh in range(N_KV_HEADS):
            k_ref[rs, hh * HEAD_DIM:(hh + 1) * HEAD_DIM] = _rope(
                pk[:, hh * HEAD_DIM:(hh + 1) * HEAD_DIM], cos, sina, sinb).astype(BF16)
        v_ref[rs, :] = proj(V_START, KV_WIDTH).astype(BF16)


def _in_proj(x2, mod6, w_in_bf, b_in, trow, tcol):
    n = x2.shape[0]
    row = lambda i: (i, 0)
    fixed = lambda i: (0, 0)
    return pl.pallas_call(
        _in_kernel,
        grid=(n // IN_TM,),
        in_specs=[
            pl.BlockSpec((IN_TM, D_MODEL), row),
            pl.BlockSpec((N_MOD, D_MODEL), fixed),
            pl.BlockSpec((D_MODEL, IN_COLS), fixed),
            pl.BlockSpec((1, IN_COLS), fixed),
            pl.BlockSpec((3, IN_TM // GRID_W, HEAD_DIM), lambda i: (0, i, 0)),
            pl.BlockSpec((3, GRID_W, HEAD_DIM), lambda i: (0, 0, 0)),
        ],
        out_specs=[
            pl.BlockSpec((IN_TM, CONV_WIDTH), row),
            pl.BlockSpec((IN_TM, ATTN_WIDTH), row),
            pl.BlockSpec((IN_TM, KV_WIDTH), row),
            pl.BlockSpec((IN_TM, KV_WIDTH), row),
        ],
        out_shape=[
            jax.ShapeDtypeStruct((n, CONV_WIDTH), F32),
            jax.ShapeDtypeStruct((n, ATTN_WIDTH), BF16),
            jax.ShapeDtypeStruct((n, KV_WIDTH), BF16),
            jax.ShapeDtypeStruct((n, KV_WIDTH), BF16),
        ],
        compiler_params=_cparams(("arbitrary",)),
        name="in_proj",
    )(x2, mod6, w_in_bf, b_in, trow, tcol)


def _ctx_kernel(x_ref, mod_ref, w_ref, b_ref, kc_ref, vc_ref):
    shift = mod_ref[0:1, :]
    scale = mod_ref[1:2, :]
    h = (_ln(x_ref[...]) * (1.0 + scale) + shift).astype(BF16)
    p = jnp.dot(h, w_ref[...], preferred_element_type=F32) + b_ref[...]
    kc_ref[...] = p[:, :KV_WIDTH].astype(BF16)
    vc_ref[...] = p[:, KV_WIDTH:].astype(BF16)


def _ctx_kv(ctx2, modc6, w_in_bf, b_in):
    kvw = 2 * KV_WIDTH
    fixed = lambda i: (0, 0)
    return pl.pallas_call(
        _ctx_kernel,
        grid=(1,),
        in_specs=[
            pl.BlockSpec((CTX_LEN, D_MODEL), fixed),
            pl.BlockSpec((N_MOD, D_MODEL), fixed),
            pl.BlockSpec((D_MODEL, kvw), lambda i: (0, K_START // kvw)),
            pl.BlockSpec((1, kvw), lambda i: (0, K_START // kvw)),
        ],
        out_specs=[pl.BlockSpec((CTX_LEN, KV_WIDTH), fixed), pl.BlockSpec((CTX_LEN, KV_WIDTH), fixed)],
        out_shape=[jax.ShapeDtypeStruct((CTX_LEN, KV_WIDTH), BF16),
                   jax.ShapeDtypeStruct((CTX_LEN, KV_WIDTH), BF16)],
        compiler_params=_cparams(("arbitrary",)),
        name="ctx_kv",
    )(ctx2, modc6, w_in_bf, b_in)


CONV_T = 512
CONV_HALO = 16
CONV_ROWS = 64


def _conv_kernel(up_ref, uc_ref, un_ref, w_ref, bdw_ref, g_ref, b_ref, o_ref, buf_ref, acc_ref):
    i = pl.program_id(0)
    last = pl.num_programs(0) - 1
    for lg in range(CONV_WIDTH // LANES):
        ls = slice(lg * LANES, (lg + 1) * LANES)
        buf_ref[lg, 0:CONV_HALO, :] = jnp.where(i > 0, up_ref[:, ls], 0.0)
        buf_ref[lg, CONV_HALO:CONV_HALO + CONV_T, :] = uc_ref[:, ls]
        buf_ref[lg, CONV_HALO + CONV_T:, :] = jnp.where(i < last, un_ref[:, ls], 0.0)
    off = CONV_HALO - CONV_KSIZE // 2

    span = CONV_ROWS + 2 * CONV_HALO

    def body(r, carry):
        r0 = pl.multiple_of(r * CONV_ROWS, CONV_ROWS)
        for lg in range(CONV_WIDTH // LANES):
            ls = slice(lg * LANES, (lg + 1) * LANES)
            win = buf_ref.at[lg, pl.ds(r0, span)]
            acc = jnp.zeros((CONV_ROWS, LANES), F32)
            for t in range(CONV_KSIZE):
                acc = acc + win[pl.ds(off + t, CONV_ROWS), :] * w_ref[t:t + 1, ls]
            acc_ref[pl.ds(r0, CONV_ROWS), ls] = acc
        return carry

    lax.fori_loop(0, CONV_T // CONV_ROWS, body, 0)
    y = _ln(acc_ref[...] + bdw_ref[...]) * g_ref[...] + b_ref[...]
    o_ref[...] = (y * jax.nn.sigmoid(y)).astype(BF16)


def _conv(u, w_dw, b_dw, ln_g, ln_b):
    n = u.shape[0]
    hb = CONV_T // CONV_HALO
    nhb = n // CONV_HALO
    fixed = lambda i: (0, 0)
    return pl.pallas_call(
        _conv_kernel,
        grid=(n // CONV_T,),
        in_specs=[
            pl.BlockSpec((CONV_HALO, CONV_WIDTH), lambda i: (jnp.maximum(i * hb - 1, 0), 0)),
            pl.BlockSpec((CONV_T, CONV_WIDTH), lambda i: (i, 0)),
            pl.BlockSpec((CONV_HALO, CONV_WIDTH), lambda i: (jnp.minimum((i + 1) * hb, nhb - 1), 0)),
            pl.BlockSpec((CONV_KSIZE, CONV_WIDTH), fixed),
            pl.BlockSpec((1, CONV_WIDTH), fixed),
            pl.BlockSpec((1, CONV_WIDTH), fixed),
            pl.BlockSpec((1, CONV_WIDTH), fixed),
        ],
        out_specs=pl.BlockSpec((CONV_T, CONV_WIDTH), lambda i: (i, 0)),
        out_shape=jax.ShapeDtypeStruct((n, CONV_WIDTH), BF16),
        scratch_shapes=[pltpu.VMEM((CONV_WIDTH // LANES, CONV_T + 2 * CONV_HALO, LANES), F32),
                        pltpu.VMEM((CONV_T, CONV_WIDTH), F32)],
        compiler_params=_cparams(("arbitrary",)),
        name="conv",
    )(u, u, u, w_dw, b_dw, ln_g, ln_b)


ATT_T = 128
ATT_NB = 2
assert ATT_T == WINDOW
LOG2E = 1.4426950408889634


def _attn_kernel(sink_ref, q_ref, kp_ref, kc_ref, kn_ref, vp_ref, vc_ref, vn_ref, kx_ref, vx_ref, o_ref,
                 bias_ref):
    i = pl.program_id(0)
    last = pl.num_programs(0) - 1
    scale = HEAD_DIM ** -0.5
    rows = Q_PER_KV * ATT_T

    @pl.when((i <= 1) | (i == last))
    def _():
        qi = lax.broadcasted_iota(jnp.int32, (rows, 3 * ATT_T), 0) & (ATT_T - 1)
        m = lax.broadcasted_iota(jnp.int32, (rows, 3 * ATT_T), 1)
        band = jnp.abs(m - qi - ATT_T) <= WINDOW
        for b in range(ATT_NB):
            kpos = (i * ATT_NB + b - 1) * ATT_T + m
            bias_ref[b] = jnp.where(band & (kpos >= 0) & (kpos < SEQ), 0.0, NEG_INF)

    hrow = jnp.right_shift(lax.broadcasted_iota(jnp.int32, (rows, 1), 0), ATT_T.bit_length() - 1)
    nt = (((1,), (1,)), ((), ()))
    def key_blocks(b, ls, p_ref, c_ref, n_ref):
        blocks = ([p_ref[:, ls]] + [c_ref[j * ATT_T:(j + 1) * ATT_T, ls] for j in range(ATT_NB)]
                  + [n_ref[:, ls]])
        return jnp.concatenate(blocks[b:b + 3], axis=0)

    def scores(b, g):
        rs = slice(b * ATT_T, (b + 1) * ATT_T)
        ls = slice(g * HEAD_DIM, (g + 1) * HEAD_DIM)
        qs = jnp.concatenate(
            [q_ref[rs, (g * Q_PER_KV + hh) * HEAD_DIM:(g * Q_PER_KV + hh + 1) * HEAD_DIM]
             for hh in range(Q_PER_KV)], axis=0)
        kw = key_blocks(b, ls, kp_ref, kc_ref, kn_ref)
        s_win = lax.dot_general(qs, kw, nt, preferred_element_type=F32) + bias_ref[b]
        s_ctx = lax.dot_general(qs, kx_ref[:, ls], nt, preferred_element_type=F32)
        return s_win, s_ctx

    def weights(g, s_win, s_ctx):
        s_sink = jnp.zeros((rows, 1), F32)
        for hh in range(Q_PER_KV):
            s_sink = jnp.where(hrow == hh, sink_ref[g * Q_PER_KV + hh], s_sink)
        raw_max = jnp.maximum(jnp.max(s_win, axis=-1, keepdims=True), jnp.max(s_ctx, axis=-1, keepdims=True))
        mx2 = jnp.maximum(raw_max * scale, s_sink) * LOG2E
        e_win = jnp.exp2(s_win * (scale * LOG2E) - mx2)
        e_ctx = jnp.exp2(s_ctx * (scale * LOG2E) - mx2)
        den = (jnp.sum(e_win, axis=-1, keepdims=True) + jnp.sum(e_ctx, axis=-1, keepdims=True)
               + jnp.exp2(s_sink * LOG2E - mx2))
        return e_win.astype(BF16), e_ctx.astype(BF16), den

    def values(b, g, e_win, e_ctx, den):
        rs = slice(b * ATT_T, (b + 1) * ATT_T)
        ls = slice(g * HEAD_DIM, (g + 1) * HEAD_DIM)
        vw = key_blocks(b, ls, vp_ref, vc_ref, vn_ref)
        o = (jnp.dot(e_ctx, vx_ref[:, ls], preferred_element_type=F32)
             + jnp.dot(e_win, vw, preferred_element_type=F32)) * (1.0 / den)
        for hh in range(Q_PER_KV):
            c0 = (g * Q_PER_KV + hh) * HEAD_DIM
            o_ref[rs, c0:c0 + HEAD_DIM] = o[hh * ATT_T:(hh + 1) * ATT_T, :].astype(BF16)

    chains = [(b, g) for b in range(ATT_NB) for g in range(N_KV_HEADS)]
    s_next = scores(*chains[0])
    for c, (b, g) in enumerate(chains):
        s_cur = s_next
        if c + 1 < len(chains):
            s_next = scores(*chains[c + 1])
        values(b, g, *weights(g, *s_cur))


def _attn(sink, q, k, v, kx, vx):
    n = q.shape[0]
    nb = n // ATT_T
    prev = lambda i, s: (jnp.maximum(i * ATT_NB - 1, 0), 0)
    cur = lambda i, s: (i, 0)
    nxt = lambda i, s: (jnp.minimum((i + 1) * ATT_NB, nb - 1), 0)
    fixed = lambda i, s: (0, 0)
    edge = lambda im: pl.BlockSpec((ATT_T, KV_WIDTH), im)
    own = pl.BlockSpec((ATT_NB * ATT_T, KV_WIDTH), cur)
    return pl.pallas_call(
        _attn_kernel,
        grid_spec=pltpu.PrefetchScalarGridSpec(
            num_scalar_prefetch=1,
            grid=(nb // ATT_NB,),
            in_specs=[
                pl.BlockSpec((ATT_NB * ATT_T, ATTN_WIDTH), cur),
                edge(prev), own, edge(nxt),
                edge(prev), own, edge(nxt),
                pl.BlockSpec((CTX_LEN, KV_WIDTH), fixed),
                pl.BlockSpec((CTX_LEN, KV_WIDTH), fixed),
            ],
            out_specs=pl.BlockSpec((ATT_NB * ATT_T, ATTN_WIDTH), cur),
            scratch_shapes=[pltpu.VMEM((ATT_NB, Q_PER_KV * ATT_T, 3 * ATT_T), F32)],
        ),
        out_shape=jax.ShapeDtypeStruct((n, ATTN_WIDTH), BF16),
        compiler_params=_cparams(("arbitrary",)),
        name="attn",
    )(sink, q, k, k, k, v, v, v, kx, vx)


OUT_TM = 512
OUT_SUB = 128


def _out_kernel(ac_ref, aa_ref, x_ref, mod_ref, w_ref, b_ref, g1_ref, b1_ref, wrh_ref, wrl_ref,
                xmid_ref, h2_ref, aff_ref):
    gate1 = mod_ref[2:3, :]
    n_sub = OUT_TM // OUT_SUB

    def mix_of(sb):
        rs = slice(sb * OUT_SUB, (sb + 1) * OUT_SUB)
        return (jnp.dot(ac_ref[rs, :], w_ref[:CONV_WIDTH, :], preferred_element_type=F32)
                + jnp.dot(aa_ref[rs, :], w_ref[CONV_WIDTH:, :], preferred_element_type=F32) + b_ref[...])

    mix_next = mix_of(0)
    for sb in range(n_sub):
        rs = slice(sb * OUT_SUB, (sb + 1) * OUT_SUB)
        mix = mix_next
        if sb + 1 < n_sub:
            mix_next = mix_of(sb + 1)
        xmid = _ln(DEEPNORM_ALPHA * x_ref[rs, :] + gate1 * mix) * g1_ref[...] + b1_ref[...]
        xmid_ref[rs, :] = xmid
        h2 = _ln(xmid) * (1.0 + mod_ref[4:5, :]) + mod_ref[3:4, :]
        for kk in range(D_MODEL // LANES):
            hh, jj = divmod(kk, SUBLANES)
            h2_ref[hh, pl.ds(sb * OUT_SUB * SUBLANES + jj, OUT_SUB, stride=SUBLANES), :] = (
                h2[:, kk * LANES:(kk + 1) * LANES])
        h_hi = h2.astype(BF16)
        h_lo = (h2 - h_hi.astype(F32)).astype(BF16)
        logits = (jnp.dot(h_hi, wrh_ref[...], preferred_element_type=F32)
                  + jnp.dot(h_lo, wrh_ref[...], preferred_element_type=F32)
                  + jnp.dot(h_hi, wrl_ref[...], preferred_element_type=F32))
        logits = logits.T[:N_EXPERTS, :]
        mx = jnp.max(logits, axis=0, keepdims=True)
        ex = jnp.exp(logits - mx)
        aff_ref[:, rs] = ex / jnp.sum(ex, axis=0, keepdims=True)


def _out_proj(a_conv, a_attn, x2, mod6, w_out_bf, b_out, g1, b1, wr_hi, wr_lo):
    n = x2.shape[0]
    row = lambda i: (i, 0)
    fixed = lambda i: (0, 0)
    return pl.pallas_call(
        _out_kernel,
        grid=(n // OUT_TM,),
        in_specs=[
            pl.BlockSpec((OUT_TM, CONV_WIDTH), row),
            pl.BlockSpec((OUT_TM, ATTN_WIDTH), row),
            pl.BlockSpec((OUT_TM, D_MODEL), row),
            pl.BlockSpec((N_MOD, D_MODEL), fixed),
            pl.BlockSpec((D_MODEL, D_MODEL), fixed),
            pl.BlockSpec((1, D_MODEL), fixed),
            pl.BlockSpec((1, D_MODEL), fixed),
            pl.BlockSpec((1, D_MODEL), fixed),
            pl.BlockSpec((D_MODEL, LANES), fixed),
            pl.BlockSpec((D_MODEL, LANES), fixed),
        ],
        out_specs=[
            pl.BlockSpec((OUT_TM, D_MODEL), row),
            pl.BlockSpec((H2_HALVES, OUT_TM * SUBLANES, LANES), lambda i: (0, i, 0)),
            pl.BlockSpec((N_EXPERTS, OUT_TM), lambda i: (0, i)),
        ],
        out_shape=[
            jax.ShapeDtypeStruct((n, D_MODEL), F32),
            jax.ShapeDtypeStruct((H2_HALVES, n * SUBLANES, LANES), F32),
            jax.ShapeDtypeStruct((N_EXPERTS, n), F32),
        ],
        compiler_params=_cparams(("arbitrary",)),
        name="out_proj",
    )(a_conv, a_attn, x2, mod6, w_out_bf, b_out, g1, b1, wr_hi, wr_lo)


def _split3(a):
    a1 = a.astype(BF16)
    r = a - a1.astype(F32)
    a2 = r.astype(BF16)
    a3 = (r - a2.astype(F32)).astype(BF16)
    return a1, a2, a3


def _select_kernel(aff_ref, idx_ref, g_ref, base_ref, pos_ref, sel_ref):
    e = pl.program_id(0)

    @pl.when(e == 0)
    def _():
        aff_all = aff_ref[...]

        def count(mask):
            c = jnp.sum(mask.astype(F32), axis=1, keepdims=True)
            return jnp.sum(c, axis=2, keepdims=True)

        def bit_step(k, thr_bits):
            cand = thr_bits | jnp.left_shift(jnp.int32(1), 30 - k)
            ge = aff_all >= pltpu.bitcast(cand, F32)
            return jnp.where(count(ge) >= CAP, cand, thr_bits)

        thr_bits = lax.fori_loop(0, 31, bit_step, jnp.zeros((N_EXPERTS, 1, 1), jnp.int32))
        thr = pltpu.bitcast(thr_bits, F32)
        gt = aff_all > thr
        eq = aff_all == thr
        need = CAP - count(gt)
        eq2 = eq.astype(F32).reshape(N_EXPERTS * N_CHUNKS, CHUNK)
        tri = (lax.broadcasted_iota(jnp.int32, (CHUNK, CHUNK), 0)
               <= lax.broadcasted_iota(jnp.int32, (CHUNK, CHUNK), 1)).astype(BF16)
        incl = jnp.dot(eq2.astype(BF16), tri, preferred_element_type=F32)
        tot = jnp.broadcast_to(incl[:, CHUNK - 1:CHUNK], (N_EXPERTS * N_CHUNKS, LANES))
        rr = lax.broadcasted_iota(jnp.int32, (N_EXPERTS * N_CHUNKS, N_EXPERTS * N_CHUNKS), 0)
        cc = lax.broadcasted_iota(jnp.int32, (N_EXPERTS * N_CHUNKS, N_EXPERTS * N_CHUNKS), 1)
        cshift = N_CHUNKS.bit_length() - 1
        low = ((jnp.right_shift(rr, cshift) == jnp.right_shift(cc, cshift)) & (cc < rr)).astype(BF16)
        before = jnp.dot(low, tot.astype(BF16), preferred_element_type=F32)
        rank = (before + incl - eq2).reshape(N_EXPERTS, N_CHUNKS, CHUNK)
        sel_ref[...] = (gt | (eq & (rank < need))).astype(F32)

    sel = sel_ref[e]
    aff = aff_ref[e]
    tri = (lax.broadcasted_iota(jnp.int32, (CHUNK, CHUNK), 0)
           <= lax.broadcasted_iota(jnp.int32, (CHUNK, CHUNK), 1)).astype(BF16)
    incl = jnp.dot(sel.astype(BF16), tri, preferred_element_type=F32)
    tot = jnp.broadcast_to(incl[:, CHUNK - 1:CHUNK], (N_CHUNKS, LANES))
    pad = lambda a: jnp.concatenate([a, jnp.zeros((LANES - N_CHUNKS, LANES), a.dtype)], axis=0)
    low = (lax.broadcasted_iota(jnp.int32, (N_CHUNKS, LANES), 1)
           < lax.broadcasted_iota(jnp.int32, (N_CHUNKS, LANES), 0)).astype(BF16)
    base = jnp.dot(low, pad(tot.astype(BF16)), preferred_element_type=F32)
    pos_ref[0] = jnp.where(sel > 0.0, base + incl - 1.0, -1.0).astype(jnp.int32)

    pick = (lax.broadcasted_iota(jnp.int32, (SUBLANES, LANES), 1) == 0).astype(BF16)
    nt = (((1,), (1,)), ((), ()))
    base_hi = jnp.floor(base * (1.0 / CHUNK))
    base_lo = base - base_hi * CHUNK
    row = lambda a: lax.dot_general(pick, pad(a.astype(BF16)), nt, preferred_element_type=F32)[0:1, :]
    base_row = row(base_hi) * CHUNK + row(base_lo)
    base_ref[0] = base_row.astype(jnp.int32)
    tot_row = row(tot)
    s_col = lax.broadcasted_iota(jnp.int32, (CAP, LANES), 0).astype(F32)
    owner = ((base_row <= s_col) & (s_col < base_row + tot_row)).astype(BF16)

    own = lambda a: jnp.dot(owner, pad(a.astype(BF16)), preferred_element_type=F32)
    incl_s = own(incl)
    base_s = own(base_hi) * CHUNK + own(base_lo)
    cvals = lax.broadcasted_iota(jnp.int32, (N_CHUNKS, LANES), 0).astype(F32)
    chunk_s = own(cvals)
    s_loc = lax.broadcasted_iota(jnp.int32, (CAP, LANES), 0).astype(F32) - base_s
    j_s = jnp.sum((incl_s <= s_loc).astype(F32), axis=1, keepdims=True)
    idx_ref[0] = (chunk_s[:, 0:1] * CHUNK + j_s).astype(jnp.int32)
    a1, a2, a3 = _split3(aff)
    aff_s = (own(a1) + own(a2)) + own(a3)
    lane = lax.broadcasted_iota(jnp.int32, (CAP, LANES), 1).astype(F32)
    g_ref[0] = jnp.sum(jnp.where(lane == j_s, aff_s, 0.0), axis=1, keepdims=True)


def _select(aff3):
    full = lambda e: (0, 0, 0)
    per = lambda e: (e, 0, 0)
    return pl.pallas_call(
        _select_kernel,
        grid=(N_EXPERTS,),
        in_specs=[pl.BlockSpec((N_EXPERTS, N_CHUNKS, CHUNK), full)],
        out_specs=[
            pl.BlockSpec((1, CAP, 1), per),
            pl.BlockSpec((1, CAP, 1), per),
            pl.BlockSpec((1, 1, LANES), per),
            pl.BlockSpec((1, N_CHUNKS, CHUNK), per),
        ],
        out_shape=[
            jax.ShapeDtypeStruct((N_EXPERTS, CAP, 1), jnp.int32),
            jax.ShapeDtypeStruct((N_EXPERTS, CAP, 1), F32),
            jax.ShapeDtypeStruct((N_EXPERTS, 1, LANES), jnp.int32),
            jax.ShapeDtypeStruct((N_EXPERTS, N_CHUNKS, CHUNK), jnp.int32),
        ],
        scratch_shapes=[pltpu.VMEM((N_EXPERTS, N_CHUNKS, CHUNK), F32)],
        compiler_params=_cparams(("arbitrary",)),
        name="select",
    )(aff3)


FFN_TF = 512
FFN_NF = D_FF // FFN_TF
FFN_TN = 256
FFN_GATHER_UNROLL = 8
FFN_VMEM_LIMIT = 60 * 1024 * 1024


def _ffn_kernel(idx_ref, h2_hbm, g_ref, wg_ref, wu_ref, wd_ref, y_ref, xs_ref, xb_ref, act_ref, acc_ref,
                sem):
    e = pl.program_id(0)
    f = pl.program_id(1)
    nf = pl.num_programs(1)

    def start_row(ee, s):
        t8 = pl.multiple_of(idx_ref[ee, s] * SUBLANES, SUBLANES)
        s8 = pl.multiple_of(s * SUBLANES, SUBLANES)
        pltpu.make_async_copy(h2_hbm.at[:, pl.ds(t8, SUBLANES), :],
                              xs_ref.at[:, pl.ds(s8, SUBLANES), :], sem).start()

    def wait_rows():
        pltpu.make_async_copy(h2_hbm.at[:, pl.ds(0, CAP * SUBLANES), :], xs_ref, sem).wait()

    @pl.when((e == 0) & (f == 0))
    def _():
        def body(s, carry):
            start_row(0, s)
            return carry
        lax.fori_loop(0, CAP, body, 0, unroll=FFN_GATHER_UNROLL)

    nxt = jnp.where(e + 1 < N_EXPERTS, e + 1, 0)
    per_step = CAP // FFN_NF

    @pl.when(f == 0)
    def _():
        wait_rows()
        for kk in range(D_MODEL // LANES):
            hh, jj = divmod(kk, SUBLANES)
            xb_ref[:, kk * LANES:(kk + 1) * LANES] = (
                xs_ref[hh, pl.ds(jj, CAP, stride=SUBLANES), :].astype(BF16))
        for s in range(per_step * FFN_NF, CAP):
            start_row(nxt, s)

    for j in range(per_step):
        start_row(nxt, f * per_step + j)

    for c in range(FFN_TF // FFN_TN):
        cs = slice(c * FFN_TN, (c + 1) * FFN_TN)
        a = jnp.dot(xb_ref[...], wg_ref[0, :, cs].astype(BF16), preferred_element_type=F32)
        u = jnp.dot(xb_ref[...], wu_ref[0, :, cs].astype(BF16), preferred_element_type=F32)
        act_ref[:, cs] = (a * jax.nn.sigmoid(a) * u).astype(BF16)
    for n in range(D_MODEL // FFN_TN):
        ns = slice(n * FFN_TN, (n + 1) * FFN_TN)
        part = jnp.dot(act_ref[...], wd_ref[0, :, ns].astype(BF16), preferred_element_type=F32)
        acc_ref[:, ns] = jnp.where(f == 0, 0.0, acc_ref[:, ns]) + part

    @pl.when(f == nf - 1)
    def _():
        y_ref[0] = (acc_ref[...] * g_ref[0]).astype(BF16)

    @pl.when((e == N_EXPERTS - 1) & (f == nf - 1))
    def _():
        wait_rows()


def _ffn(idx, h2, g, w_gate, w_up, w_down):
    return pl.pallas_call(
        _ffn_kernel,
        grid_spec=pltpu.PrefetchScalarGridSpec(
            num_scalar_prefetch=1,
            grid=(N_EXPERTS, D_FF // FFN_TF),
            in_specs=[
                pl.BlockSpec(memory_space=pl.ANY),
                pl.BlockSpec((1, CAP, 1), lambda e, f, idx: (e, 0, 0)),
                pl.BlockSpec((1, D_MODEL, FFN_TF), lambda e, f, idx: (e, 0, f)),
                pl.BlockSpec((1, D_MODEL, FFN_TF), lambda e, f, idx: (e, 0, f)),
                pl.BlockSpec((1, FFN_TF, D_MODEL), lambda e, f, idx: (e, f, 0)),
            ],
            out_specs=pl.BlockSpec((1, CAP, D_MODEL), lambda e, f, idx: (e, 0, 0)),
            scratch_shapes=[
                pltpu.VMEM((H2_HALVES, CAP * SUBLANES, LANES), F32),
                pltpu.VMEM((CAP, D_MODEL), BF16),
                pltpu.VMEM((CAP, FFN_TF), BF16),
                pltpu.VMEM((CAP, D_MODEL), F32),
                pltpu.SemaphoreType.DMA(()),
            ],
        ),
        out_shape=jax.ShapeDtypeStruct((N_EXPERTS, CAP, D_MODEL), BF16),
        compiler_params=_cparams(("arbitrary", "arbitrary"), FFN_VMEM_LIMIT),
        name="ffn",
    )(idx, h2, g, w_gate, w_up, w_down)


CMB_CPT = 2
CMB_T = CMB_CPT * CHUNK
CMB_K = 256
CMB_GRAN = 16
CMB_FIRST = 64
CMB_MAIN = N_EXPERTS * CMB_FIRST
CMB_SIZES = (128, 64, 32, 16)
CMB_OVF = N_EXPERTS * sum(CMB_SIZES)
assert CMB_MAIN % CMB_K == 0 and CMB_OVF % CMB_K == 0
assert CMB_T + CMB_GRAN <= CMB_FIRST + sum(CMB_SIZES)


def _combine_kernel(base_ref, pos_ref, xmid_ref, mod_ref, g2_ref, b2_ref, y_hbm, o_ref,
                    stage_ref, ovf_ref, spread_ref, rcol_ref, acc_ref, sem, ovf_sem):
    i = pl.program_id(0)
    slot = i & 1
    y_rows = N_EXPERTS * CAP

    def windows(tile):
        starts, extra = [], []
        for e in range(N_EXPERTS):
            s0 = e * CAP + base_ref[e * (N_CHUNKS + 1) + tile * CMB_CPT]
            s1 = e * CAP + base_ref[e * (N_CHUNKS + 1) + (tile + 1) * CMB_CPT]
            start = jnp.minimum(s0 & -CMB_GRAN, y_rows - CMB_FIRST)
            starts.append(start)
            extra.append(jnp.maximum(s1 - (start + CMB_FIRST) + CMB_GRAN - 1, 0) & -CMB_GRAN)
        return starts, extra

    def stage_main(tile, slot_):
        for e, start in enumerate(windows(tile)[0]):
            pltpu.make_async_copy(
                y_hbm.at[pl.ds(pl.multiple_of(start, CMB_GRAN), CMB_FIRST)],
                stage_ref.at[slot_, pl.ds(e * CMB_FIRST, CMB_FIRST)], sem.at[slot_]).start()

    @pl.when(i == 0)
    def _():
        stage_ref[...] = jnp.zeros_like(stage_ref)
        ovf_ref[...] = jnp.zeros_like(ovf_ref)
        ee = lax.broadcasted_iota(jnp.int32, (LANES, CMB_MAIN), 0)
        cc = lax.broadcasted_iota(jnp.int32, (LANES, CMB_MAIN), 1)
        lo = ee * CMB_FIRST
        spread_ref[...] = ((cc >= lo) & (cc < lo + CMB_FIRST)).astype(F32).astype(BF16)
        c1 = lax.broadcasted_iota(jnp.int32, (SUBLANES, CMB_MAIN), 1)
        owner = jnp.zeros((SUBLANES, CMB_MAIN), jnp.int32)
        for e in range(1, N_EXPERTS):
            owner = owner + (c1 >= e * CMB_FIRST).astype(jnp.int32)
        rcol_ref[...] = (c1 - owner * CMB_FIRST).astype(F32)
        stage_main(0, 0)

    @pl.when(i + 1 < pl.num_programs(0))
    def _():
        stage_main(i + 1, 1 - slot)

    starts, extra = windows(i)
    pltpu.make_async_copy(y_hbm.at[pl.ds(0, CMB_MAIN)], stage_ref.at[slot], sem.at[slot]).wait()

    lane = lax.broadcasted_iota(jnp.int32, (1, LANES), 1)
    shift = jnp.zeros((1, LANES), jnp.int32)
    for e in range(N_EXPERTS):
        shift = jnp.where(lane == e, e * CAP - starts[e], shift)
    pos = pos_ref[...]
    rel = jnp.where(pos >= 0, pos + shift, -1)
    spread = jnp.dot(rel.astype(F32).astype(BF16), spread_ref[...], preferred_element_type=F32)
    for k in range(CMB_MAIN // CMB_K):
        ks = slice(k * CMB_K, (k + 1) * CMB_K)
        w = (spread[:, ks] == rcol_ref[0:1, ks]).astype(F32).astype(BF16)
        part = jnp.dot(w, stage_ref[slot, ks, :], preferred_element_type=F32)
        if k == 0:
            acc_ref[...] = part
        else:
            acc_ref[...] += part

    n_extra = extra[0]
    for e in range(1, N_EXPERTS):
        n_extra = n_extra + extra[e]

    @pl.when(n_extra > 0)
    def _():
        copies, rowidx = [], []
        ooff = jnp.int32(0)
        for e in range(N_EXPERTS):
            done = jnp.int32(0)
            for size in CMB_SIZES:
                pred = (extra[e] & size) != 0
                src = pl.multiple_of(starts[e] + CMB_FIRST + done, CMB_GRAN)
                dst = pl.multiple_of(ooff + done, CMB_GRAN)
                copies.append((pred, pltpu.make_async_copy(
                    y_hbm.at[pl.ds(src, size)], ovf_ref.at[pl.ds(dst, size)], ovf_sem)))
                done = done + jnp.where(pred, size, 0)
            r = rel[:, e:e + 1]
            rowidx.append(jnp.where(r >= CMB_FIRST, r - CMB_FIRST + ooff, -1))
            ooff = ooff + extra[e]
        for pred, cp in copies:
            @pl.when(pred)
            def _():
                cp.start()
        for pred, cp in copies:
            @pl.when(pred)
            def _():
                cp.wait()

        def body(k, carry):
            r0 = pl.multiple_of(k * CMB_K, CMB_K)
            col = lax.broadcasted_iota(jnp.int32, (CMB_T, CMB_K), 1) + r0
            w = jnp.zeros((CMB_T, CMB_K), F32)
            for e in range(N_EXPERTS):
                w = w + (rowidx[e] == col).astype(F32)
            acc_ref[...] += jnp.dot(w.astype(BF16), ovf_ref[pl.ds(r0, CMB_K), :],
                                    preferred_element_type=F32)
            return carry

        lax.fori_loop(0, (n_extra + CMB_K - 1) // CMB_K, body, 0)

    gate2 = mod_ref[5:6, :]
    o_ref[...] = _ln(DEEPNORM_ALPHA * xmid_ref[...] + gate2 * acc_ref[...]) * g2_ref[...] + b2_ref[...]


def _combine(base_flat, pos_te, xmid, mod6, g2, b2, y_flat):
    n = xmid.shape[0]
    row = lambda i, b: (i, 0)
    fixed = lambda i, b: (0, 0)
    return pl.pallas_call(
        _combine_kernel,
        grid_spec=pltpu.PrefetchScalarGridSpec(
            num_scalar_prefetch=1,
            grid=(n // CMB_T,),
            in_specs=[
                pl.BlockSpec((CMB_T, LANES), row),
                pl.BlockSpec((CMB_T, D_MODEL), row),
                pl.BlockSpec((N_MOD, D_MODEL), fixed),
                pl.BlockSpec((1, D_MODEL), fixed),
                pl.BlockSpec((1, D_MODEL), fixed),
                pl.BlockSpec(memory_space=pl.ANY),
            ],
            out_specs=pl.BlockSpec((CMB_T, D_MODEL), row),
            scratch_shapes=[
                pltpu.VMEM((2, CMB_MAIN, D_MODEL), BF16),
                pltpu.VMEM((CMB_OVF, D_MODEL), BF16),
                pltpu.VMEM((LANES, CMB_MAIN), BF16),
                pltpu.VMEM((SUBLANES, CMB_MAIN), F32),
                pltpu.VMEM((CMB_T, D_MODEL), F32),
                pltpu.SemaphoreType.DMA((2,)),
                pltpu.SemaphoreType.DMA(()),
            ],
        ),
        out_shape=jax.ShapeDtypeStruct((n, D_MODEL), F32),
        compiler_params=_cparams(("arbitrary",)),
        name="combine",
    )(base_flat, pos_te, xmid, mod6, g2, b2, y_flat)


def _rope_tables(n):
    rows = n // GRID_W
    inv = ROPE_THETA ** (-np.arange(0, ROPE_AXIS_DIM, 2, dtype=np.float64) / ROPE_AXIS_DIM)
    ang_r = np.arange(rows, dtype=np.float64)[:, None] * inv[None, :]
    ang_c = np.arange(GRID_W, dtype=np.float64)[:, None] * inv[None, :]
    zr, zc = np.zeros_like(ang_r), np.zeros_like(ang_c)
    cr, sr, cc, sc = np.cos(ang_r), np.sin(ang_r), np.cos(ang_c), np.sin(ang_c)
    trow = np.stack([np.concatenate(p, axis=1) for p in
                     ([cr, cr, zr, zr], [-sr, zr, zr, zr], [zr, sr, zr, zr])])
    tcol = np.stack([np.concatenate(p, axis=1) for p in
                     ([zc, zc, cc, cc], [zc, zc, -sc, zc], [zc, zc, zc, sc])])
    return jnp.asarray(trow, F32), jnp.asarray(tcol, F32)


def kernel(x, c, ctx, c_ctx, w_mod, b_mod, w_in, b_in, w_dw, b_dw, conv_ln_g, conv_ln_b, sink,
           w_out, b_out, ln1_g, ln1_b, w_router, w_gate, w_up, w_down, ln2_g, ln2_b):
    assert x.shape == (1, SEQ, D_MODEL) and ctx.shape == (1, CTX_LEN, D_MODEL)
    assert w_mod.shape[0] == DEPTH
    x2 = x[0]
    ctx2 = ctx[0]
    r1 = lambda a: a.reshape(1, -1)

    ct = jnp.stack([c[0], c_ctx], axis=1)
    mod = _mod(ct, w_mod[0], r1(b_mod[0]))
    mod6 = mod[0].reshape(N_MOD, D_MODEL)
    modc6 = mod[1].reshape(N_MOD, D_MODEL)

    w_in_bf = w_in[0].astype(BF16)
    w_out_bf = w_out[0].astype(BF16)
    trow, tcol = _rope_tables(SEQ)
    u, q, k, v = _in_proj(x2, mod6, w_in_bf, r1(b_in[0]), trow, tcol)
    kx, vx = _ctx_kv(ctx2, modc6, w_in_bf, r1(b_in[0]))
    a_conv = _conv(u, w_dw[0], r1(b_dw[0]), r1(conv_ln_g[0]), r1(conv_ln_b[0]))
    a_attn = _attn(sink[0], q, k, v, kx, vx)
    wr = jnp.pad(w_router[0], ((0, 0), (0, LANES - N_EXPERTS)))
    wr_hi = wr.astype(BF16)
    wr_lo = (wr - wr_hi.astype(F32)).astype(BF16)
    xmid, h2, aff_t = _out_proj(a_conv, a_attn, x2, mod6, w_out_bf, r1(b_out[0]),
                                r1(ln1_g[0]), r1(ln1_b[0]), wr_hi, wr_lo)

    idx, g, base, pos = _select(aff_t.reshape(N_EXPERTS, N_CHUNKS, CHUNK))
    y = _ffn(idx.reshape(N_EXPERTS, CAP), h2, g, w_gate[0], w_up[0], w_down[0])

    base_flat = jnp.concatenate(
        [base[:, 0, :N_CHUNKS], jnp.full((N_EXPERTS, 1), CAP, jnp.int32)], axis=1).reshape(-1)
    pos_te = jnp.pad(pos.reshape(N_EXPERTS, SEQ).T, ((0, 0), (0, LANES - N_EXPERTS)),
                     constant_values=-1)
    out = _combine(base_flat, pos_te, xmid, mod6, r1(ln2_g[0]), r1(ln2_b[0]),
                   y.reshape(N_EXPERTS * CAP, D_MODEL))
    return out[None]
```

```python
import numpy as np

import jax
import jax.numpy as jnp
from jax import lax
from jax.experimental import pallas as pl
from jax.experimental.pallas import tpu as pltpu

D_MODEL = 2048
SEQ = 8192
GRID_W = 64
CTX_LEN = 256
HEAD_DIM = 128
N_Q_HEADS = 8
N_KV_HEADS = 2
Q_PER_KV = N_Q_HEADS // N_KV_HEADS
ATTN_WIDTH = N_Q_HEADS * HEAD_DIM
KV_WIDTH = N_KV_HEADS * HEAD_DIM
CONV_WIDTH = D_MODEL - ATTN_WIDTH
CONV_KSIZE = 31
WINDOW = 128
ROPE_THETA = 10000.0
ROPE_AXIS_DIM = HEAD_DIM // 2
N_EXPERTS = 16
EC_CAPACITY = 2
CAP = EC_CAPACITY * SEQ // N_EXPERTS
D_FF = 5632
N_MOD = 6
LN_EPS = 1e-5
NEG_INF = -1e30
DEPTH = 1
DEEPNORM_ALPHA = (2.0 * DEPTH) ** 0.25
Q_START = 2 * CONV_WIDTH
K_START = Q_START + ATTN_WIDTH
V_START = K_START + KV_WIDTH
IN_COLS = V_START + KV_WIDTH

LANES = 128
SUBLANES = 8
H2_HALVES = D_MODEL // (SUBLANES * LANES)
CHUNK = LANES
N_CHUNKS = SEQ // CHUNK
VMEM_LIMIT = 56 * 1024 * 1024
VMEM_LIMIT_BIG = 60 * 1024 * 1024

F32 = jnp.float32
BF16 = jnp.bfloat16


def _ln(xv):
    mu = jnp.mean(xv, axis=-1, keepdims=True)
    xc = xv - mu
    var = jnp.mean(xc * xc, axis=-1, keepdims=True)
    return xc * lax.rsqrt(var + LN_EPS)


def _cparams(sem, vmem=VMEM_LIMIT):
    return pltpu.CompilerParams(dimension_semantics=sem, vmem_limit_bytes=vmem)


MOD_TN = 1024
MOD_UNROLL = 4


def _mod_kernel(ct_ref, w_ref, b_ref, o_ref, s0_ref, s1_ref):
    @pl.when(pl.program_id(0) == 0)
    def _():
        ct = ct_ref[...]
        s = ct * jax.nn.sigmoid(ct)
        s0_ref[...] = jnp.broadcast_to(s[:, 0:1], (D_MODEL, LANES))
        s1_ref[...] = jnp.broadcast_to(s[:, 1:2], (D_MODEL, LANES))

    def body(kb, acc):
        k0 = pl.multiple_of(kb * SUBLANES, SUBLANES)
        s0 = s0_ref[pl.ds(k0, SUBLANES), :]
        s1 = s1_ref[pl.ds(k0, SUBLANES), :]
        a0, a1 = [], []
        for j in range(MOD_TN // LANES):
            w = w_ref[pl.ds(k0, SUBLANES), j * LANES:(j + 1) * LANES]
            a0.append(acc[0][j] + w * s0)
            a1.append(acc[1][j] + w * s1)
        return tuple(a0), tuple(a1)

    zeros = tuple(jnp.zeros((SUBLANES, LANES), F32) for _ in range(MOD_TN // LANES))
    acc0, acc1 = lax.fori_loop(0, D_MODEL // SUBLANES, body, (zeros, zeros), unroll=MOD_UNROLL)
    for j in range(MOD_TN // LANES):
        b = b_ref[:, j * LANES:(j + 1) * LANES]
        o_ref[0:1, j * LANES:(j + 1) * LANES] = jnp.sum(acc0[j], axis=0, keepdims=True) + b
        o_ref[1:2, j * LANES:(j + 1) * LANES] = jnp.sum(acc1[j], axis=0, keepdims=True) + b


def _mod(ct, w_mod, b_mod):
    n_out = N_MOD * D_MODEL
    return pl.pallas_call(
        _mod_kernel,
        grid=(n_out // MOD_TN,),
        in_specs=[
            pl.BlockSpec((D_MODEL, 2), lambda j: (0, 0)),
            pl.BlockSpec((D_MODEL, MOD_TN), lambda j: (0, j)),
            pl.BlockSpec((1, MOD_TN), lambda j: (0, j)),
        ],
        out_specs=pl.BlockSpec((2, MOD_TN), lambda j: (0, j)),
        out_shape=jax.ShapeDtypeStruct((2, n_out), F32),
        scratch_shapes=[pltpu.VMEM((D_MODEL, LANES), F32), pltpu.VMEM((D_MODEL, LANES), F32)],
        compiler_params=_cparams(("arbitrary",)),
        name="mod",
    )(ct, w_mod, b_mod)


IN_TM = 512
IN_SUB = 256
IN_TN = 512


def _rope(p, cos, sina, sinb):
    return (p * cos + pltpu.roll(p, HEAD_DIM - ROPE_AXIS_DIM // 2, axis=1) * sina
            + pltpu.roll(p, ROPE_AXIS_DIM // 2, axis=1) * sinb)


def _in_kernel(x_ref, mod_ref, w_ref, b_ref, trow_ref, tcol_ref, u_ref, q_ref, k_ref, v_ref):
    shift = mod_ref[0:1, :]
    scale = mod_ref[1:2, :]
    grid_rows = IN_SUB // GRID_W

    for sb in range(IN_TM // IN_SUB):
        rs = slice(sb * IN_SUB, (sb + 1) * IN_SUB)
        h = (_ln(x_ref[rs, :]) * (1.0 + scale) + shift).astype(BF16)

        def table(kind):
            by_row = jnp.concatenate(
                [jnp.broadcast_to(trow_ref[kind, sb * grid_rows + r:sb * grid_rows + r + 1, :],
                                  (GRID_W, HEAD_DIM)) for r in range(grid_rows)], axis=0)
            by_col = jnp.concatenate([tcol_ref[kind]] * grid_rows, axis=0)
            return by_row + by_col

        cos, sina, sinb = table(0), table(1), table(2)

        def proj(c0, width):
            return (jnp.dot(h, w_ref[:, c0:c0 + width], preferred_element_type=F32)
                    + b_ref[:, c0:c0 + width])

        for j in range(CONV_WIDTH // IN_TN):
            pv = proj(j * IN_TN, IN_TN)
            pg = proj(CONV_WIDTH + j * IN_TN, IN_TN)
            u_ref[rs, j * IN_TN:(j + 1) * IN_TN] = pv * jax.nn.sigmoid(pg)
        for j in range(ATTN_WIDTH // IN_TN):
            pq = proj(Q_START + j * IN_TN, IN_TN)
            for hh in range(IN_TN // HEAD_DIM):
                c0 = j * IN_TN + hh * HEAD_DIM
                q_ref[rs, c0:c0 + HEAD_DIM] = _rope(
                    pq[:, hh * HEAD_DIM:(hh + 1) * HEAD_DIM], cos, sina, sinb).astype(BF16)
        pk = proj(K_START, KV_WIDTH)
        for hh in range(N_KV_HEADS):
            k_ref[rs, hh * HEAD_DIM:(hh + 1) * HEAD_DIM] = _rope(
                pk[:, hh * HEAD_DIM:(hh + 1) * HEAD_DIM], cos, sina, sinb).astype(BF16)
        v_ref[rs, :] = proj(V_START, KV_WIDTH).astype(BF16)


def _in_proj(x2, mod6, w_in_bf, b_in, trow, tcol):
    n = x2.shape[0]
    row = lambda i: (i, 0)
    fixed = lambda i: (0, 0)
    return pl.pallas_call(
        _in_kernel,
        grid=(n // IN_TM,),
        in_specs=[
            pl.BlockSpec((IN_TM, D_MODEL), row),
            pl.BlockSpec((N_MOD, D_MODEL), fixed),
            pl.BlockSpec((D_MODEL, IN_COLS), fixed),
            pl.BlockSpec((1, IN_COLS), fixed),
            pl.BlockSpec((3, IN_TM // GRID_W, HEAD_DIM), lambda i: (0, i, 0)),
            pl.BlockSpec((3, GRID_W, HEAD_DIM), lambda i: (0, 0, 0)),
        ],
        out_specs=[
            pl.BlockSpec((IN_TM, CONV_WIDTH), row),
            pl.BlockSpec((IN_TM, ATTN_WIDTH), row),
            pl.BlockSpec((IN_TM, KV_WIDTH), row),
            pl.BlockSpec((IN_TM, KV_WIDTH), row),
        ],
        out_shape=[
            jax.ShapeDtypeStruct((n, CONV_WIDTH), F32),
            jax.ShapeDtypeStruct((n, ATTN_WIDTH), BF16),
            jax.ShapeDtypeStruct((n, KV_WIDTH), BF16),
            jax.ShapeDtypeStruct((n, KV_WIDTH), BF16),
        ],
        compiler_params=_cparams(("arbitrary",)),
        name="in_proj",
    )(x2, mod6, w_in_bf, b_in, trow, tcol)


def _ctx_kernel(x_ref, mod_ref, w_ref, b_ref, kc_ref, vc_ref):
    shift = mod_ref[0:1, :]
    scale = mod_ref[1:2, :]
    h = (_ln(x_ref[...]) * (1.0 + scale) + shift).astype(BF16)
    p = jnp.dot(h, w_ref[...], preferred_element_type=F32) + b_ref[...]
    kc_ref[...] = p[:, :KV_WIDTH].astype(BF16)
    vc_ref[...] = p[:, KV_WIDTH:].astype(BF16)


def _ctx_kv(ctx2, modc6, w_in_bf, b_in):
    kvw = 2 * KV_WIDTH
    fixed = lambda i: (0, 0)
    return pl.pallas_call(
        _ctx_kernel,
        grid=(1,),
        in_specs=[
            pl.BlockSpec((CTX_LEN, D_MODEL), fixed),
            pl.BlockSpec((N_MOD, D_MODEL), fixed),
            pl.BlockSpec((D_MODEL, kvw), lambda i: (0, K_START // kvw)),
            pl.BlockSpec((1, kvw), lambda i: (0, K_START // kvw)),
        ],
        out_specs=[pl.BlockSpec((CTX_LEN, KV_WIDTH), fixed), pl.BlockSpec((CTX_LEN, KV_WIDTH), fixed)],
        out_shape=[jax.ShapeDtypeStruct((CTX_LEN, KV_WIDTH), BF16),
                   jax.ShapeDtypeStruct((CTX_LEN, KV_WIDTH), BF16)],
        compiler_params=_cparams(("arbitrary",)),
        name="ctx_kv",
    )(ctx2, modc6, w_in_bf, b_in)


CONV_T = 512
CONV_HALO = 16
CONV_ROWS = 64


def _conv_fill(buf_ref, up_ref, uc_ref, un_ref, has_prev, has_next):
    for lg in range(CONV_WIDTH // LANES):
        ls = slice(lg * LANES, (lg + 1) * LANES)
        buf_ref[lg, 0:CONV_HALO, :] = jnp.where(has_prev, up_ref[:, ls], 0.0)
        buf_ref[lg, CONV_HALO:CONV_HALO + CONV_T, :] = uc_ref[:, ls]
        buf_ref[lg, CONV_HALO + CONV_T:, :] = jnp.where(has_next, un_ref[:, ls], 0.0)


def _conv_rows(buf_ref, w_ref, acc_ref, r0):
    off = CONV_HALO - CONV_KSIZE // 2
    span = CONV_ROWS + 2 * CONV_HALO
    for lg in range(CONV_WIDTH // LANES):
        ls = slice(lg * LANES, (lg + 1) * LANES)
        win = buf_ref.at[lg, pl.ds(r0, span)]
        acc = jnp.zeros((CONV_ROWS, LANES), F32)
        for t in range(CONV_KSIZE):
            acc = acc + win[pl.ds(off + t, CONV_ROWS), :] * w_ref[t:t + 1, ls]
        acc_ref[pl.ds(r0, CONV_ROWS), ls] = acc


def _conv_finish(acc_ref, bdw_ref, g_ref, b_ref):
    y = _ln(acc_ref[...] + bdw_ref[...]) * g_ref[...] + b_ref[...]
    return (y * jax.nn.sigmoid(y)).astype(BF16)


def _conv_kernel(up_ref, uc_ref, un_ref, w_ref, bdw_ref, g_ref, b_ref, o_ref, buf_ref, acc_ref):
    i = pl.program_id(0)
    last = pl.num_programs(0) - 1
    _conv_fill(buf_ref, up_ref, uc_ref, un_ref, i > 0, i < last)

    def body(r, carry):
        _conv_rows(buf_ref, w_ref, acc_ref, pl.multiple_of(r * CONV_ROWS, CONV_ROWS))
        return carry

    lax.fori_loop(0, CONV_T // CONV_ROWS, body, 0)
    o_ref[...] = _conv_finish(acc_ref, bdw_ref, g_ref, b_ref)


def _conv(u, w_dw, b_dw, ln_g, ln_b):
    n = u.shape[0]
    hb = CONV_T // CONV_HALO
    nhb = n // CONV_HALO
    fixed = lambda i: (0, 0)
    return pl.pallas_call(
        _conv_kernel,
        grid=(n // CONV_T,),
        in_specs=[
            pl.BlockSpec((CONV_HALO, CONV_WIDTH), lambda i: (jnp.maximum(i * hb - 1, 0), 0)),
            pl.BlockSpec((CONV_T, CONV_WIDTH), lambda i: (i, 0)),
            pl.BlockSpec((CONV_HALO, CONV_WIDTH), lambda i: (jnp.minimum((i + 1) * hb, nhb - 1), 0)),
            pl.BlockSpec((CONV_KSIZE, CONV_WIDTH), fixed),
            pl.BlockSpec((1, CONV_WIDTH), fixed),
            pl.BlockSpec((1, CONV_WIDTH), fixed),
            pl.BlockSpec((1, CONV_WIDTH), fixed),
        ],
        out_specs=pl.BlockSpec((CONV_T, CONV_WIDTH), lambda i: (i, 0)),
        out_shape=jax.ShapeDtypeStruct((n, CONV_WIDTH), BF16),
        scratch_shapes=[pltpu.VMEM((CONV_WIDTH // LANES, CONV_T + 2 * CONV_HALO, LANES), F32),
                        pltpu.VMEM((CONV_T, CONV_WIDTH), F32)],
        compiler_params=_cparams(("arbitrary",)),
        name="conv",
    )(u, u, u, w_dw, b_dw, ln_g, ln_b)


ATT_T = 128
ATT_NB = 2
assert ATT_T == WINDOW
LOG2E = 1.4426950408889634


def _attn_kernel(sink_ref, q_ref, kp_ref, kc_ref, kn_ref, vp_ref, vc_ref, vn_ref, kx_ref, vx_ref, o_ref,
                 bias_ref):
    i = pl.program_id(0)
    last = pl.num_programs(0) - 1
    scale = HEAD_DIM ** -0.5
    rows = Q_PER_KV * ATT_T

    @pl.when((i <= 1) | (i == last))
    def _():
        qi = lax.broadcasted_iota(jnp.int32, (rows, 3 * ATT_T), 0) & (ATT_T - 1)
        m = lax.broadcasted_iota(jnp.int32, (rows, 3 * ATT_T), 1)
        band = jnp.abs(m - qi - ATT_T) <= WINDOW
        for b in range(ATT_NB):
            kpos = (i * ATT_NB + b - 1) * ATT_T + m
            bias_ref[b] = jnp.where(band & (kpos >= 0) & (kpos < SEQ), 0.0, NEG_INF)

    hrow = jnp.right_shift(lax.broadcasted_iota(jnp.int32, (rows, 1), 0), ATT_T.bit_length() - 1)
    nt = (((1,), (1,)), ((), ()))
    def key_blocks(b, ls, p_ref, c_ref, n_ref):
        blocks = ([p_ref[:, ls]] + [c_ref[j * ATT_T:(j + 1) * ATT_T, ls] for j in range(ATT_NB)]
                  + [n_ref[:, ls]])
        return jnp.concatenate(blocks[b:b + 3], axis=0)

    def scores(b, g):
        rs = slice(b * ATT_T, (b + 1) * ATT_T)
        ls = slice(g * HEAD_DIM, (g + 1) * HEAD_DIM)
        qs = jnp.concatenate(
            [q_ref[rs, (g * Q_PER_KV + hh) * HEAD_DIM:(g * Q_PER_KV + hh + 1) * HEAD_DIM]
             for hh in range(Q_PER_KV)], axis=0)
        kw = key_blocks(b, ls, kp_ref, kc_ref, kn_ref)
        s_win = lax.dot_general(qs, kw, nt, preferred_element_type=F32) + bias_ref[b]
        s_ctx = lax.dot_general(qs, kx_ref[:, ls], nt, preferred_element_type=F32)
        return s_win, s_ctx

    def weights(g, s_win, s_ctx):
        s_sink = jnp.zeros((rows, 1), F32)
        for hh in range(Q_PER_KV):
            s_sink = jnp.where(hrow == hh, sink_ref[g * Q_PER_KV + hh], s_sink)
        raw_max = jnp.maximum(jnp.max(s_win, axis=-1, keepdims=True), jnp.max(s_ctx, axis=-1, keepdims=True))
        mx2 = jnp.maximum(raw_max * scale, s_sink) * LOG2E
        e_win = jnp.exp2(s_win * (scale * LOG2E) - mx2)
        e_ctx = jnp.exp2(s_ctx * (scale * LOG2E) - mx2)
        den = (jnp.sum(e_win, axis=-1, keepdims=True) + jnp.sum(e_ctx, axis=-1, keepdims=True)
               + jnp.exp2(s_sink * LOG2E - mx2))
        return e_win.astype(BF16), e_ctx.astype(BF16), den

    def values(b, g, e_win, e_ctx, den):
        rs = slice(b * ATT_T, (b + 1) * ATT_T)
        ls = slice(g * HEAD_DIM, (g + 1) * HEAD_DIM)
        vw = key_blocks(b, ls, vp_ref, vc_ref, vn_ref)
        o = (jnp.dot(e_ctx, vx_ref[:, ls], preferred_element_type=F32)
             + jnp.dot(e_win, vw, preferred_element_type=F32)) * (1.0 / den)
        for hh in range(Q_PER_KV):
            c0 = (g * Q_PER_KV + hh) * HEAD_DIM
            o_ref[rs, c0:c0 + HEAD_DIM] = o[hh * ATT_T:(hh + 1) * ATT_T, :].astype(BF16)

    chains = [(b, g) for b in range(ATT_NB) for g in range(N_KV_HEADS)]
    s_next = scores(*chains[0])
    for c, (b, g) in enumerate(chains):
        s_cur = s_next
        if c + 1 < len(chains):
            s_next = scores(*chains[c + 1])
        values(b, g, *weights(g, *s_cur))


def _attn(sink, q, k, v, kx, vx):
    n = q.shape[0]
    nb = n // ATT_T
    prev = lambda i, s: (jnp.maximum(i * ATT_NB - 1, 0), 0)
    cur = lambda i, s: (i, 0)
    nxt = lambda i, s: (jnp.minimum((i + 1) * ATT_NB, nb - 1), 0)
    fixed = lambda i, s: (0, 0)
    edge = lambda im: pl.BlockSpec((ATT_T, KV_WIDTH), im)
    own = pl.BlockSpec((ATT_NB * ATT_T, KV_WIDTH), cur)
    return pl.pallas_call(
        _attn_kernel,
        grid_spec=pltpu.PrefetchScalarGridSpec(
            num_scalar_prefetch=1,
            grid=(nb // ATT_NB,),
            in_specs=[
                pl.BlockSpec((ATT_NB * ATT_T, ATTN_WIDTH), cur),
                edge(prev), own, edge(nxt),
                edge(prev), own, edge(nxt),
                pl.BlockSpec((CTX_LEN, KV_WIDTH), fixed),
                pl.BlockSpec((CTX_LEN, KV_WIDTH), fixed),
            ],
            out_specs=pl.BlockSpec((ATT_NB * ATT_T, ATTN_WIDTH), cur),
            scratch_shapes=[pltpu.VMEM((ATT_NB, Q_PER_KV * ATT_T, 3 * ATT_T), F32)],
        ),
        out_shape=jax.ShapeDtypeStruct((n, ATTN_WIDTH), BF16),
        compiler_params=_cparams(("arbitrary",)),
        name="attn",
    )(sink, q, k, k, k, v, v, v, kx, vx)


OUT_TM = 512
OUT_SUB = 128


def _out_kernel(ac_ref, aa_ref, x_ref, mod_ref, w_ref, b_ref, g1_ref, b1_ref, wrh_ref, wrl_ref,
                xmid_ref, h2_ref, aff_ref):
    gate1 = mod_ref[2:3, :]
    n_sub = OUT_TM // OUT_SUB

    def mix_of(sb):
        rs = slice(sb * OUT_SUB, (sb + 1) * OUT_SUB)
        return (jnp.dot(ac_ref[rs, :], w_ref[:CONV_WIDTH, :], preferred_element_type=F32)
                + jnp.dot(aa_ref[rs, :], w_ref[CONV_WIDTH:, :], preferred_element_type=F32) + b_ref[...])

    mix_next = mix_of(0)
    for sb in range(n_sub):
        rs = slice(sb * OUT_SUB, (sb + 1) * OUT_SUB)
        mix = mix_next
        if sb + 1 < n_sub:
            mix_next = mix_of(sb + 1)
        xmid = _ln(DEEPNORM_ALPHA * x_ref[rs, :] + gate1 * mix) * g1_ref[...] + b1_ref[...]
        xmid_ref[rs, :] = xmid
        h2 = _ln(xmid) * (1.0 + mod_ref[4:5, :]) + mod_ref[3:4, :]
        for kk in range(D_MODEL // LANES):
            hh, jj = divmod(kk, SUBLANES)
            h2_ref[hh, pl.ds(sb * OUT_SUB * SUBLANES + jj, OUT_SUB, stride=SUBLANES), :] = (
                h2[:, kk * LANES:(kk + 1) * LANES])
        h_hi = h2.astype(BF16)
        h_lo = (h2 - h_hi.astype(F32)).astype(BF16)
        logits = (jnp.dot(h_hi, wrh_ref[...], preferred_element_type=F32)
                  + jnp.dot(h_lo, wrh_ref[...], preferred_element_type=F32)
                  + jnp.dot(h_hi, wrl_ref[...], preferred_element_type=F32))
        logits = logits.T[:N_EXPERTS, :]
        mx = jnp.max(logits, axis=0, keepdims=True)
        ex = jnp.exp(logits - mx)
        aff_ref[:, rs] = ex / jnp.sum(ex, axis=0, keepdims=True)


def _out_proj(a_conv, a_attn, x2, mod6, w_out_bf, b_out, g1, b1, wr_hi, wr_lo):
    n = x2.shape[0]
    row = lambda i: (i, 0)
    fixed = lambda i: (0, 0)
    return pl.pallas_call(
        _out_kernel,
        grid=(n // OUT_TM,),
        in_specs=[
            pl.BlockSpec((OUT_TM, CONV_WIDTH), row),
            pl.BlockSpec((OUT_TM, ATTN_WIDTH), row),
            pl.BlockSpec((OUT_TM, D_MODEL), row),
            pl.BlockSpec((N_MOD, D_MODEL), fixed),
            pl.BlockSpec((D_MODEL, D_MODEL), fixed),
            pl.BlockSpec((1, D_MODEL), fixed),
            pl.BlockSpec((1, D_MODEL), fixed),
            pl.BlockSpec((1, D_MODEL), fixed),
            pl.BlockSpec((D_MODEL, LANES), fixed),
            pl.BlockSpec((D_MODEL, LANES), fixed),
        ],
        out_specs=[
            pl.BlockSpec((OUT_TM, D_MODEL), row),
            pl.BlockSpec((H2_HALVES, OUT_TM * SUBLANES, LANES), lambda i: (0, i, 0)),
            pl.BlockSpec((N_EXPERTS, OUT_TM), lambda i: (0, i)),
        ],
        out_shape=[
            jax.ShapeDtypeStruct((n, D_MODEL), F32),
            jax.ShapeDtypeStruct((H2_HALVES, n * SUBLANES, LANES), F32),
            jax.ShapeDtypeStruct((N_EXPERTS, n), F32),
        ],
        compiler_params=_cparams(("arbitrary",)),
        name="out_proj",
    )(a_conv, a_attn, x2, mod6, w_out_bf, b_out, g1, b1, wr_hi, wr_lo)


def _split3(a):
    a1 = a.astype(BF16)
    r = a - a1.astype(F32)
    a2 = r.astype(BF16)
    a3 = (r - a2.astype(F32)).astype(BF16)
    return a1, a2, a3


def _select_kernel(aff_ref, idx_ref, g_ref, base_ref, pos_ref, sel_ref):
    e = pl.program_id(0)

    @pl.when(e == 0)
    def _():
        aff_all = aff_ref[...]

        def count(mask):
            c = jnp.sum(mask.astype(F32), axis=1, keepdims=True)
            return jnp.sum(c, axis=2, keepdims=True)

        def bit_step(k, thr_bits):
            cand = thr_bits | jnp.left_shift(jnp.int32(1), 30 - k)
            ge = aff_all >= pltpu.bitcast(cand, F32)
            return jnp.where(count(ge) >= CAP, cand, thr_bits)

        thr_bits = lax.fori_loop(0, 31, bit_step, jnp.zeros((N_EXPERTS, 1, 1), jnp.int32))
        thr = pltpu.bitcast(thr_bits, F32)
        gt = aff_all > thr
        eq = aff_all == thr
        need = CAP - count(gt)
        eq2 = eq.astype(F32).reshape(N_EXPERTS * N_CHUNKS, CHUNK)
        tri = (lax.broadcasted_iota(jnp.int32, (CHUNK, CHUNK), 0)
               <= lax.broadcasted_iota(jnp.int32, (CHUNK, CHUNK), 1)).astype(BF16)
        incl = jnp.dot(eq2.astype(BF16), tri, preferred_element_type=F32)
        tot = jnp.broadcast_to(incl[:, CHUNK - 1:CHUNK], (N_EXPERTS * N_CHUNKS, LANES))
        rr = lax.broadcasted_iota(jnp.int32, (N_EXPERTS * N_CHUNKS, N_EXPERTS * N_CHUNKS), 0)
        cc = lax.broadcasted_iota(jnp.int32, (N_EXPERTS * N_CHUNKS, N_EXPERTS * N_CHUNKS), 1)
        cshift = N_CHUNKS.bit_length() - 1
        low = ((jnp.right_shift(rr, cshift) == jnp.right_shift(cc, cshift)) & (cc < rr)).astype(BF16)
        before = jnp.dot(low, tot.astype(BF16), preferred_element_type=F32)
        rank = (before + incl - eq2).reshape(N_EXPERTS, N_CHUNKS, CHUNK)
        sel_ref[...] = (gt | (eq & (rank < need))).astype(F32)

    sel = sel_ref[e]
    aff = aff_ref[e]
    tri = (lax.broadcasted_iota(jnp.int32, (CHUNK, CHUNK), 0)
           <= lax.broadcasted_iota(jnp.int32, (CHUNK, CHUNK), 1)).astype(BF16)
    incl = jnp.dot(sel.astype(BF16), tri, preferred_element_type=F32)
    tot = jnp.broadcast_to(incl[:, CHUNK - 1:CHUNK], (N_CHUNKS, LANES))
    pad = lambda a: jnp.concatenate([a, jnp.zeros((LANES - N_CHUNKS, LANES), a.dtype)], axis=0)
    low = (lax.broadcasted_iota(jnp.int32, (N_CHUNKS, LANES), 1)
           < lax.broadcasted_iota(jnp.int32, (N_CHUNKS, LANES), 0)).astype(BF16)
    base = jnp.dot(low, pad(tot.astype(BF16)), preferred_element_type=F32)
    pos_ref[0] = jnp.where(sel > 0.0, base + incl - 1.0, -1.0).astype(jnp.int32)

    pick = (lax.broadcasted_iota(jnp.int32, (SUBLANES, LANES), 1) == 0).astype(BF16)
    nt = (((1,), (1,)), ((), ()))
    base_hi = jnp.floor(base * (1.0 / CHUNK))
    base_lo = base - base_hi * CHUNK
    row = lambda a: lax.dot_general(pick, pad(a.astype(BF16)), nt, preferred_element_type=F32)[0:1, :]
    base_row = row(base_hi) * CHUNK + row(base_lo)
    base_ref[0] = base_row.astype(jnp.int32)
    tot_row = row(tot)
    s_col = lax.broadcasted_iota(jnp.int32, (CAP, LANES), 0).astype(F32)
    owner = ((base_row <= s_col) & (s_col < base_row + tot_row)).astype(BF16)

    own = lambda a: jnp.dot(owner, pad(a.astype(BF16)), preferred_element_type=F32)
    incl_s = own(incl)
    base_s = own(base_hi) * CHUNK + own(base_lo)
    cvals = lax.broadcasted_iota(jnp.int32, (N_CHUNKS, LANES), 0).astype(F32)
    chunk_s = own(cvals)
    s_loc = lax.broadcasted_iota(jnp.int32, (CAP, LANES), 0).astype(F32) - base_s
    j_s = jnp.sum((incl_s <= s_loc).astype(F32), axis=1, keepdims=True)
    idx_ref[0] = (chunk_s[:, 0:1] * CHUNK + j_s).astype(jnp.int32)
    a1, a2, a3 = _split3(aff)
    aff_s = (own(a1) + own(a2)) + own(a3)
    lane = lax.broadcasted_iota(jnp.int32, (CAP, LANES), 1).astype(F32)
    g_ref[0] = jnp.sum(jnp.where(lane == j_s, aff_s, 0.0), axis=1, keepdims=True)


def _select(aff3):
    full = lambda e: (0, 0, 0)
    per = lambda e: (e, 0, 0)
    return pl.pallas_call(
        _select_kernel,
        grid=(N_EXPERTS,),
        in_specs=[pl.BlockSpec((N_EXPERTS, N_CHUNKS, CHUNK), full)],
        out_specs=[
            pl.BlockSpec((1, CAP, 1), per),
            pl.BlockSpec((1, CAP, 1), per),
            pl.BlockSpec((1, 1, LANES), per),
            pl.BlockSpec((1, N_CHUNKS, CHUNK), per),
        ],
        out_shape=[
            jax.ShapeDtypeStruct((N_EXPERTS, CAP, 1), jnp.int32),
            jax.ShapeDtypeStruct((N_EXPERTS, CAP, 1), F32),
            jax.ShapeDtypeStruct((N_EXPERTS, 1, LANES), jnp.int32),
            jax.ShapeDtypeStruct((N_EXPERTS, N_CHUNKS, CHUNK), jnp.int32),
        ],
        scratch_shapes=[pltpu.VMEM((N_EXPERTS, N_CHUNKS, CHUNK), F32)],
        compiler_params=_cparams(("arbitrary",)),
        name="select",
    )(aff3)


FFN_TF = 512
FFN_NF = D_FF // FFN_TF
FFN_TN = 256
FFN_GATHER_UNROLL = 8


def _ffn_kernel(idx_ref, h2_hbm, g_ref, wg_ref, wu_ref, wd_ref, y_ref, xs_ref, xb_ref, act_ref, acc_ref,
                sem):
    e = pl.program_id(0)
    f = pl.program_id(1)
    nf = pl.num_programs(1)

    def start_row(ee, s):
        t8 = pl.multiple_of(idx_ref[ee, s] * SUBLANES, SUBLANES)
        s8 = pl.multiple_of(s * SUBLANES, SUBLANES)
        pltpu.make_async_copy(h2_hbm.at[:, pl.ds(t8, SUBLANES), :],
                              xs_ref.at[:, pl.ds(s8, SUBLANES), :], sem).start()

    def wait_rows():
        pltpu.make_async_copy(h2_hbm.at[:, pl.ds(0, CAP * SUBLANES), :], xs_ref, sem).wait()

    @pl.when((e == 0) & (f == 0))
    def _():
        def body(s, carry):
            start_row(0, s)
            return carry
        lax.fori_loop(0, CAP, body, 0, unroll=FFN_GATHER_UNROLL)

    nxt = jnp.where(e + 1 < N_EXPERTS, e + 1, 0)
    per_step = CAP // FFN_NF

    @pl.when(f == 0)
    def _():
        wait_rows()
        for kk in range(D_MODEL // LANES):
            hh, jj = divmod(kk, SUBLANES)
            xb_ref[:, kk * LANES:(kk + 1) * LANES] = (
                xs_ref[hh, pl.ds(jj, CAP, stride=SUBLANES), :].astype(BF16))
        for s in range(per_step * FFN_NF, CAP):
            start_row(nxt, s)

    for j in range(per_step):
        start_row(nxt, f * per_step + j)

    for c in range(FFN_TF // FFN_TN):
        cs = slice(c * FFN_TN, (c + 1) * FFN_TN)
        a = jnp.dot(xb_ref[...], wg_ref[0, :, cs].astype(BF16), preferred_element_type=F32)
        u = jnp.dot(xb_ref[...], wu_ref[0, :, cs].astype(BF16), preferred_element_type=F32)
        act_ref[:, cs] = (a * jax.nn.sigmoid(a) * u).astype(BF16)
    for n in range(D_MODEL // FFN_TN):
        ns = slice(n * FFN_TN, (n + 1) * FFN_TN)
        part = jnp.dot(act_ref[...], wd_ref[0, :, ns].astype(BF16), preferred_element_type=F32)
        acc_ref[:, ns] = jnp.where(f == 0, 0.0, acc_ref[:, ns]) + part

    @pl.when(f == nf - 1)
    def _():
        y_ref[0] = (acc_ref[...] * g_ref[0]).astype(BF16)

    @pl.when((e == N_EXPERTS - 1) & (f == nf - 1))
    def _():
        wait_rows()


def _ffn(idx, h2, g, w_gate, w_up, w_down):
    return pl.pallas_call(
        _ffn_kernel,
        grid_spec=pltpu.PrefetchScalarGridSpec(
            num_scalar_prefetch=1,
            grid=(N_EXPERTS, D_FF // FFN_TF),
            in_specs=[
                pl.BlockSpec(memory_space=pl.ANY),
                pl.BlockSpec((1, CAP, 1), lambda e, f, idx: (e, 0, 0)),
                pl.BlockSpec((1, D_MODEL, FFN_TF), lambda e, f, idx: (e, 0, f)),
                pl.BlockSpec((1, D_MODEL, FFN_TF), lambda e, f, idx: (e, 0, f)),
                pl.BlockSpec((1, FFN_TF, D_MODEL), lambda e, f, idx: (e, f, 0)),
            ],
            out_specs=pl.BlockSpec((1, CAP, D_MODEL), lambda e, f, idx: (e, 0, 0)),
            scratch_shapes=[
                pltpu.VMEM((H2_HALVES, CAP * SUBLANES, LANES), F32),
                pltpu.VMEM((CAP, D_MODEL), BF16),
                pltpu.VMEM((CAP, FFN_TF), BF16),
                pltpu.VMEM((CAP, D_MODEL), F32),
                pltpu.SemaphoreType.DMA(()),
            ],
        ),
        out_shape=jax.ShapeDtypeStruct((N_EXPERTS, CAP, D_MODEL), BF16),
        compiler_params=_cparams(("arbitrary", "arbitrary"), VMEM_LIMIT_BIG),
        name="ffn",
    )(idx, h2, g, w_gate, w_up, w_down)


CMB_CPT = 2
CMB_T = CMB_CPT * CHUNK
CMB_K = 256
CMB_GRAN = 16
CMB_FIRST = 64
CMB_MAIN = N_EXPERTS * CMB_FIRST
CMB_SIZES = (128, 64, 32, 16)
CMB_OVF = N_EXPERTS * sum(CMB_SIZES)
assert CMB_MAIN % CMB_K == 0 and CMB_OVF % CMB_K == 0
assert CMB_T + CMB_GRAN <= CMB_FIRST + sum(CMB_SIZES)


def _combine_kernel(base_ref, pos_ref, xmid_ref, mod_ref, g2_ref, b2_ref, y_hbm, o_ref,
                    stage_ref, ovf_ref, spread_ref, rcol_ref, acc_ref, sem, ovf_sem):
    i = pl.program_id(0)
    slot = i & 1
    y_rows = N_EXPERTS * CAP

    def windows(tile):
        starts, extra = [], []
        for e in range(N_EXPERTS):
            s0 = e * CAP + base_ref[e * (N_CHUNKS + 1) + tile * CMB_CPT]
            s1 = e * CAP + base_ref[e * (N_CHUNKS + 1) + (tile + 1) * CMB_CPT]
            start = jnp.minimum(s0 & -CMB_GRAN, y_rows - CMB_FIRST)
            starts.append(start)
            extra.append(jnp.maximum(s1 - (start + CMB_FIRST) + CMB_GRAN - 1, 0) & -CMB_GRAN)
        return starts, extra

    def stage_main(tile, slot_):
        for e, start in enumerate(windows(tile)[0]):
            pltpu.make_async_copy(
                y_hbm.at[pl.ds(pl.multiple_of(start, CMB_GRAN), CMB_FIRST)],
                stage_ref.at[slot_, pl.ds(e * CMB_FIRST, CMB_FIRST)], sem.at[slot_]).start()

    @pl.when(i == 0)
    def _():
        stage_ref[...] = jnp.zeros_like(stage_ref)
        ovf_ref[...] = jnp.zeros_like(ovf_ref)
        ee = lax.broadcasted_iota(jnp.int32, (LANES, CMB_MAIN), 0)
        cc = lax.broadcasted_iota(jnp.int32, (LANES, CMB_MAIN), 1)
        lo = ee * CMB_FIRST
        spread_ref[...] = ((cc >= lo) & (cc < lo + CMB_FIRST)).astype(F32).astype(BF16)
        c1 = lax.broadcasted_iota(jnp.int32, (SUBLANES, CMB_MAIN), 1)
        owner = jnp.zeros((SUBLANES, CMB_MAIN), jnp.int32)
        for e in range(1, N_EXPERTS):
            owner = owner + (c1 >= e * CMB_FIRST).astype(jnp.int32)
        rcol_ref[...] = (c1 - owner * CMB_FIRST).astype(F32)
        stage_main(0, 0)

    @pl.when(i + 1 < pl.num_programs(0))
    def _():
        stage_main(i + 1, 1 - slot)

    starts, extra = windows(i)
    pltpu.make_async_copy(y_hbm.at[pl.ds(0, CMB_MAIN)], stage_ref.at[slot], sem.at[slot]).wait()

    lane = lax.broadcasted_iota(jnp.int32, (1, LANES), 1)
    shift = jnp.zeros((1, LANES), jnp.int32)
    for e in range(N_EXPERTS):
        shift = jnp.where(lane == e, e * CAP - starts[e], shift)
    pos = pos_ref[...]
    rel = jnp.where(pos >= 0, pos + shift, -1)
    spread = jnp.dot(rel.astype(F32).astype(BF16), spread_ref[...], preferred_element_type=F32)
    for k in range(CMB_MAIN // CMB_K):
        ks = slice(k * CMB_K, (k + 1) * CMB_K)
        w = (spread[:, ks] == rcol_ref[0:1, ks]).astype(F32).astype(BF16)
        part = jnp.dot(w, stage_ref[slot, ks, :], preferred_element_type=F32)
        if k == 0:
            acc_ref[...] = part
        else:
            acc_ref[...] += part

    n_extra = extra[0]
    for e in range(1, N_EXPERTS):
        n_extra = n_extra + extra[e]

    @pl.when(n_extra > 0)
    def _():
        copies, rowidx = [], []
        ooff = jnp.int32(0)
        for e in range(N_EXPERTS):
            done = jnp.int32(0)
            for size in CMB_SIZES:
                pred = (extra[e] & size) != 0
                src = pl.multiple_of(starts[e] + CMB_FIRST + done, CMB_GRAN)
                dst = pl.multiple_of(ooff + done, CMB_GRAN)
                copies.append((pred, pltpu.make_async_copy(
                    y_hbm.at[pl.ds(src, size)], ovf_ref.at[pl.ds(dst, size)], ovf_sem)))
                done = done + jnp.where(pred, size, 0)
            r = rel[:, e:e + 1]
            rowidx.append(jnp.where(r >= CMB_FIRST, r - CMB_FIRST + ooff, -1))
            ooff = ooff + extra[e]
        for pred, cp in copies:
            @pl.when(pred)
            def _():
                cp.start()
        for pred, cp in copies:
            @pl.when(pred)
            def _():
                cp.wait()

        def body(k, carry):
            r0 = pl.multiple_of(k * CMB_K, CMB_K)
            col = lax.broadcasted_iota(jnp.int32, (CMB_T, CMB_K), 1) + r0
            w = jnp.zeros((CMB_T, CMB_K), F32)
            for e in range(N_EXPERTS):
                w = w + (rowidx[e] == col).astype(F32)
            acc_ref[...] += jnp.dot(w.astype(BF16), ovf_ref[pl.ds(r0, CMB_K), :],
                                    preferred_element_type=F32)
            return carry

        lax.fori_loop(0, (n_extra + CMB_K - 1) // CMB_K, body, 0)

    gate2 = mod_ref[5:6, :]
    o_ref[...] = _ln(DEEPNORM_ALPHA * xmid_ref[...] + gate2 * acc_ref[...]) * g2_ref[...] + b2_ref[...]


def _combine(base_flat, pos_te, xmid, mod6, g2, b2, y_flat):
    n = xmid.shape[0]
    row = lambda i, b: (i, 0)
    fixed = lambda i, b: (0, 0)
    return pl.pallas_call(
        _combine_kernel,
        grid_spec=pltpu.PrefetchScalarGridSpec(
            num_scalar_prefetch=1,
            grid=(n // CMB_T,),
            in_specs=[
                pl.BlockSpec((CMB_T, LANES), row),
                pl.BlockSpec((CMB_T, D_MODEL), row),
                pl.BlockSpec((N_MOD, D_MODEL), fixed),
                pl.BlockSpec((1, D_MODEL), fixed),
                pl.BlockSpec((1, D_MODEL), fixed),
                pl.BlockSpec(memory_space=pl.ANY),
            ],
            out_specs=pl.BlockSpec((CMB_T, D_MODEL), row),
            scratch_shapes=[
                pltpu.VMEM((2, CMB_MAIN, D_MODEL), BF16),
                pltpu.VMEM((CMB_OVF, D_MODEL), BF16),
                pltpu.VMEM((LANES, CMB_MAIN), BF16),
                pltpu.VMEM((SUBLANES, CMB_MAIN), F32),
                pltpu.VMEM((CMB_T, D_MODEL), F32),
                pltpu.SemaphoreType.DMA((2,)),
                pltpu.SemaphoreType.DMA(()),
            ],
        ),
        out_shape=jax.ShapeDtypeStruct((n, D_MODEL), F32),
        compiler_params=_cparams(("arbitrary",)),
        name="combine",
    )(base_flat, pos_te, xmid, mod6, g2, b2, y_flat)


def _rope_tables(n):
    rows = n // GRID_W
    inv = ROPE_THETA ** (-np.arange(0, ROPE_AXIS_DIM, 2, dtype=np.float64) / ROPE_AXIS_DIM)
    ang_r = np.arange(rows, dtype=np.float64)[:, None] * inv[None, :]
    ang_c = np.arange(GRID_W, dtype=np.float64)[:, None] * inv[None, :]
    zr, zc = np.zeros_like(ang_r), np.zeros_like(ang_c)
    cr, sr, cc, sc = np.cos(ang_r), np.sin(ang_r), np.cos(ang_c), np.sin(ang_c)
    trow = np.stack([np.concatenate(p, axis=1) for p in
                     ([cr, cr, zr, zr], [-sr, zr, zr, zr], [zr, sr, zr, zr])])
    tcol = np.stack([np.concatenate(p, axis=1) for p in
                     ([zc, zc, cc, cc], [zc, zc, -sc, zc], [zc, zc, zc, sc])])
    return jnp.asarray(trow, F32), jnp.asarray(tcol, F32)


def kernel(x, c, ctx, c_ctx, w_mod, b_mod, w_in, b_in, w_dw, b_dw, conv_ln_g, conv_ln_b, sink,
           w_out, b_out, ln1_g, ln1_b, w_router, w_gate, w_up, w_down, ln2_g, ln2_b):
    assert x.shape == (1, SEQ, D_MODEL) and ctx.shape == (1, CTX_LEN, D_MODEL)
    assert w_mod.shape[0] == DEPTH
    x2 = x[0]
    ctx2 = ctx[0]
    r1 = lambda a: a.reshape(1, -1)

    ct = jnp.stack([c[0], c_ctx], axis=1)
    mod = _mod(ct, w_mod[0], r1(b_mod[0]))
    mod6 = mod[0].reshape(N_MOD, D_MODEL)
    modc6 = mod[1].reshape(N_MOD, D_MODEL)

    w_in_bf = w_in[0].astype(BF16)
    w_out_bf = w_out[0].astype(BF16)
    trow, tcol = _rope_tables(SEQ)
    u, q, k, v = _in_proj(x2, mod6, w_in_bf, r1(b_in[0]), trow, tcol)
    kx, vx = _ctx_kv(ctx2, modc6, w_in_bf, r1(b_in[0]))
    a_attn = _attn(sink[0], q, k, v, kx, vx)
    wr = jnp.pad(w_router[0], ((0, 0), (0, LANES - N_EXPERTS)))
    wr_top = lax.reduce_precision(wr, exponent_bits=8, mantissa_bits=7)
    wr_hi = wr_top.astype(BF16)
    wr_lo = (wr - wr_top).astype(BF16)
    a_conv = _conv(u, w_dw[0], r1(b_dw[0]), r1(conv_ln_g[0]), r1(conv_ln_b[0]))
    xmid, h2, aff_t = _out_proj(a_conv, a_attn, x2, mod6, w_out_bf, r1(b_out[0]),
                                r1(ln1_g[0]), r1(ln1_b[0]), wr_hi, wr_lo)

    idx, g, base, pos = _select(aff_t.reshape(N_EXPERTS, N_CHUNKS, CHUNK))
    y = _ffn(idx.reshape(N_EXPERTS, CAP), h2, g, w_gate[0], w_up[0], w_down[0])

    base_flat = jnp.concatenate(
        [base[:, 0, :N_CHUNKS], jnp.full((N_EXPERTS, 1), CAP, jnp.int32)], axis=1).reshape(-1)
    pos_te = jnp.pad(pos.reshape(N_EXPERTS, SEQ).T, ((0, 0), (0, LANES - N_EXPERTS)),
                     constant_values=-1)
    out = _combine(base_flat, pos_te, xmid, mod6, r1(ln2_g[0]), r1(ln2_b[0]),
                   y.reshape(N_EXPERTS * CAP, D_MODEL))
    return out[None]
```

```python
import numpy as np

import jax
import jax.numpy as jnp
from jax import lax
from jax.experimental import pallas as pl
from jax.experimental.pallas import tpu as pltpu

D_MODEL = 2048
SEQ = 8192
GRID_W = 64
CTX_LEN = 256
HEAD_DIM = 128
N_Q_HEADS = 8
N_KV_HEADS = 2
Q_PER_KV = N_Q_HEADS // N_KV_HEADS
ATTN_WIDTH = N_Q_HEADS * HEAD_DIM
KV_WIDTH = N_KV_HEADS * HEAD_DIM
CONV_WIDTH = D_MODEL - ATTN_WIDTH
CONV_KSIZE = 31
WINDOW = 128
ROPE_THETA = 10000.0
ROPE_AXIS_DIM = HEAD_DIM // 2
N_EXPERTS = 16
EC_CAPACITY = 2
CAP = EC_CAPACITY * SEQ // N_EXPERTS
D_FF = 5632
N_MOD = 6
LN_EPS = 1e-5
NEG_INF = -1e30
DEPTH = 1
DEEPNORM_ALPHA = (2.0 * DEPTH) ** 0.25
Q_START = 2 * CONV_WIDTH
K_START = Q_START + ATTN_WIDTH
V_START = K_START + KV_WIDTH
IN_COLS = V_START + KV_WIDTH

LANES = 128
SUBLANES = 8
H2_HALVES = D_MODEL // (SUBLANES * LANES)
CHUNK = LANES
N_CHUNKS = SEQ // CHUNK
VMEM_LIMIT = 56 * 1024 * 1024
VMEM_LIMIT_BIG = 60 * 1024 * 1024

F32 = jnp.float32
BF16 = jnp.bfloat16


def _ln(xv):
    mu = jnp.mean(xv, axis=-1, keepdims=True)
    xc = xv - mu
    var = jnp.mean(xc * xc, axis=-1, keepdims=True)
    return xc * lax.rsqrt(var + LN_EPS)


def _cparams(sem, vmem=VMEM_LIMIT):
    return pltpu.CompilerParams(dimension_semantics=sem, vmem_limit_bytes=vmem)


MOD_TN = 1024
MOD_UNROLL = 4


def _mod_kernel(ct_ref, w_ref, b_ref, o_ref, s0_ref, s1_ref):
    @pl.when(pl.program_id(0) == 0)
    def _():
        ct = ct_ref[...]
        s = ct * jax.nn.sigmoid(ct)
        s0_ref[...] = jnp.broadcast_to(s[:, 0:1], (D_MODEL, LANES))
        s1_ref[...] = jnp.broadcast_to(s[:, 1:2], (D_MODEL, LANES))

    def body(kb, acc):
        k0 = pl.multiple_of(kb * SUBLANES, SUBLANES)
        s0 = s0_ref[pl.ds(k0, SUBLANES), :]
        s1 = s1_ref[pl.ds(k0, SUBLANES), :]
        a0, a1 = [], []
        for j in range(MOD_TN // LANES):
            w = w_ref[pl.ds(k0, SUBLANES), j * LANES:(j + 1) * LANES]
            a0.append(acc[0][j] + w * s0)
            a1.append(acc[1][j] + w * s1)
        return tuple(a0), tuple(a1)

    zeros = tuple(jnp.zeros((SUBLANES, LANES), F32) for _ in range(MOD_TN // LANES))
    acc0, acc1 = lax.fori_loop(0, D_MODEL // SUBLANES, body, (zeros, zeros), unroll=MOD_UNROLL)
    for j in range(MOD_TN // LANES):
        b = b_ref[:, j * LANES:(j + 1) * LANES]
        o_ref[0:1, j * LANES:(j + 1) * LANES] = jnp.sum(acc0[j], axis=0, keepdims=True) + b
        o_ref[1:2, j * LANES:(j + 1) * LANES] = jnp.sum(acc1[j], axis=0, keepdims=True) + b


def _mod(ct, w_mod, b_mod):
    n_out = N_MOD * D_MODEL
    return pl.pallas_call(
        _mod_kernel,
        grid=(n_out // MOD_TN,),
        in_specs=[
            pl.BlockSpec((D_MODEL, 2), lambda j: (0, 0)),
            pl.BlockSpec((D_MODEL, MOD_TN), lambda j: (0, j)),
            pl.BlockSpec((1, MOD_TN), lambda j: (0, j)),
        ],
        out_specs=pl.BlockSpec((2, MOD_TN), lambda j: (0, j)),
        out_shape=jax.ShapeDtypeStruct((2, n_out), F32),
        scratch_shapes=[pltpu.VMEM((D_MODEL, LANES), F32), pltpu.VMEM((D_MODEL, LANES), F32)],
        compiler_params=_cparams(("arbitrary",)),
        name="mod",
    )(ct, w_mod, b_mod)


IN_TM = 512
IN_SUB = 256
IN_TN = 512


def _rope(p, cos, sina, sinb):
    return (p * cos + pltpu.roll(p, HEAD_DIM - ROPE_AXIS_DIM // 2, axis=1) * sina
            + pltpu.roll(p, ROPE_AXIS_DIM // 2, axis=1) * sinb)


def _in_kernel(x_ref, mod_ref, w_ref, b_ref, trow_ref, tcol_ref, u_ref, q_ref, k_ref, v_ref):
    shift = mod_ref[0:1, :]
    scale = mod_ref[1:2, :]
    grid_rows = IN_SUB // GRID_W

    for sb in range(IN_TM // IN_SUB):
        rs = slice(sb * IN_SUB, (sb + 1) * IN_SUB)
        h = (_ln(x_ref[rs, :]) * (1.0 + scale) + shift).astype(BF16)

        def table(kind):
            by_row = jnp.concatenate(
                [jnp.broadcast_to(trow_ref[kind, sb * grid_rows + r:sb * grid_rows + r + 1, :],
                                  (GRID_W, HEAD_DIM)) for r in range(grid_rows)], axis=0)
            by_col = jnp.concatenate([tcol_ref[kind]] * grid_rows, axis=0)
            return by_row + by_col

        cos, sina, sinb = table(0), table(1), table(2)

        def proj(c0, width):
            return (jnp.dot(h, w_ref[:, c0:c0 + width], preferred_element_type=F32)
                    + b_ref[:, c0:c0 + width])

        for j in range(CONV_WIDTH // IN_TN):
            pv = proj(j * IN_TN, IN_TN)
            pg = proj(CONV_WIDTH + j * IN_TN, IN_TN)
            u_ref[rs, j * IN_TN:(j + 1) * IN_TN] = pv * jax.nn.sigmoid(pg)
        for j in range(ATTN_WIDTH // IN_TN):
            pq = proj(Q_START + j * IN_TN, IN_TN)
            for hh in range(IN_TN // HEAD_DIM):
                c0 = j * IN_TN + hh * HEAD_DIM
                q_ref[rs, c0:c0 + HEAD_DIM] = _rope(
                    pq[:, hh * HEAD_DIM:(hh + 1) * HEAD_DIM], cos, sina, sinb).astype(BF16)
        pk = proj(K_START, KV_WIDTH)
        for hh in range(N_KV_HEADS):
            k_ref[rs, hh * HEAD_DIM:(hh + 1) * HEAD_DIM] = _rope(
                pk[:, hh * HEAD_DIM:(hh + 1) * HEAD_DIM], cos, sina, sinb).astype(BF16)
        v_ref[rs, :] = proj(V_START, KV_WIDTH).astype(BF16)


def _in_proj(x2, mod6, w_in_bf, b_in, trow, tcol):
    n = x2.shape[0]
    row = lambda i: (i, 0)
    fixed = lambda i: (0, 0)
    return pl.pallas_call(
        _in_kernel,
        grid=(n // IN_TM,),
        in_specs=[
            pl.BlockSpec((IN_TM, D_MODEL), row),
            pl.BlockSpec((N_MOD, D_MODEL), fixed),
            pl.BlockSpec((D_MODEL, IN_COLS), fixed),
            pl.BlockSpec((1, IN_COLS), fixed),
            pl.BlockSpec((3, IN_TM // GRID_W, HEAD_DIM), lambda i: (0, i, 0)),
            pl.BlockSpec((3, GRID_W, HEAD_DIM), lambda i: (0, 0, 0)),
        ],
        out_specs=[
            pl.BlockSpec((IN_TM, CONV_WIDTH), row),
            pl.BlockSpec((IN_TM, ATTN_WIDTH), row),
            pl.BlockSpec((IN_TM, KV_WIDTH), row),
            pl.BlockSpec((IN_TM, KV_WIDTH), row),
        ],
        out_shape=[
            jax.ShapeDtypeStruct((n, CONV_WIDTH), F32),
            jax.ShapeDtypeStruct((n, ATTN_WIDTH), BF16),
            jax.ShapeDtypeStruct((n, KV_WIDTH), BF16),
            jax.ShapeDtypeStruct((n, KV_WIDTH), BF16),
        ],
        compiler_params=_cparams(("arbitrary",)),
        name="in_proj",
    )(x2, mod6, w_in_bf, b_in, trow, tcol)


def _ctx_kernel(x_ref, mod_ref, w_ref, b_ref, kc_ref, vc_ref):
    shift = mod_ref[0:1, :]
    scale = mod_ref[1:2, :]
    h = (_ln(x_ref[...]) * (1.0 + scale) + shift).astype(BF16)
    p = jnp.dot(h, w_ref[...], preferred_element_type=F32) + b_ref[...]
    kc_ref[...] = p[:, :KV_WIDTH].astype(BF16)
    vc_ref[...] = p[:, KV_WIDTH:].astype(BF16)


def _ctx_kv(ctx2, modc6, w_in_bf, b_in):
    kvw = 2 * KV_WIDTH
    fixed = lambda i: (0, 0)
    return pl.pallas_call(
        _ctx_kernel,
        grid=(1,),
        in_specs=[
            pl.BlockSpec((CTX_LEN, D_MODEL), fixed),
            pl.BlockSpec((N_MOD, D_MODEL), fixed),
            pl.BlockSpec((D_MODEL, kvw), lambda i: (0, K_START // kvw)),
            pl.BlockSpec((1, kvw), lambda i: (0, K_START // kvw)),
        ],
        out_specs=[pl.BlockSpec((CTX_LEN, KV_WIDTH), fixed), pl.BlockSpec((CTX_LEN, KV_WIDTH), fixed)],
        out_shape=[jax.ShapeDtypeStruct((CTX_LEN, KV_WIDTH), BF16),
                   jax.ShapeDtypeStruct((CTX_LEN, KV_WIDTH), BF16)],
        compiler_params=_cparams(("arbitrary",)),
        name="ctx_kv",
    )(ctx2, modc6, w_in_bf, b_in)


CONV_T = 512
CONV_HALO = 16
CONV_ROWS = 64


def _conv_fill(buf_ref, up_ref, uc_ref, un_ref, has_prev, has_next):
    for lg in range(CONV_WIDTH // LANES):
        ls = slice(lg * LANES, (lg + 1) * LANES)
        buf_ref[lg, 0:CONV_HALO, :] = jnp.where(has_prev, up_ref[:, ls], 0.0)
        buf_ref[lg, CONV_HALO:CONV_HALO + CONV_T, :] = uc_ref[:, ls]
        buf_ref[lg, CONV_HALO + CONV_T:, :] = jnp.where(has_next, un_ref[:, ls], 0.0)


def _conv_rows(buf_ref, w_ref, acc_ref, r0):
    off = CONV_HALO - CONV_KSIZE // 2
    span = CONV_ROWS + 2 * CONV_HALO
    for lg in range(CONV_WIDTH // LANES):
        ls = slice(lg * LANES, (lg + 1) * LANES)
        win = buf_ref.at[lg, pl.ds(r0, span)]
        acc = jnp.zeros((CONV_ROWS, LANES), F32)
        for t in range(CONV_KSIZE):
            acc = acc + win[pl.ds(off + t, CONV_ROWS), :] * w_ref[t:t + 1, ls]
        acc_ref[pl.ds(r0, CONV_ROWS), ls] = acc


def _conv_finish(acc_ref, bdw_ref, g_ref, b_ref):
    y = _ln(acc_ref[...] + bdw_ref[...]) * g_ref[...] + b_ref[...]
    return (y * jax.nn.sigmoid(y)).astype(BF16)


def _conv_kernel(up_ref, uc_ref, un_ref, w_ref, bdw_ref, g_ref, b_ref, o_ref, buf_ref, acc_ref):
    i = pl.program_id(0)
    last = pl.num_programs(0) - 1
    _conv_fill(buf_ref, up_ref, uc_ref, un_ref, i > 0, i < last)

    def body(r, carry):
        _conv_rows(buf_ref, w_ref, acc_ref, pl.multiple_of(r * CONV_ROWS, CONV_ROWS))
        return carry

    lax.fori_loop(0, CONV_T // CONV_ROWS, body, 0)
    o_ref[...] = _conv_finish(acc_ref, bdw_ref, g_ref, b_ref)


def _conv(u, w_dw, b_dw, ln_g, ln_b):
    n = u.shape[0]
    hb = CONV_T // CONV_HALO
    nhb = n // CONV_HALO
    fixed = lambda i: (0, 0)
    return pl.pallas_call(
        _conv_kernel,
        grid=(n // CONV_T,),
        in_specs=[
            pl.BlockSpec((CONV_HALO, CONV_WIDTH), lambda i: (jnp.maximum(i * hb - 1, 0), 0)),
            pl.BlockSpec((CONV_T, CONV_WIDTH), lambda i: (i, 0)),
            pl.BlockSpec((CONV_HALO, CONV_WIDTH), lambda i: (jnp.minimum((i + 1) * hb, nhb - 1), 0)),
            pl.BlockSpec((CONV_KSIZE, CONV_WIDTH), fixed),
            pl.BlockSpec((1, CONV_WIDTH), fixed),
            pl.BlockSpec((1, CONV_WIDTH), fixed),
            pl.BlockSpec((1, CONV_WIDTH), fixed),
        ],
        out_specs=pl.BlockSpec((CONV_T, CONV_WIDTH), lambda i: (i, 0)),
        out_shape=jax.ShapeDtypeStruct((n, CONV_WIDTH), BF16),
        scratch_shapes=[pltpu.VMEM((CONV_WIDTH // LANES, CONV_T + 2 * CONV_HALO, LANES), F32),
                        pltpu.VMEM((CONV_T, CONV_WIDTH), F32)],
        compiler_params=_cparams(("arbitrary",)),
        name="conv",
    )(u, u, u, w_dw, b_dw, ln_g, ln_b)


ATT_T = 128
ATT_NB = 2
assert ATT_T == WINDOW
LOG2E = 1.4426950408889634


def _attn_kernel(sink_ref, q_ref, kp_ref, kc_ref, kn_ref, vp_ref, vc_ref, vn_ref, kx_ref, vx_ref, o_ref,
                 bias_ref):
    i = pl.program_id(0)
    last = pl.num_programs(0) - 1
    scale = HEAD_DIM ** -0.5
    rows = Q_PER_KV * ATT_T

    @pl.when((i <= 1) | (i == last))
    def _():
        qi = lax.broadcasted_iota(jnp.int32, (rows, 3 * ATT_T), 0) & (ATT_T - 1)
        m = lax.broadcasted_iota(jnp.int32, (rows, 3 * ATT_T), 1)
        band = jnp.abs(m - qi - ATT_T) <= WINDOW
        for b in range(ATT_NB):
            kpos = (i * ATT_NB + b - 1) * ATT_T + m
            bias_ref[b] = jnp.where(band & (kpos >= 0) & (kpos < SEQ), 0.0, NEG_INF)

    hrow = jnp.right_shift(lax.broadcasted_iota(jnp.int32, (rows, 1), 0), ATT_T.bit_length() - 1)
    nt = (((1,), (1,)), ((), ()))
    def key_blocks(b, ls, p_ref, c_ref, n_ref):
        blocks = ([p_ref[:, ls]] + [c_ref[j * ATT_T:(j + 1) * ATT_T, ls] for j in range(ATT_NB)]
                  + [n_ref[:, ls]])
        return jnp.concatenate(blocks[b:b + 3], axis=0)

    def scores(b, g):
        rs = slice(b * ATT_T, (b + 1) * ATT_T)
        ls = slice(g * HEAD_DIM, (g + 1) * HEAD_DIM)
        qs = jnp.concatenate(
            [q_ref[rs, (g * Q_PER_KV + hh) * HEAD_DIM:(g * Q_PER_KV + hh + 1) * HEAD_DIM]
             for hh in range(Q_PER_KV)], axis=0)
        kw = key_blocks(b, ls, kp_ref, kc_ref, kn_ref)
        s_win = lax.dot_general(qs, kw, nt, preferred_element_type=F32) + bias_ref[b]
        s_ctx = lax.dot_general(qs, kx_ref[:, ls], nt, preferred_element_type=F32)
        return s_win, s_ctx

    def weights(g, s_win, s_ctx):
        s_sink = jnp.zeros((rows, 1), F32)
        for hh in range(Q_PER_KV):
            s_sink = jnp.where(hrow == hh, sink_ref[g * Q_PER_KV + hh], s_sink)
        raw_max = jnp.maximum(jnp.max(s_win, axis=-1, keepdims=True), jnp.max(s_ctx, axis=-1, keepdims=True))
        mx2 = jnp.maximum(raw_max * scale, s_sink) * LOG2E
        e_win = jnp.exp2(s_win * (scale * LOG2E) - mx2)
        e_ctx = jnp.exp2(s_ctx * (scale * LOG2E) - mx2)
        den = (jnp.sum(e_win, axis=-1, keepdims=True) + jnp.sum(e_ctx, axis=-1, keepdims=True)
               + jnp.exp2(s_sink * LOG2E - mx2))
        return e_win.astype(BF16), e_ctx.astype(BF16), den

    def values(b, g, e_win, e_ctx, den):
        rs = slice(b * ATT_T, (b + 1) * ATT_T)
        ls = slice(g * HEAD_DIM, (g + 1) * HEAD_DIM)
        vw = key_blocks(b, ls, vp_ref, vc_ref, vn_ref)
        o = (jnp.dot(e_ctx, vx_ref[:, ls], preferred_element_type=F32)
             + jnp.dot(e_win, vw, preferred_element_type=F32)) * (1.0 / den)
        for hh in range(Q_PER_KV):
            c0 = (g * Q_PER_KV + hh) * HEAD_DIM
            o_ref[rs, c0:c0 + HEAD_DIM] = o[hh * ATT_T:(hh + 1) * ATT_T, :].astype(BF16)

    chains = [(b, g) for b in range(ATT_NB) for g in range(N_KV_HEADS)]
    s_next = scores(*chains[0])
    for c, (b, g) in enumerate(chains):
        s_cur = s_next
        if c + 1 < len(chains):
            s_next = scores(*chains[c + 1])
        values(b, g, *weights(g, *s_cur))


def _attn(sink, q, k, v, kx, vx):
    n = q.shape[0]
    nb = n // ATT_T
    prev = lambda i, s: (jnp.maximum(i * ATT_NB - 1, 0), 0)
    cur = lambda i, s: (i, 0)
    nxt = lambda i, s: (jnp.minimum((i + 1) * ATT_NB, nb - 1), 0)
    fixed = lambda i, s: (0, 0)
    edge = lambda im: pl.BlockSpec((ATT_T, KV_WIDTH), im)
    own = pl.BlockSpec((ATT_NB * ATT_T, KV_WIDTH), cur)
    return pl.pallas_call(
        _attn_kernel,
        grid_spec=pltpu.PrefetchScalarGridSpec(
            num_scalar_prefetch=1,
            grid=(nb // ATT_NB,),
            in_specs=[
                pl.BlockSpec((ATT_NB * ATT_T, ATTN_WIDTH), cur),
                edge(prev), own, edge(nxt),
                edge(prev), own, edge(nxt),
                pl.BlockSpec((CTX_LEN, KV_WIDTH), fixed),
                pl.BlockSpec((CTX_LEN, KV_WIDTH), fixed),
            ],
            out_specs=pl.BlockSpec((ATT_NB * ATT_T, ATTN_WIDTH), cur),
            scratch_shapes=[pltpu.VMEM((ATT_NB, Q_PER_KV * ATT_T, 3 * ATT_T), F32)],
        ),
        out_shape=jax.ShapeDtypeStruct((n, ATTN_WIDTH), BF16),
        compiler_params=_cparams(("arbitrary",)),
        name="attn",
    )(sink, q, k, k, k, v, v, v, kx, vx)


OUT_TM = 512
OUT_SUB = 128


def _out_kernel(ac_ref, aa_ref, x_ref, mod_ref, w_ref, b_ref, g1_ref, b1_ref, wr_ref,
                xmid_ref, h2_ref, aff_ref, wrh_ref, wrl_ref):
    gate1 = mod_ref[2:3, :]
    n_sub = OUT_TM // OUT_SUB

    @pl.when(pl.program_id(0) == 0)
    def _():
        wr = wr_ref[...]
        hi = wr.astype(BF16)
        wrh_ref[...] = hi
        wrl_ref[...] = (wr - hi.astype(F32)).astype(BF16)

    def mix_of(sb):
        rs = slice(sb * OUT_SUB, (sb + 1) * OUT_SUB)
        return (jnp.dot(ac_ref[rs, :], w_ref[:CONV_WIDTH, :], preferred_element_type=F32)
                + jnp.dot(aa_ref[rs, :], w_ref[CONV_WIDTH:, :], preferred_element_type=F32) + b_ref[...])

    mix_next = mix_of(0)
    for sb in range(n_sub):
        rs = slice(sb * OUT_SUB, (sb + 1) * OUT_SUB)
        mix = mix_next
        if sb + 1 < n_sub:
            mix_next = mix_of(sb + 1)
        xmid = _ln(DEEPNORM_ALPHA * x_ref[rs, :] + gate1 * mix) * g1_ref[...] + b1_ref[...]
        xmid_ref[rs, :] = xmid
        h2 = _ln(xmid) * (1.0 + mod_ref[4:5, :]) + mod_ref[3:4, :]
        for kk in range(D_MODEL // LANES):
            hh, jj = divmod(kk, SUBLANES)
            h2_ref[hh, pl.ds(sb * OUT_SUB * SUBLANES + jj, OUT_SUB, stride=SUBLANES), :] = (
                h2[:, kk * LANES:(kk + 1) * LANES])
        h_hi = h2.astype(BF16)
        h_lo = (h2 - h_hi.astype(F32)).astype(BF16)
        logits = (jnp.dot(h_hi, wrh_ref[...], preferred_element_type=F32)
                  + jnp.dot(h_lo, wrh_ref[...], preferred_element_type=F32)
                  + jnp.dot(h_hi, wrl_ref[...], preferred_element_type=F32))
        logits = logits.T[:N_EXPERTS, :]
        mx = jnp.max(logits, axis=0, keepdims=True)
        ex = jnp.exp(logits - mx)
        aff_ref[:, rs] = ex / jnp.sum(ex, axis=0, keepdims=True)


def _out_proj(a_conv, a_attn, x2, mod6, w_out_bf, b_out, g1, b1, wr):
    n = x2.shape[0]
    row = lambda i: (i, 0)
    fixed = lambda i: (0, 0)
    return pl.pallas_call(
        _out_kernel,
        grid=(n // OUT_TM,),
        in_specs=[
            pl.BlockSpec((OUT_TM, CONV_WIDTH), row),
            pl.BlockSpec((OUT_TM, ATTN_WIDTH), row),
            pl.BlockSpec((OUT_TM, D_MODEL), row),
            pl.BlockSpec((N_MOD, D_MODEL), fixed),
            pl.BlockSpec((D_MODEL, D_MODEL), fixed),
            pl.BlockSpec((1, D_MODEL), fixed),
            pl.BlockSpec((1, D_MODEL), fixed),
            pl.BlockSpec((1, D_MODEL), fixed),
            pl.BlockSpec((D_MODEL, LANES), fixed),
        ],
        out_specs=[
            pl.BlockSpec((OUT_TM, D_MODEL), row),
            pl.BlockSpec((H2_HALVES, OUT_TM * SUBLANES, LANES), lambda i: (0, i, 0)),
            pl.BlockSpec((N_EXPERTS, OUT_TM), lambda i: (0, i)),
        ],
        out_shape=[
            jax.ShapeDtypeStruct((n, D_MODEL), F32),
            jax.ShapeDtypeStruct((H2_HALVES, n * SUBLANES, LANES), F32),
            jax.ShapeDtypeStruct((N_EXPERTS, n), F32),
        ],
        scratch_shapes=[pltpu.VMEM((D_MODEL, LANES), BF16), pltpu.VMEM((D_MODEL, LANES), BF16)],
        compiler_params=_cparams(("arbitrary",)),
        name="out_proj",
    )(a_conv, a_attn, x2, mod6, w_out_bf, b_out, g1, b1, wr)


def _split3(a):
    a1 = a.astype(BF16)
    r = a - a1.astype(F32)
    a2 = r.astype(BF16)
    a3 = (r - a2.astype(F32)).astype(BF16)
    return a1, a2, a3


def _select_kernel(aff_ref, idx_ref, g_ref, base_ref, pos_ref, sel_ref):
    e = pl.program_id(0)

    @pl.when(e == 0)
    def _():
        aff_all = aff_ref[...]

        def count(mask):
            c = jnp.sum(mask.astype(F32), axis=1, keepdims=True)
            return jnp.sum(c, axis=2, keepdims=True)

        def bit_step(k, thr_bits):
            cand = thr_bits | jnp.left_shift(jnp.int32(1), 30 - k)
            ge = aff_all >= pltpu.bitcast(cand, F32)
            return jnp.where(count(ge) >= CAP, cand, thr_bits)

        thr_bits = lax.fori_loop(0, 31, bit_step, jnp.zeros((N_EXPERTS, 1, 1), jnp.int32))
        thr = pltpu.bitcast(thr_bits, F32)
        gt = aff_all > thr
        eq = aff_all == thr
        need = CAP - count(gt)
        eq2 = eq.astype(F32).reshape(N_EXPERTS * N_CHUNKS, CHUNK)
        tri = (lax.broadcasted_iota(jnp.int32, (CHUNK, CHUNK), 0)
               <= lax.broadcasted_iota(jnp.int32, (CHUNK, CHUNK), 1)).astype(BF16)
        incl = jnp.dot(eq2.astype(BF16), tri, preferred_element_type=F32)
        tot = jnp.broadcast_to(incl[:, CHUNK - 1:CHUNK], (N_EXPERTS * N_CHUNKS, LANES))
        rr = lax.broadcasted_iota(jnp.int32, (N_EXPERTS * N_CHUNKS, N_EXPERTS * N_CHUNKS), 0)
        cc = lax.broadcasted_iota(jnp.int32, (N_EXPERTS * N_CHUNKS, N_EXPERTS * N_CHUNKS), 1)
        cshift = N_CHUNKS.bit_length() - 1
        low = ((jnp.right_shift(rr, cshift) == jnp.right_shift(cc, cshift)) & (cc < rr)).astype(BF16)
        before = jnp.dot(low, tot.astype(BF16), preferred_element_type=F32)
        rank = (before + incl - eq2).reshape(N_EXPERTS, N_CHUNKS, CHUNK)
        sel_ref[...] = (gt | (eq & (rank < need))).astype(F32)

    sel = sel_ref[e]
    aff = aff_ref[e]
    tri = (lax.broadcasted_iota(jnp.int32, (CHUNK, CHUNK), 0)
           <= lax.broadcasted_iota(jnp.int32, (CHUNK, CHUNK), 1)).astype(BF16)
    incl = jnp.dot(sel.astype(BF16), tri, preferred_element_type=F32)
    tot = jnp.broadcast_to(incl[:, CHUNK - 1:CHUNK], (N_CHUNKS, LANES))
    pad = lambda a: jnp.concatenate([a, jnp.zeros((LANES - N_CHUNKS, LANES), a.dtype)], axis=0)
    low = (lax.broadcasted_iota(jnp.int32, (N_CHUNKS, LANES), 1)
           < lax.broadcasted_iota(jnp.int32, (N_CHUNKS, LANES), 0)).astype(BF16)
    base = jnp.dot(low, pad(tot.astype(BF16)), preferred_element_type=F32)
    pos_ref[0] = jnp.where(sel > 0.0, base + incl - 1.0, -1.0).astype(jnp.int32)

    pick = (lax.broadcasted_iota(jnp.int32, (SUBLANES, LANES), 1) == 0).astype(BF16)
    nt = (((1,), (1,)), ((), ()))
    base_hi = jnp.floor(base * (1.0 / CHUNK))
    base_lo = base - base_hi * CHUNK
    row = lambda a: lax.dot_general(pick, pad(a.astype(BF16)), nt, preferred_element_type=F32)[0:1, :]
    base_row = row(base_hi) * CHUNK + row(base_lo)
    base_ref[0] = base_row.astype(jnp.int32)
    tot_row = row(tot)
    s_col = lax.broadcasted_iota(jnp.int32, (CAP, LANES), 0).astype(F32)
    owner = ((base_row <= s_col) & (s_col < base_row + tot_row)).astype(BF16)

    own = lambda a: jnp.dot(owner, pad(a.astype(BF16)), preferred_element_type=F32)
    incl_s = own(incl)
    base_s = own(base_hi) * CHUNK + own(base_lo)
    cvals = lax.broadcasted_iota(jnp.int32, (N_CHUNKS, LANES), 0).astype(F32)
    chunk_s = own(cvals)
    s_loc = lax.broadcasted_iota(jnp.int32, (CAP, LANES), 0).astype(F32) - base_s
    j_s = jnp.sum((incl_s <= s_loc).astype(F32), axis=1, keepdims=True)
    idx_ref[0] = (chunk_s[:, 0:1] * CHUNK + j_s).astype(jnp.int32)
    a1, a2, a3 = _split3(aff)
    aff_s = (own(a1) + own(a2)) + own(a3)
    lane = lax.broadcasted_iota(jnp.int32, (CAP, LANES), 1).astype(F32)
    g_ref[0] = jnp.sum(jnp.where(lane == j_s, aff_s, 0.0), axis=1, keepdims=True)


def _select(aff3):
    full = lambda e: (0, 0, 0)
    per = lambda e: (e, 0, 0)
    return pl.pallas_call(
        _select_kernel,
        grid=(N_EXPERTS,),
        in_specs=[pl.BlockSpec((N_EXPERTS, N_CHUNKS, CHUNK), full)],
        out_specs=[
            pl.BlockSpec((1, CAP, 1), per),
            pl.BlockSpec((1, CAP, 1), per),
            pl.BlockSpec((1, 1, LANES), per),
            pl.BlockSpec((1, N_CHUNKS, CHUNK), per),
        ],
        out_shape=[
            jax.ShapeDtypeStruct((N_EXPERTS, CAP, 1), jnp.int32),
            jax.ShapeDtypeStruct((N_EXPERTS, CAP, 1), F32),
            jax.ShapeDtypeStruct((N_EXPERTS, 1, LANES), jnp.int32),
            jax.ShapeDtypeStruct((N_EXPERTS, N_CHUNKS, CHUNK), jnp.int32),
        ],
        scratch_shapes=[pltpu.VMEM((N_EXPERTS, N_CHUNKS, CHUNK), F32)],
        compiler_params=_cparams(("arbitrary",)),
        name="select",
    )(aff3)


FFN_TF = 512
FFN_NF = D_FF // FFN_TF
FFN_TN = 256
FFN_GATHER_UNROLL = 8


def _ffn_kernel(idx_ref, h2_hbm, g_ref, wg_ref, wu_ref, wd_ref, y_ref, xs_ref, xb_ref, act_ref, acc_ref,
                sem):
    e = pl.program_id(0)
    f = pl.program_id(1)
    nf = pl.num_programs(1)

    def start_row(ee, s):
        t8 = pl.multiple_of(idx_ref[ee, s] * SUBLANES, SUBLANES)
        s8 = pl.multiple_of(s * SUBLANES, SUBLANES)
        pltpu.make_async_copy(h2_hbm.at[:, pl.ds(t8, SUBLANES), :],
                              xs_ref.at[:, pl.ds(s8, SUBLANES), :], sem).start()

    def wait_rows():
        pltpu.make_async_copy(h2_hbm.at[:, pl.ds(0, CAP * SUBLANES), :], xs_ref, sem).wait()

    @pl.when((e == 0) & (f == 0))
    def _():
        def body(s, carry):
            start_row(0, s)
            return carry
        lax.fori_loop(0, CAP, body, 0, unroll=FFN_GATHER_UNROLL)

    nxt = jnp.where(e + 1 < N_EXPERTS, e + 1, 0)
    per_step = CAP // FFN_NF

    @pl.when(f == 0)
    def _():
        wait_rows()
        for kk in range(D_MODEL // LANES):
            hh, jj = divmod(kk, SUBLANES)
            xb_ref[:, kk * LANES:(kk + 1) * LANES] = (
                xs_ref[hh, pl.ds(jj, CAP, stride=SUBLANES), :].astype(BF16))
        for s in range(per_step * FFN_NF, CAP):
            start_row(nxt, s)

    for j in range(per_step):
        start_row(nxt, f * per_step + j)

    for c in range(FFN_TF // FFN_TN):
        cs = slice(c * FFN_TN, (c + 1) * FFN_TN)
        a = jnp.dot(xb_ref[...], wg_ref[0, :, cs].astype(BF16), preferred_element_type=F32)
        u = jnp.dot(xb_ref[...], wu_ref[0, :, cs].astype(BF16), preferred_element_type=F32)
        act_ref[:, cs] = (a * jax.nn.sigmoid(a) * u).astype(BF16)
    for n in range(D_MODEL // FFN_TN):
        ns = slice(n * FFN_TN, (n + 1) * FFN_TN)
        part = jnp.dot(act_ref[...], wd_ref[0, :, ns].astype(BF16), preferred_element_type=F32)
        acc_ref[:, ns] = jnp.where(f == 0, 0.0, acc_ref[:, ns]) + part

    @pl.when(f == nf - 1)
    def _():
        y_ref[0] = (acc_ref[...] * g_ref[0]).astype(BF16)

    @pl.when((e == N_EXPERTS - 1) & (f == nf - 1))
    def _():
        wait_rows()


def _ffn(idx, h2, g, w_gate, w_up, w_down):
    return pl.pallas_call(
        _ffn_kernel,
        grid_spec=pltpu.PrefetchScalarGridSpec(
            num_scalar_prefetch=1,
            grid=(N_EXPERTS, D_FF // FFN_TF),
            in_specs=[
                pl.BlockSpec(memory_space=pl.ANY),
                pl.BlockSpec((1, CAP, 1), lambda e, f, idx: (e, 0, 0)),
                pl.BlockSpec((1, D_MODEL, FFN_TF), lambda e, f, idx: (e, 0, f)),
                pl.BlockSpec((1, D_MODEL, FFN_TF), lambda e, f, idx: (e, 0, f)),
                pl.BlockSpec((1, FFN_TF, D_MODEL), lambda e, f, idx: (e, f, 0)),
            ],
            out_specs=pl.BlockSpec((1, CAP, D_MODEL), lambda e, f, idx: (e, 0, 0)),
            scratch_shapes=[
                pltpu.VMEM((H2_HALVES, CAP * SUBLANES, LANES), F32),
                pltpu.VMEM((CAP, D_MODEL), BF16),
                pltpu.VMEM((CAP, FFN_TF), BF16),
                pltpu.VMEM((CAP, D_MODEL), F32),
                pltpu.SemaphoreType.DMA(()),
            ],
        ),
        out_shape=jax.ShapeDtypeStruct((N_EXPERTS, CAP, D_MODEL), BF16),
        compiler_params=_cparams(("arbitrary", "arbitrary"), VMEM_LIMIT_BIG),
        name="ffn",
    )(idx, h2, g, w_gate, w_up, w_down)


CMB_CPT = 2
CMB_T = CMB_CPT * CHUNK
CMB_K = 256
CMB_GRAN = 16
CMB_FIRST = 64
CMB_MAIN = N_EXPERTS * CMB_FIRST
CMB_SIZES = (128, 64, 32, 16)
CMB_OVF = N_EXPERTS * sum(CMB_SIZES)
assert CMB_MAIN % CMB_K == 0 and CMB_OVF % CMB_K == 0
assert CMB_T + CMB_GRAN <= CMB_FIRST + sum(CMB_SIZES)


def _combine_kernel(base_ref, pos_ref, xmid_ref, mod_ref, g2_ref, b2_ref, y_hbm, o_ref,
                    stage_ref, ovf_ref, spread_ref, rcol_ref, acc_ref, sem, ovf_sem):
    i = pl.program_id(0)
    slot = i & 1
    y_rows = N_EXPERTS * CAP

    def windows(tile):
        starts, extra = [], []
        for e in range(N_EXPERTS):
            s0 = e * CAP + base_ref[e * (N_CHUNKS + 1) + tile * CMB_CPT]
            s1 = e * CAP + base_ref[e * (N_CHUNKS + 1) + (tile + 1) * CMB_CPT]
            start = jnp.minimum(s0 & -CMB_GRAN, y_rows - CMB_FIRST)
            starts.append(start)
            extra.append(jnp.maximum(s1 - (start + CMB_FIRST) + CMB_GRAN - 1, 0) & -CMB_GRAN)
        return starts, extra

    def stage_main(tile, slot_):
        for e, start in enumerate(windows(tile)[0]):
            pltpu.make_async_copy(
                y_hbm.at[pl.ds(pl.multiple_of(start, CMB_GRAN), CMB_FIRST)],
                stage_ref.at[slot_, pl.ds(e * CMB_FIRST, CMB_FIRST)], sem.at[slot_]).start()

    @pl.when(i == 0)
    def _():
        stage_ref[...] = jnp.zeros_like(stage_ref)
        ovf_ref[...] = jnp.zeros_like(ovf_ref)
        ee = lax.broadcasted_iota(jnp.int32, (LANES, CMB_MAIN), 0)
        cc = lax.broadcasted_iota(jnp.int32, (LANES, CMB_MAIN), 1)
        lo = ee * CMB_FIRST
        spread_ref[...] = ((cc >= lo) & (cc < lo + CMB_FIRST)).astype(F32).astype(BF16)
        c1 = lax.broadcasted_iota(jnp.int32, (SUBLANES, CMB_MAIN), 1)
        owner = jnp.zeros((SUBLANES, CMB_MAIN), jnp.int32)
        for e in range(1, N_EXPERTS):
            owner = owner + (c1 >= e * CMB_FIRST).astype(jnp.int32)
        rcol_ref[...] = (c1 - owner * CMB_FIRST).astype(F32)
        stage_main(0, 0)

    @pl.when(i + 1 < pl.num_programs(0))
    def _():
        stage_main(i + 1, 1 - slot)

    starts, extra = windows(i)
    pltpu.make_async_copy(y_hbm.at[pl.ds(0, CMB_MAIN)], stage_ref.at[slot], sem.at[slot]).wait()

    lane = lax.broadcasted_iota(jnp.int32, (1, LANES), 1)
    shift = jnp.zeros((1, LANES), jnp.int32)
    for e in range(N_EXPERTS):
        shift = jnp.where(lane == e, e * CAP - starts[e], shift)
    pos = pos_ref[...]
    rel = jnp.where(pos >= 0, pos + shift, -1)
    spread = jnp.dot(rel.astype(F32).astype(BF16), spread_ref[...], preferred_element_type=F32)
    for k in range(CMB_MAIN // CMB_K):
        ks = slice(k * CMB_K, (k + 1) * CMB_K)
        w = (spread[:, ks] == rcol_ref[0:1, ks]).astype(F32).astype(BF16)
        part = jnp.dot(w, stage_ref[slot, ks, :], preferred_element_type=F32)
        if k == 0:
            acc_ref[...] = part
        else:
            acc_ref[...] += part

    n_extra = extra[0]
    for e in range(1, N_EXPERTS):
        n_extra = n_extra + extra[e]

    @pl.when(n_extra > 0)
    def _():
        copies, rowidx = [], []
        ooff = jnp.int32(0)
        for e in range(N_EXPERTS):
            done = jnp.int32(0)
            for size in CMB_SIZES:
                pred = (extra[e] & size) != 0
                src = pl.multiple_of(starts[e] + CMB_FIRST + done, CMB_GRAN)
                dst = pl.multiple_of(ooff + done, CMB_GRAN)
                copies.append((pred, pltpu.make_async_copy(
                    y_hbm.at[pl.ds(src, size)], ovf_ref.at[pl.ds(dst, size)], ovf_sem)))
                done = done + jnp.where(pred, size, 0)
            r = rel[:, e:e + 1]
            rowidx.append(jnp.where(r >= CMB_FIRST, r - CMB_FIRST + ooff, -1))
            ooff = ooff + extra[e]
        for pred, cp in copies:
            @pl.when(pred)
            def _():
                cp.start()
        for pred, cp in copies:
            @pl.when(pred)
            def _():
                cp.wait()

        def body(k, carry):
            r0 = pl.multiple_of(k * CMB_K, CMB_K)
            col = lax.broadcasted_iota(jnp.int32, (CMB_T, CMB_K), 1) + r0
            w = jnp.zeros((CMB_T, CMB_K), F32)
            for e in range(N_EXPERTS):
                w = w + (rowidx[e] == col).astype(F32)
            acc_ref[...] += jnp.dot(w.astype(BF16), ovf_ref[pl.ds(r0, CMB_K), :],
                                    preferred_element_type=F32)
            return carry

        lax.fori_loop(0, (n_extra + CMB_K - 1) // CMB_K, body, 0)

    gate2 = mod_ref[5:6, :]
    o_ref[...] = _ln(DEEPNORM_ALPHA * xmid_ref[...] + gate2 * acc_ref[...]) * g2_ref[...] + b2_ref[...]


def _combine(base_flat, pos_te, xmid, mod6, g2, b2, y_flat):
    n = xmid.shape[0]
    row = lambda i, b: (i, 0)
    fixed = lambda i, b: (0, 0)
    return pl.pallas_call(
        _combine_kernel,
        grid_spec=pltpu.PrefetchScalarGridSpec(
            num_scalar_prefetch=1,
            grid=(n // CMB_T,),
            in_specs=[
                pl.BlockSpec((CMB_T, LANES), row),
                pl.BlockSpec((CMB_T, D_MODEL), row),
                pl.BlockSpec((N_MOD, D_MODEL), fixed),
                pl.BlockSpec((1, D_MODEL), fixed),
                pl.BlockSpec((1, D_MODEL), fixed),
                pl.BlockSpec(memory_space=pl.ANY),
            ],
            out_specs=pl.BlockSpec((CMB_T, D_MODEL), row),
            scratch_shapes=[
                pltpu.VMEM((2, CMB_MAIN, D_MODEL), BF16),
                pltpu.VMEM((CMB_OVF, D_MODEL), BF16),
                pltpu.VMEM((LANES, CMB_MAIN), BF16),
                pltpu.VMEM((SUBLANES, CMB_MAIN), F32),
                pltpu.VMEM((CMB_T, D_MODEL), F32),
                pltpu.SemaphoreType.DMA((2,)),
                pltpu.SemaphoreType.DMA(()),
            ],
        ),
        out_shape=jax.ShapeDtypeStruct((n, D_MODEL), F32),
        compiler_params=_cparams(("arbitrary",)),
        name="combine",
    )(base_flat, pos_te, xmid, mod6, g2, b2, y_flat)


def _rope_tables(n):
    rows = n // GRID_W
    inv = ROPE_THETA ** (-np.arange(0, ROPE_AXIS_DIM, 2, dtype=np.float64) / ROPE_AXIS_DIM)
    ang_r = np.arange(rows, dtype=np.float64)[:, None] * inv[None, :]
    ang_c = np.arange(GRID_W, dtype=np.float64)[:, None] * inv[None, :]
    zr, zc = np.zeros_like(ang_r), np.zeros_like(ang_c)
    cr, sr, cc, sc = np.cos(ang_r), np.sin(ang_r), np.cos(ang_c), np.sin(ang_c)
    trow = np.stack([np.concatenate(p, axis=1) for p in
                     ([cr, cr, zr, zr], [-sr, zr, zr, zr], [zr, sr, zr, zr])])
    tcol = np.stack([np.concatenate(p, axis=1) for p in
                     ([zc, zc, cc, cc], [zc, zc, -sc, zc], [zc, zc, zc, sc])])
    return jnp.asarray(trow, F32), jnp.asarray(tcol, F32)


def kernel(x, c, ctx, c_ctx, w_mod, b_mod, w_in, b_in, w_dw, b_dw, conv_ln_g, conv_ln_b, sink,
           w_out, b_out, ln1_g, ln1_b, w_router, w_gate, w_up, w_down, ln2_g, ln2_b):
    assert x.shape == (1, SEQ, D_MODEL) and ctx.shape == (1, CTX_LEN, D_MODEL)
    assert w_mod.shape[0] == DEPTH
    x2 = x[0]
    ctx2 = ctx[0]
    r1 = lambda a: a.reshape(1, -1)

    ct = jnp.stack([c[0], c_ctx], axis=1)
    mod = _mod(ct, w_mod[0], r1(b_mod[0]))
    mod6 = mod[0].reshape(N_MOD, D_MODEL)
    modc6 = mod[1].reshape(N_MOD, D_MODEL)

    w_in_bf = w_in[0].astype(BF16)
    w_out_bf = w_out[0].astype(BF16)
    trow, tcol = _rope_tables(SEQ)
    u, q, k, v = _in_proj(x2, mod6, w_in_bf, r1(b_in[0]), trow, tcol)
    kx, vx = _ctx_kv(ctx2, modc6, w_in_bf, r1(b_in[0]))
    a_attn = _attn(sink[0], q, k, v, kx, vx)
    wr = jnp.pad(w_router[0], ((0, 0), (0, LANES - N_EXPERTS)))
    a_conv = _conv(u, w_dw[0], r1(b_dw[0]), r1(conv_ln_g[0]), r1(conv_ln_b[0]))
    xmid, h2, aff_t = _out_proj(a_conv, a_attn, x2, mod6, w_out_bf, r1(b_out[0]),
                                r1(ln1_g[0]), r1(ln1_b[0]), wr)

    idx, g, base, pos = _select(aff_t.reshape(N_EXPERTS, N_CHUNKS, CHUNK))
    y = _ffn(idx.reshape(N_EXPERTS, CAP), h2, g, w_gate[0], w_up[0], w_down[0])

    base_flat = jnp.concatenate(
        [base[:, 0, :N_CHUNKS], jnp.full((N_EXPERTS, 1), CAP, jnp.int32)], axis=1).reshape(-1)
    pos_te = jnp.pad(pos.reshape(N_EXPERTS, SEQ).T, ((0, 0), (0, LANES - N_EXPERTS)),
                     constant_values=-1)
    out = _combine(base_flat, pos_te, xmid, mod6, r1(ln2_g[0]), r1(ln2_b[0]),
                   y.reshape(N_EXPERTS * CAP, D_MODEL))
    return out[None]
```

```python
import numpy as np

import jax
import jax.numpy as jnp
from jax import lax
from jax.experimental import pallas as pl
from jax.experimental.pallas import tpu as pltpu

D_MODEL = 2048
SEQ = 8192
GRID_W = 64
CTX_LEN = 256
HEAD_DIM = 128
N_Q_HEADS = 8
N_KV_HEADS = 2
Q_PER_KV = N_Q_HEADS // N_KV_HEADS
ATTN_WIDTH = N_Q_HEADS * HEAD_DIM
KV_WIDTH = N_KV_HEADS * HEAD_DIM
CONV_WIDTH = D_MODEL - ATTN_WIDTH
CONV_KSIZE = 31
WINDOW = 128
ROPE_THETA = 10000.0
ROPE_AXIS_DIM = HEAD_DIM // 2
N_EXPERTS = 16
EC_CAPACITY = 2
CAP = EC_CAPACITY * SEQ // N_EXPERTS
D_FF = 5632
N_MOD = 6
LN_EPS = 1e-5
NEG_INF = -1e30
DEPTH = 1
DEEPNORM_ALPHA = (2.0 * DEPTH) ** 0.25
Q_START = 2 * CONV_WIDTH
K_START = Q_START + ATTN_WIDTH
V_START = K_START + KV_WIDTH
IN_COLS = V_START + KV_WIDTH

LANES = 128
SUBLANES = 8
H2_HALVES = D_MODEL // (SUBLANES * LANES)
CHUNK = LANES
N_CHUNKS = SEQ // CHUNK
VMEM_LIMIT = 56 * 1024 * 1024
VMEM_LIMIT_BIG = 60 * 1024 * 1024

F32 = jnp.float32
BF16 = jnp.bfloat16


def _ln(xv):
    mu = jnp.mean(xv, axis=-1, keepdims=True)
    xc = xv - mu
    var = jnp.mean(xc * xc, axis=-1, keepdims=True)
    return xc * lax.rsqrt(var + LN_EPS)


def _cparams(sem, vmem=VMEM_LIMIT):
    return pltpu.CompilerParams(dimension_semantics=sem, vmem_limit_bytes=vmem)


MOD_TN = 1024
MOD_UNROLL = 4


def _mod_kernel(ct_ref, w_ref, b_ref, o_ref, s0_ref, s1_ref):
    @pl.when(pl.program_id(0) == 0)
    def _():
        ct = ct_ref[...]
        s = ct * jax.nn.sigmoid(ct)
        s0_ref[...] = jnp.broadcast_to(s[:, 0:1], (D_MODEL, LANES))
        s1_ref[...] = jnp.broadcast_to(s[:, 1:2], (D_MODEL, LANES))

    def body(kb, acc):
        k0 = pl.multiple_of(kb * SUBLANES, SUBLANES)
        s0 = s0_ref[pl.ds(k0, SUBLANES), :]
        s1 = s1_ref[pl.ds(k0, SUBLANES), :]
        a0, a1 = [], []
        for j in range(MOD_TN // LANES):
            w = w_ref[pl.ds(k0, SUBLANES), j * LANES:(j + 1) * LANES]
            a0.append(acc[0][j] + w * s0)
            a1.append(acc[1][j] + w * s1)
        return tuple(a0), tuple(a1)

    zeros = tuple(jnp.zeros((SUBLANES, LANES), F32) for _ in range(MOD_TN // LANES))
    acc0, acc1 = lax.fori_loop(0, D_MODEL // SUBLANES, body, (zeros, zeros), unroll=MOD_UNROLL)
    for j in range(MOD_TN // LANES):
        b = b_ref[:, j * LANES:(j + 1) * LANES]
        o_ref[0:1, j * LANES:(j + 1) * LANES] = jnp.sum(acc0[j], axis=0, keepdims=True) + b
        o_ref[1:2, j * LANES:(j + 1) * LANES] = jnp.sum(acc1[j], axis=0, keepdims=True) + b


def _mod(ct, w_mod, b_mod):
    n_out = N_MOD * D_MODEL
    return pl.pallas_call(
        _mod_kernel,
        grid=(n_out // MOD_TN,),
        in_specs=[
            pl.BlockSpec((D_MODEL, 2), lambda j: (0, 0)),
            pl.BlockSpec((D_MODEL, MOD_TN), lambda j: (0, j)),
            pl.BlockSpec((1, MOD_TN), lambda j: (0, j)),
        ],
        out_specs=pl.BlockSpec((2, MOD_TN), lambda j: (0, j)),
        out_shape=jax.ShapeDtypeStruct((2, n_out), F32),
        scratch_shapes=[pltpu.VMEM((D_MODEL, LANES), F32), pltpu.VMEM((D_MODEL, LANES), F32)],
        compiler_params=_cparams(("arbitrary",)),
        name="mod",
    )(ct, w_mod, b_mod)


IN_TM = 512
IN_SUB = 256
IN_TN = 512


def _rope(p, cos, sina, sinb):
    return (p * cos + pltpu.roll(p, HEAD_DIM - ROPE_AXIS_DIM // 2, axis=1) * sina
            + pltpu.roll(p, ROPE_AXIS_DIM // 2, axis=1) * sinb)


def _in_kernel(x_ref, mod_ref, w_ref, b_ref, trow_ref, tcol_ref, u_ref, q_ref, k_ref, v_ref):
    shift = mod_ref[0:1, :]
    scale = mod_ref[1:2, :]
    grid_rows = IN_SUB // GRID_W

    for sb in range(IN_TM // IN_SUB):
        rs = slice(sb * IN_SUB, (sb + 1) * IN_SUB)
        h = (_ln(x_ref[rs, :]) * (1.0 + scale) + shift).astype(BF16)

        def table(kind):
            by_row = jnp.concatenate(
                [jnp.broadcast_to(trow_ref[kind, sb * grid_rows + r:sb * grid_rows + r + 1, :],
                                  (GRID_W, HEAD_DIM)) for r in range(grid_rows)], axis=0)
            by_col = jnp.concatenate([tcol_ref[kind]] * grid_rows, axis=0)
            return by_row + by_col

        cos, sina, sinb = table(0), table(1), table(2)

        def proj(c0, width):
            return (jnp.dot(h, w_ref[:, c0:c0 + width], preferred_element_type=F32)
                    + b_ref[:, c0:c0 + width])

        for j in range(CONV_WIDTH // IN_TN):
            pv = proj(j * IN_TN, IN_TN)
            pg = proj(CONV_WIDTH + j * IN_TN, IN_TN)
            u_ref[rs, j * IN_TN:(j + 1) * IN_TN] = pv * jax.nn.sigmoid(pg)
        for j in range(ATTN_WIDTH // IN_TN):
            pq = proj(Q_START + j * IN_TN, IN_TN)
            for hh in range(IN_TN // HEAD_DIM):
                c0 = j * IN_TN + hh * HEAD_DIM
                q_ref[rs, c0:c0 + HEAD_DIM] = _rope(
                    pq[:, hh * HEAD_DIM:(hh + 1) * HEAD_DIM], cos, sina, sinb).astype(BF16)
        pk = proj(K_START, KV_WIDTH)
        for hh in range(N_KV_HEADS):
            k_ref[rs, hh * HEAD_DIM:(hh + 1) * HEAD_DIM] = _rope(
                pk[:, hh * HEAD_DIM:(hh + 1) * HEAD_DIM], cos, sina, sinb).astype(BF16)
        v_ref[rs, :] = proj(V_START, KV_WIDTH).astype(BF16)


def _in_proj(x2, mod6, w_in_bf, b_in, trow, tcol):
    n = x2.shape[0]
    row = lambda i: (i, 0)
    fixed = lambda i: (0, 0)
    return pl.pallas_call(
        _in_kernel,
        grid=(n // IN_TM,),
        in_specs=[
            pl.BlockSpec((IN_TM, D_MODEL), row),
            pl.BlockSpec((N_MOD, D_MODEL), fixed),
            pl.BlockSpec((D_MODEL, IN_COLS), fixed),
            pl.BlockSpec((1, IN_COLS), fixed),
            pl.BlockSpec((3, IN_TM // GRID_W, HEAD_DIM), lambda i: (0, i, 0)),
            pl.BlockSpec((3, GRID_W, HEAD_DIM), lambda i: (0, 0, 0)),
        ],
        out_specs=[
            pl.BlockSpec((IN_TM, CONV_WIDTH), row),
            pl.BlockSpec((IN_TM, ATTN_WIDTH), row),
            pl.BlockSpec((IN_TM, KV_WIDTH), row),
            pl.BlockSpec((IN_TM, KV_WIDTH), row),
        ],
        out_shape=[
            jax.ShapeDtypeStruct((n, CONV_WIDTH), F32),
            jax.ShapeDtypeStruct((n, ATTN_WIDTH), BF16),
            jax.ShapeDtypeStruct((n, KV_WIDTH), BF16),
            jax.ShapeDtypeStruct((n, KV_WIDTH), BF16),
        ],
        compiler_params=_cparams(("arbitrary",)),
        name="in_proj",
    )(x2, mod6, w_in_bf, b_in, trow, tcol)


def _ctx_kernel(x_ref, mod_ref, w_ref, b_ref, kc_ref, vc_ref):
    shift = mod_ref[0:1, :]
    scale = mod_ref[1:2, :]
    h = (_ln(x_ref[...]) * (1.0 + scale) + shift).astype(BF16)
    p = jnp.dot(h, w_ref[...], preferred_element_type=F32) + b_ref[...]
    kc_ref[...] = p[:, :KV_WIDTH].astype(BF16)
    vc_ref[...] = p[:, KV_WIDTH:].astype(BF16)


def _ctx_kv(ctx2, modc6, w_in_bf, b_in):
    kvw = 2 * KV_WIDTH
    fixed = lambda i: (0, 0)
    return pl.pallas_call(
        _ctx_kernel,
        grid=(1,),
        in_specs=[
            pl.BlockSpec((CTX_LEN, D_MODEL), fixed),
            pl.BlockSpec((N_MOD, D_MODEL), fixed),
            pl.BlockSpec((D_MODEL, kvw), lambda i: (0, K_START // kvw)),
            pl.BlockSpec((1, kvw), lambda i: (0, K_START // kvw)),
        ],
        out_specs=[pl.BlockSpec((CTX_LEN, KV_WIDTH), fixed), pl.BlockSpec((CTX_LEN, KV_WIDTH), fixed)],
        out_shape=[jax.ShapeDtypeStruct((CTX_LEN, KV_WIDTH), BF16),
                   jax.ShapeDtypeStruct((CTX_LEN, KV_WIDTH), BF16)],
        compiler_params=_cparams(("arbitrary",)),
        name="ctx_kv",
    )(ctx2, modc6, w_in_bf, b_in)


CONV_T = 512
CONV_HALO = 16
CONV_ROWS = 64


def _conv_fill(buf_ref, up_ref, uc_ref, un_ref, has_prev, has_next):
    for lg in range(CONV_WIDTH // LANES):
        ls = slice(lg * LANES, (lg + 1) * LANES)
        buf_ref[lg, 0:CONV_HALO, :] = jnp.where(has_prev, up_ref[:, ls], 0.0)
        buf_ref[lg, CONV_HALO:CONV_HALO + CONV_T, :] = uc_ref[:, ls]
        buf_ref[lg, CONV_HALO + CONV_T:, :] = jnp.where(has_next, un_ref[:, ls], 0.0)


def _conv_rows(buf_ref, w_ref, acc_ref, r0):
    off = CONV_HALO - CONV_KSIZE // 2
    span = CONV_ROWS + 2 * CONV_HALO
    for lg in range(CONV_WIDTH // LANES):
        ls = slice(lg * LANES, (lg + 1) * LANES)
        win = buf_ref.at[lg, pl.ds(r0, span)]
        acc = jnp.zeros((CONV_ROWS, LANES), F32)
        for t in range(CONV_KSIZE):
            acc = acc + win[pl.ds(off + t, CONV_ROWS), :] * w_ref[t:t + 1, ls]
        acc_ref[pl.ds(r0, CONV_ROWS), ls] = acc


def _conv_finish(acc_ref, bdw_ref, g_ref, b_ref):
    y = _ln(acc_ref[...] + bdw_ref[...]) * g_ref[...] + b_ref[...]
    return (y * jax.nn.sigmoid(y)).astype(BF16)


def _conv_kernel(up_ref, uc_ref, un_ref, w_ref, bdw_ref, g_ref, b_ref, o_ref, buf_ref, acc_ref):
    i = pl.program_id(0)
    last = pl.num_programs(0) - 1
    _conv_fill(buf_ref, up_ref, uc_ref, un_ref, i > 0, i < last)

    def body(r, carry):
        _conv_rows(buf_ref, w_ref, acc_ref, pl.multiple_of(r * CONV_ROWS, CONV_ROWS))
        return carry

    lax.fori_loop(0, CONV_T // CONV_ROWS, body, 0)
    o_ref[...] = _conv_finish(acc_ref, bdw_ref, g_ref, b_ref)


def _conv(u, w_dw, b_dw, ln_g, ln_b):
    n = u.shape[0]
    hb = CONV_T // CONV_HALO
    nhb = n // CONV_HALO
    fixed = lambda i: (0, 0)
    return pl.pallas_call(
        _conv_kernel,
        grid=(n // CONV_T,),
        in_specs=[
            pl.BlockSpec((CONV_HALO, CONV_WIDTH), lambda i: (jnp.maximum(i * hb - 1, 0), 0)),
            pl.BlockSpec((CONV_T, CONV_WIDTH), lambda i: (i, 0)),
            pl.BlockSpec((CONV_HALO, CONV_WIDTH), lambda i: (jnp.minimum((i + 1) * hb, nhb - 1), 0)),
            pl.BlockSpec((CONV_KSIZE, CONV_WIDTH), fixed),
            pl.BlockSpec((1, CONV_WIDTH), fixed),
            pl.BlockSpec((1, CONV_WIDTH), fixed),
            pl.BlockSpec((1, CONV_WIDTH), fixed),
        ],
        out_specs=pl.BlockSpec((CONV_T, CONV_WIDTH), lambda i: (i, 0)),
        out_shape=jax.ShapeDtypeStruct((n, CONV_WIDTH), BF16),
        scratch_shapes=[pltpu.VMEM((CONV_WIDTH // LANES, CONV_T + 2 * CONV_HALO, LANES), F32),
                        pltpu.VMEM((CONV_T, CONV_WIDTH), F32)],
        compiler_params=_cparams(("arbitrary",)),
        name="conv",
    )(u, u, u, w_dw, b_dw, ln_g, ln_b)


ATT_T = 128
ATT_NB = 2
assert ATT_T == WINDOW
LOG2E = 1.4426950408889634


def _attn_kernel(sink_ref, q_ref, kp_ref, kc_ref, kn_ref, vp_ref, vc_ref, vn_ref, kx_ref, vx_ref, o_ref,
                 bias_ref):
    i = pl.program_id(0)
    last = pl.num_programs(0) - 1
    scale = HEAD_DIM ** -0.5
    rows = Q_PER_KV * ATT_T

    @pl.when((i <= 1) | (i == last))
    def _():
        qi = lax.broadcasted_iota(jnp.int32, (rows, 3 * ATT_T), 0) & (ATT_T - 1)
        m = lax.broadcasted_iota(jnp.int32, (rows, 3 * ATT_T), 1)
        band = jnp.abs(m - qi - ATT_T) <= WINDOW
        for b in range(ATT_NB):
            kpos = (i * ATT_NB + b - 1) * ATT_T + m
            bias_ref[b] = jnp.where(band & (kpos >= 0) & (kpos < SEQ), 0.0, NEG_INF)

    hrow = jnp.right_shift(lax.broadcasted_iota(jnp.int32, (rows, 1), 0), ATT_T.bit_length() - 1)
    nt = (((1,), (1,)), ((), ()))
    def key_blocks(b, ls, p_ref, c_ref, n_ref):
        blocks = ([p_ref[:, ls]] + [c_ref[j * ATT_T:(j + 1) * ATT_T, ls] for j in range(ATT_NB)]
                  + [n_ref[:, ls]])
        return jnp.concatenate(blocks[b:b + 3], axis=0)

    def scores(b, g):
        rs = slice(b * ATT_T, (b + 1) * ATT_T)
        ls = slice(g * HEAD_DIM, (g + 1) * HEAD_DIM)
        qs = jnp.concatenate(
            [q_ref[rs, (g * Q_PER_KV + hh) * HEAD_DIM:(g * Q_PER_KV + hh + 1) * HEAD_DIM]
             for hh in range(Q_PER_KV)], axis=0)
        kw = key_blocks(b, ls, kp_ref, kc_ref, kn_ref)
        s_win = lax.dot_general(qs, kw, nt, preferred_element_type=F32) + bias_ref[b]
        s_ctx = lax.dot_general(qs, kx_ref[:, ls], nt, preferred_element_type=F32)
        return s_win, s_ctx

    def weights(g, s_win, s_ctx):
        s_sink = jnp.zeros((rows, 1), F32)
        for hh in range(Q_PER_KV):
            s_sink = jnp.where(hrow == hh, sink_ref[g * Q_PER_KV + hh], s_sink)
        raw_max = jnp.maximum(jnp.max(s_win, axis=-1, keepdims=True), jnp.max(s_ctx, axis=-1, keepdims=True))
        mx2 = jnp.maximum(raw_max * scale, s_sink) * LOG2E
        e_win = jnp.exp2(s_win * (scale * LOG2E) - mx2)
        e_ctx = jnp.exp2(s_ctx * (scale * LOG2E) - mx2)
        den = (jnp.sum(e_win, axis=-1, keepdims=True) + jnp.sum(e_ctx, axis=-1, keepdims=True)
               + jnp.exp2(s_sink * LOG2E - mx2))
        return e_win.astype(BF16), e_ctx.astype(BF16), den

    def values(b, g, e_win, e_ctx, den):
        rs = slice(b * ATT_T, (b + 1) * ATT_T)
        ls = slice(g * HEAD_DIM, (g + 1) * HEAD_DIM)
        vw = key_blocks(b, ls, vp_ref, vc_ref, vn_ref)
        o = (jnp.dot(e_ctx, vx_ref[:, ls], preferred_element_type=F32)
             + jnp.dot(e_win, vw, preferred_element_type=F32)) * (1.0 / den)
        for hh in range(Q_PER_KV):
            c0 = (g * Q_PER_KV + hh) * HEAD_DIM
            o_ref[rs, c0:c0 + HEAD_DIM] = o[hh * ATT_T:(hh + 1) * ATT_T, :].astype(BF16)

    chains = [(b, g) for b in range(ATT_NB) for g in range(N_KV_HEADS)]
    s_next = scores(*chains[0])
    for c, (b, g) in enumerate(chains):
        s_cur = s_next
        if c + 1 < len(chains):
            s_next = scores(*chains[c + 1])
        values(b, g, *weights(g, *s_cur))


def _attn(sink, q, k, v, kx, vx):
    n = q.shape[0]
    nb = n // ATT_T
    prev = lambda i, s: (jnp.maximum(i * ATT_NB - 1, 0), 0)
    cur = lambda i, s: (i, 0)
    nxt = lambda i, s: (jnp.minimum((i + 1) * ATT_NB, nb - 1), 0)
    fixed = lambda i, s: (0, 0)
    edge = lambda im: pl.BlockSpec((ATT_T, KV_WIDTH), im)
    own = pl.BlockSpec((ATT_NB * ATT_T, KV_WIDTH), cur)
    return pl.pallas_call(
        _attn_kernel,
        grid_spec=pltpu.PrefetchScalarGridSpec(
            num_scalar_prefetch=1,
            grid=(nb // ATT_NB,),
            in_specs=[
                pl.BlockSpec((ATT_NB * ATT_T, ATTN_WIDTH), cur),
                edge(prev), own, edge(nxt),
                edge(prev), own, edge(nxt),
                pl.BlockSpec((CTX_LEN, KV_WIDTH), fixed),
                pl.BlockSpec((CTX_LEN, KV_WIDTH), fixed),
            ],
            out_specs=pl.BlockSpec((ATT_NB * ATT_T, ATTN_WIDTH), cur),
            scratch_shapes=[pltpu.VMEM((ATT_NB, Q_PER_KV * ATT_T, 3 * ATT_T), F32)],
        ),
        out_shape=jax.ShapeDtypeStruct((n, ATTN_WIDTH), BF16),
        compiler_params=_cparams(("arbitrary",)),
        name="attn",
    )(sink, q, k, k, k, v, v, v, kx, vx)


OUT_TM = 512
OUT_SUB = 128


def _out_kernel(ac_ref, aa_ref, x_ref, mod_ref, w_ref, b_ref, g1_ref, b1_ref, wr_ref,
                xmid_ref, h2_ref, aff_ref, wrh_ref, wrl_ref):
    gate1 = mod_ref[2:3, :]
    n_sub = OUT_TM // OUT_SUB

    @pl.when(pl.program_id(0) == 0)
    def _():
        wr = wr_ref[...]
        hi = wr.astype(BF16)
        wrh_ref[...] = hi
        wrl_ref[...] = (wr - hi.astype(F32)).astype(BF16)

    def mix_of(sb):
        rs = slice(sb * OUT_SUB, (sb + 1) * OUT_SUB)
        return (jnp.dot(ac_ref[rs, :], w_ref[:CONV_WIDTH, :], preferred_element_type=F32)
                + jnp.dot(aa_ref[rs, :], w_ref[CONV_WIDTH:, :], preferred_element_type=F32) + b_ref[...])

    mix_next = mix_of(0)
    for sb in range(n_sub):
        rs = slice(sb * OUT_SUB, (sb + 1) * OUT_SUB)
        mix = mix_next
        if sb + 1 < n_sub:
            mix_next = mix_of(sb + 1)
        xmid = _ln(DEEPNORM_ALPHA * x_ref[rs, :] + gate1 * mix) * g1_ref[...] + b1_ref[...]
        xmid_ref[rs, :] = xmid
        h2 = _ln(xmid) * (1.0 + mod_ref[4:5, :]) + mod_ref[3:4, :]
        for kk in range(D_MODEL // LANES):
            hh, jj = divmod(kk, SUBLANES)
            h2_ref[hh, pl.ds(sb * OUT_SUB * SUBLANES + jj, OUT_SUB, stride=SUBLANES), :] = (
                h2[:, kk * LANES:(kk + 1) * LANES])
        h_hi = h2.astype(BF16)
        h_lo = (h2 - h_hi.astype(F32)).astype(BF16)
        logits = (jnp.dot(h_hi, wrh_ref[...], preferred_element_type=F32)
                  + jnp.dot(h_lo, wrh_ref[...], preferred_element_type=F32)
                  + jnp.dot(h_hi, wrl_ref[...], preferred_element_type=F32))
        logits = logits.T[:N_EXPERTS, :]
        mx = jnp.max(logits, axis=0, keepdims=True)
        ex = jnp.exp(logits - mx)
        aff_ref[:, rs] = ex / jnp.sum(ex, axis=0, keepdims=True)


def _out_proj(a_conv, a_attn, x2, mod6, w_out_bf, b_out, g1, b1, wr):
    n = x2.shape[0]
    row = lambda i: (i, 0)
    fixed = lambda i: (0, 0)
    return pl.pallas_call(
        _out_kernel,
        grid=(n // OUT_TM,),
        in_specs=[
            pl.BlockSpec((OUT_TM, CONV_WIDTH), row),
            pl.BlockSpec((OUT_TM, ATTN_WIDTH), row),
            pl.BlockSpec((OUT_TM, D_MODEL), row),
            pl.BlockSpec((N_MOD, D_MODEL), fixed),
            pl.BlockSpec((D_MODEL, D_MODEL), fixed),
            pl.BlockSpec((1, D_MODEL), fixed),
            pl.BlockSpec((1, D_MODEL), fixed),
            pl.BlockSpec((1, D_MODEL), fixed),
            pl.BlockSpec((D_MODEL, LANES), fixed),
        ],
        out_specs=[
            pl.BlockSpec((OUT_TM, D_MODEL), row),
            pl.BlockSpec((H2_HALVES, OUT_TM * SUBLANES, LANES), lambda i: (0, i, 0)),
            pl.BlockSpec((N_EXPERTS, OUT_TM), lambda i: (0, i)),
        ],
        out_shape=[
            jax.ShapeDtypeStruct((n, D_MODEL), F32),
            jax.ShapeDtypeStruct((H2_HALVES, n * SUBLANES, LANES), F32),
            jax.ShapeDtypeStruct((N_EXPERTS, n), F32),
        ],
        scratch_shapes=[pltpu.VMEM((D_MODEL, LANES), BF16), pltpu.VMEM((D_MODEL, LANES), BF16)],
        compiler_params=_cparams(("arbitrary",)),
        name="out_proj",
    )(a_conv, a_attn, x2, mod6, w_out_bf, b_out, g1, b1, wr)


def _split3(a):
    a1 = a.astype(BF16)
    r = a - a1.astype(F32)
    a2 = r.astype(BF16)
    a3 = (r - a2.astype(F32)).astype(BF16)
    return a1, a2, a3


def _select_kernel(aff_ref, idx_ref, g_ref, base_ref, pos_ref, sel_ref):
    e = pl.program_id(0)

    @pl.when(e == 0)
    def _():
        aff_all = aff_ref[...]

        def count(mask):
            c = jnp.sum(mask.astype(F32), axis=1, keepdims=True)
            return jnp.sum(c, axis=2, keepdims=True)

        def bit_step(k, thr_bits):
            cand = thr_bits | jnp.left_shift(jnp.int32(1), 30 - k)
            ge = aff_all >= pltpu.bitcast(cand, F32)
            return jnp.where(count(ge) >= CAP, cand, thr_bits)

        thr_bits = lax.fori_loop(0, 31, bit_step, jnp.zeros((N_EXPERTS, 1, 1), jnp.int32))
        thr = pltpu.bitcast(thr_bits, F32)
        gt = aff_all > thr
        eq = aff_all == thr
        need = CAP - count(gt)
        eq2 = eq.astype(F32).reshape(N_EXPERTS * N_CHUNKS, CHUNK)
        tri = (lax.broadcasted_iota(jnp.int32, (CHUNK, CHUNK), 0)
               <= lax.broadcasted_iota(jnp.int32, (CHUNK, CHUNK), 1)).astype(BF16)
        incl = jnp.dot(eq2.astype(BF16), tri, preferred_element_type=F32)
        tot = jnp.broadcast_to(incl[:, CHUNK - 1:CHUNK], (N_EXPERTS * N_CHUNKS, LANES))
        rr = lax.broadcasted_iota(jnp.int32, (N_EXPERTS * N_CHUNKS, N_EXPERTS * N_CHUNKS), 0)
        cc = lax.broadcasted_iota(jnp.int32, (N_EXPERTS * N_CHUNKS, N_EXPERTS * N_CHUNKS), 1)
        cshift = N_CHUNKS.bit_length() - 1
        low = ((jnp.right_shift(rr, cshift) == jnp.right_shift(cc, cshift)) & (cc < rr)).astype(BF16)
        before = jnp.dot(low, tot.astype(BF16), preferred_element_type=F32)
        rank = (before + incl - eq2).reshape(N_EXPERTS, N_CHUNKS, CHUNK)
        sel_ref[...] = (gt | (eq & (rank < need))).astype(F32)

    sel = sel_ref[e]
    aff = aff_ref[e]
    tri = (lax.broadcasted_iota(jnp.int32, (CHUNK, CHUNK), 0)
           <= lax.broadcasted_iota(jnp.int32, (CHUNK, CHUNK), 1)).astype(BF16)
    incl = jnp.dot(sel.astype(BF16), tri, preferred_element_type=F32)
    tot = jnp.broadcast_to(incl[:, CHUNK - 1:CHUNK], (N_CHUNKS, LANES))
    pad = lambda a: jnp.concatenate([a, jnp.zeros((LANES - N_CHUNKS, LANES), a.dtype)], axis=0)
    low = (lax.broadcasted_iota(jnp.int32, (N_CHUNKS, LANES), 1)
           < lax.broadcasted_iota(jnp.int32, (N_CHUNKS, LANES), 0)).astype(BF16)
    base = jnp.dot(low, pad(tot.astype(BF16)), preferred_element_type=F32)
    pos_ref[0] = jnp.where(sel > 0.0, base + incl - 1.0, -1.0).astype(jnp.int32)

    pick = (lax.broadcasted_iota(jnp.int32, (SUBLANES, LANES), 1) == 0).astype(BF16)
    nt = (((1,), (1,)), ((), ()))
    base_hi = jnp.floor(base * (1.0 / CHUNK))
    base_lo = base - base_hi * CHUNK
    row = lambda a: lax.dot_general(pick, pad(a.astype(BF16)), nt, preferred_element_type=F32)[0:1, :]
    base_row = row(base_hi) * CHUNK + row(base_lo)
    base_ref[0] = base_row.astype(jnp.int32)
    tot_row = row(tot)
    s_col = lax.broadcasted_iota(jnp.int32, (CAP, LANES), 0).astype(F32)
    owner = ((base_row <= s_col) & (s_col < base_row + tot_row)).astype(BF16)

    own = lambda a: jnp.dot(owner, pad(a.astype(BF16)), preferred_element_type=F32)
    incl_s = own(incl)
    base_s = own(base_hi) * CHUNK + own(base_lo)
    cvals = lax.broadcasted_iota(jnp.int32, (N_CHUNKS, LANES), 0).astype(F32)
    chunk_s = own(cvals)
    s_loc = lax.broadcasted_iota(jnp.int32, (CAP, LANES), 0).astype(F32) - base_s
    j_s = jnp.sum((incl_s <= s_loc).astype(F32), axis=1, keepdims=True)
    idx_ref[0] = (chunk_s[:, 0:1] * CHUNK + j_s).astype(jnp.int32)
    a1, a2, a3 = _split3(aff)
    aff_s = (own(a1) + own(a2)) + own(a3)
    lane = lax.broadcasted_iota(jnp.int32, (CAP, LANES), 1).astype(F32)
    g_ref[0] = jnp.sum(jnp.where(lane == j_s, aff_s, 0.0), axis=1, keepdims=True)


def _select(aff3):
    full = lambda e: (0, 0, 0)
    per = lambda e: (e, 0, 0)
    return pl.pallas_call(
        _select_kernel,
        grid=(N_EXPERTS,),
        in_specs=[pl.BlockSpec((N_EXPERTS, N_CHUNKS, CHUNK), full)],
        out_specs=[
            pl.BlockSpec((1, CAP, 1), per),
            pl.BlockSpec((1, CAP, 1), per),
            pl.BlockSpec((1, 1, LANES), per),
            pl.BlockSpec((1, N_CHUNKS, CHUNK), per),
        ],
        out_shape=[
            jax.ShapeDtypeStruct((N_EXPERTS, CAP, 1), jnp.int32),
            jax.ShapeDtypeStruct((N_EXPERTS, CAP, 1), F32),
            jax.ShapeDtypeStruct((N_EXPERTS, 1, LANES), jnp.int32),
            jax.ShapeDtypeStruct((N_EXPERTS, N_CHUNKS, CHUNK), jnp.int32),
        ],
        scratch_shapes=[pltpu.VMEM((N_EXPERTS, N_CHUNKS, CHUNK), F32)],
        compiler_params=_cparams(("arbitrary",)),
        name="select",
    )(aff3)


FFN_TF = 512
FFN_NF = D_FF // FFN_TF
FFN_TN = 256
FFN_GATHER_UNROLL = 8


def _ffn_kernel(idx_ref, h2_hbm, g_ref, wg_ref, wu_ref, wd_ref, y_ref, xs_ref, xb_ref, act_ref, acc_ref,
                sem):
    e = pl.program_id(0)
    f = pl.program_id(1)
    nf = pl.num_programs(1)

    def start_row(ee, s, priority=0):
        t8 = pl.multiple_of(idx_ref[ee, s] * SUBLANES, SUBLANES)
        s8 = pl.multiple_of(s * SUBLANES, SUBLANES)
        pltpu.make_async_copy(h2_hbm.at[:, pl.ds(t8, SUBLANES), :],
                              xs_ref.at[:, pl.ds(s8, SUBLANES), :], sem).start(priority=priority)

    def wait_rows():
        pltpu.make_async_copy(h2_hbm.at[:, pl.ds(0, CAP * SUBLANES), :], xs_ref, sem).wait()

    @pl.when((e == 0) & (f == 0))
    def _():
        def body(s, carry):
            start_row(0, s)
            return carry
        lax.fori_loop(0, CAP, body, 0, unroll=FFN_GATHER_UNROLL)

    nxt = jnp.where(e + 1 < N_EXPERTS, e + 1, 0)
    per_step = CAP // FFN_NF

    @pl.when(f == 0)
    def _():
        wait_rows()
        for kk in range(D_MODEL // LANES):
            hh, jj = divmod(kk, SUBLANES)
            xb_ref[:, kk * LANES:(kk + 1) * LANES] = (
                xs_ref[hh, pl.ds(jj, CAP, stride=SUBLANES), :].astype(BF16))
        for s in range(per_step * FFN_NF, CAP):
            start_row(nxt, s)

    for j in range(per_step):
        start_row(nxt, f * per_step + j, priority=j % 2)

    for c in range(FFN_TF // FFN_TN):
        cs = slice(c * FFN_TN, (c + 1) * FFN_TN)
        a = jnp.dot(xb_ref[...], wg_ref[0, :, cs].astype(BF16), preferred_element_type=F32)
        u = jnp.dot(xb_ref[...], wu_ref[0, :, cs].astype(BF16), preferred_element_type=F32)
        act_ref[:, cs] = (a * jax.nn.sigmoid(a) * u).astype(BF16)
    for n in range(D_MODEL // FFN_TN):
        ns = slice(n * FFN_TN, (n + 1) * FFN_TN)
        part = jnp.dot(act_ref[...], wd_ref[0, :, ns].astype(BF16), preferred_element_type=F32)
        acc_ref[:, ns] = jnp.where(f == 0, 0.0, acc_ref[:, ns]) + part

    @pl.when(f == nf - 1)
    def _():
        y_ref[0] = (acc_ref[...] * g_ref[0]).astype(BF16)

    @pl.when((e == N_EXPERTS - 1) & (f == nf - 1))
    def _():
        wait_rows()


def _ffn(idx, h2, g, w_gate, w_up, w_down):
    return pl.pallas_call(
        _ffn_kernel,
        grid_spec=pltpu.PrefetchScalarGridSpec(
            num_scalar_prefetch=1,
            grid=(N_EXPERTS, D_FF // FFN_TF),
            in_specs=[
                pl.BlockSpec(memory_space=pl.ANY),
                pl.BlockSpec((1, CAP, 1), lambda e, f, idx: (e, 0, 0)),
                pl.BlockSpec((1, D_MODEL, FFN_TF), lambda e, f, idx: (e, 0, f)),
                pl.BlockSpec((1, D_MODEL, FFN_TF), lambda e, f, idx: (e, 0, f)),
                pl.BlockSpec((1, FFN_TF, D_MODEL), lambda e, f, idx: (e, f, 0)),
            ],
            out_specs=pl.BlockSpec((1, CAP, D_MODEL), lambda e, f, idx: (e, 0, 0)),
            scratch_shapes=[
                pltpu.VMEM((H2_HALVES, CAP * SUBLANES, LANES), F32),
                pltpu.VMEM((CAP, D_MODEL), BF16),
                pltpu.VMEM((CAP, FFN_TF), BF16),
                pltpu.VMEM((CAP, D_MODEL), F32),
                pltpu.SemaphoreType.DMA(()),
            ],
        ),
        out_shape=jax.ShapeDtypeStruct((N_EXPERTS, CAP, D_MODEL), BF16),
        compiler_params=_cparams(("arbitrary", "arbitrary"), VMEM_LIMIT_BIG),
        name="ffn",
    )(idx, h2, g, w_gate, w_up, w_down)


CMB_CPT = 2
CMB_T = CMB_CPT * CHUNK
CMB_K = 256
CMB_GRAN = 16
CMB_FIRST = 64
CMB_MAIN = N_EXPERTS * CMB_FIRST
CMB_SIZES = (128, 64, 32, 16)
CMB_OVF = N_EXPERTS * sum(CMB_SIZES)
assert CMB_MAIN % CMB_K == 0 and CMB_OVF % CMB_K == 0
assert CMB_T + CMB_GRAN <= CMB_FIRST + sum(CMB_SIZES)


def _combine_kernel(base_ref, pos_ref, xmid_ref, mod_ref, g2_ref, b2_ref, y_hbm, o_ref,
                    stage_ref, ovf_ref, spread_ref, rcol_ref, acc_ref, sem, ovf_sem):
    i = pl.program_id(0)
    slot = i & 1
    y_rows = N_EXPERTS * CAP

    def windows(tile):
        starts, extra = [], []
        for e in range(N_EXPERTS):
            s0 = e * CAP + base_ref[e * (N_CHUNKS + 1) + tile * CMB_CPT]
            s1 = e * CAP + base_ref[e * (N_CHUNKS + 1) + (tile + 1) * CMB_CPT]
            start = jnp.minimum(s0 & -CMB_GRAN, y_rows - CMB_FIRST)
            starts.append(start)
            extra.append(jnp.maximum(s1 - (start + CMB_FIRST) + CMB_GRAN - 1, 0) & -CMB_GRAN)
        return starts, extra

    def stage_main(tile, slot_):
        for e, start in enumerate(windows(tile)[0]):
            pltpu.make_async_copy(
                y_hbm.at[pl.ds(pl.multiple_of(start, CMB_GRAN), CMB_FIRST)],
                stage_ref.at[slot_, pl.ds(e * CMB_FIRST, CMB_FIRST)], sem.at[slot_]).start()

    @pl.when(i == 0)
    def _():
        stage_ref[...] = jnp.zeros_like(stage_ref)
        ovf_ref[...] = jnp.zeros_like(ovf_ref)
        ee = lax.broadcasted_iota(jnp.int32, (LANES, CMB_MAIN), 0)
        cc = lax.broadcasted_iota(jnp.int32, (LANES, CMB_MAIN), 1)
        lo = ee * CMB_FIRST
        spread_ref[...] = ((cc >= lo) & (cc < lo + CMB_FIRST)).astype(F32).astype(BF16)
        c1 = lax.broadcasted_iota(jnp.int32, (SUBLANES, CMB_MAIN), 1)
        owner = jnp.zeros((SUBLANES, CMB_MAIN), jnp.int32)
        for e in range(1, N_EXPERTS):
            owner = owner + (c1 >= e * CMB_FIRST).astype(jnp.int32)
        rcol_ref[...] = (c1 - owner * CMB_FIRST).astype(F32)
        stage_main(0, 0)

    @pl.when(i + 1 < pl.num_programs(0))
    def _():
        stage_main(i + 1, 1 - slot)

    starts, extra = windows(i)
    pltpu.make_async_copy(y_hbm.at[pl.ds(0, CMB_MAIN)], stage_ref.at[slot], sem.at[slot]).wait()

    lane = lax.broadcasted_iota(jnp.int32, (1, LANES), 1)
    shift = jnp.zeros((1, LANES), jnp.int32)
    for e in range(N_EXPERTS):
        shift = jnp.where(lane == e, e * CAP - starts[e], shift)
    pos = pos_ref[...]
    rel = jnp.where(pos >= 0, pos + shift, -1)
    spread = jnp.dot(rel.astype(F32).astype(BF16), spread_ref[...], preferred_element_type=F32)
    for k in range(CMB_MAIN // CMB_K):
        ks = slice(k * CMB_K, (k + 1) * CMB_K)
        w = (spread[:, ks] == rcol_ref[0:1, ks]).astype(F32).astype(BF16)
        part = jnp.dot(w, stage_ref[slot, ks, :], preferred_element_type=F32)
        if k == 0:
            acc_ref[...] = part
        else:
            acc_ref[...] += part

    n_extra = extra[0]
    for e in range(1, N_EXPERTS):
        n_extra = n_extra + extra[e]

    @pl.when(n_extra > 0)
    def _():
        copies, rowidx = [], []
        ooff = jnp.int32(0)
        for e in range(N_EXPERTS):
            done = jnp.int32(0)
            for size in CMB_SIZES:
                pred = (extra[e] & size) != 0
                src = pl.multiple_of(starts[e] + CMB_FIRST + done, CMB_GRAN)
                dst = pl.multiple_of(ooff + done, CMB_GRAN)
                copies.append((pred, pltpu.make_async_copy(
                    y_hbm.at[pl.ds(src, size)], ovf_ref.at[pl.ds(dst, size)], ovf_sem)))
                done = done + jnp.where(pred, size, 0)
            r = rel[:, e:e + 1]
            rowidx.append(jnp.where(r >= CMB_FIRST, r - CMB_FIRST + ooff, -1))
            ooff = ooff + extra[e]
        for pred, cp in copies:
            @pl.when(pred)
            def _():
                cp.start()
        for pred, cp in copies:
            @pl.when(pred)
            def _():
                cp.wait()

        def body(k, carry):
            r0 = pl.multiple_of(k * CMB_K, CMB_K)
            col = lax.broadcasted_iota(jnp.int32, (CMB_T, CMB_K), 1) + r0
            w = jnp.zeros((CMB_T, CMB_K), F32)
            for e in range(N_EXPERTS):
                w = w + (rowidx[e] == col).astype(F32)
            acc_ref[...] += jnp.dot(w.astype(BF16), ovf_ref[pl.ds(r0, CMB_K), :],
                                    preferred_element_type=F32)
            return carry

        lax.fori_loop(0, (n_extra + CMB_K - 1) // CMB_K, body, 0)

    gate2 = mod_ref[5:6, :]
    o_ref[...] = _ln(DEEPNORM_ALPHA * xmid_ref[...] + gate2 * acc_ref[...]) * g2_ref[...] + b2_ref[...]


def _combine(base_flat, pos_te, xmid, mod6, g2, b2, y_flat):
    n = xmid.shape[0]
    row = lambda i, b: (i, 0)
    fixed = lambda i, b: (0, 0)
    return pl.pallas_call(
        _combine_kernel,
        grid_spec=pltpu.PrefetchScalarGridSpec(
            num_scalar_prefetch=1,
            grid=(n // CMB_T,),
            in_specs=[
                pl.BlockSpec((CMB_T, LANES), row),
                pl.BlockSpec((CMB_T, D_MODEL), row),
                pl.BlockSpec((N_MOD, D_MODEL), fixed),
                pl.BlockSpec((1, D_MODEL), fixed),
                pl.BlockSpec((1, D_MODEL), fixed),
                pl.BlockSpec(memory_space=pl.ANY),
            ],
            out_specs=pl.BlockSpec((CMB_T, D_MODEL), row),
            scratch_shapes=[
                pltpu.VMEM((2, CMB_MAIN, D_MODEL), BF16),
                pltpu.VMEM((CMB_OVF, D_MODEL), BF16),
                pltpu.VMEM((LANES, CMB_MAIN), BF16),
                pltpu.VMEM((SUBLANES, CMB_MAIN), F32),
                pltpu.VMEM((CMB_T, D_MODEL), F32),
                pltpu.SemaphoreType.DMA((2,)),
                pltpu.SemaphoreType.DMA(()),
            ],
        ),
        out_shape=jax.ShapeDtypeStruct((n, D_MODEL), F32),
        compiler_params=_cparams(("arbitrary",)),
        name="combine",
    )(base_flat, pos_te, xmid, mod6, g2, b2, y_flat)


def _rope_tables(n):
    rows = n // GRID_W
    inv = ROPE_THETA ** (-np.arange(0, ROPE_AXIS_DIM, 2, dtype=np.float64) / ROPE_AXIS_DIM)
    ang_r = np.arange(rows, dtype=np.float64)[:, None] * inv[None, :]
    ang_c = np.arange(GRID_W, dtype=np.float64)[:, None] * inv[None, :]
    zr, zc = np.zeros_like(ang_r), np.zeros_like(ang_c)
    cr, sr, cc, sc = np.cos(ang_r), np.sin(ang_r), np.cos(ang_c), np.sin(ang_c)
    trow = np.stack([np.concatenate(p, axis=1) for p in
                     ([cr, cr, zr, zr], [-sr, zr, zr, zr], [zr, sr, zr, zr])])
    tcol = np.stack([np.concatenate(p, axis=1) for p in
                     ([zc, zc, cc, cc], [zc, zc, -sc, zc], [zc, zc, zc, sc])])
    return jnp.asarray(trow, F32), jnp.asarray(tcol, F32)


def kernel(x, c, ctx, c_ctx, w_mod, b_mod, w_in, b_in, w_dw, b_dw, conv_ln_g, conv_ln_b, sink,
           w_out, b_out, ln1_g, ln1_b, w_router, w_gate, w_up, w_down, ln2_g, ln2_b):
    assert x.shape == (1, SEQ, D_MODEL) and ctx.shape == (1, CTX_LEN, D_MODEL)
    assert w_mod.shape[0] == DEPTH
    x2 = x[0]
    ctx2 = ctx[0]
    r1 = lambda a: a.reshape(1, -1)

    ct = jnp.stack([c[0], c_ctx], axis=1)
    mod = _mod(ct, w_mod[0], r1(b_mod[0]))
    mod6 = mod[0].reshape(N_MOD, D_MODEL)
    modc6 = mod[1].reshape(N_MOD, D_MODEL)

    w_in_bf = w_in[0].astype(BF16)
    w_out_bf = w_out[0].astype(BF16)
    trow, tcol = _rope_tables(SEQ)
    u, q, k, v = _in_proj(x2, mod6, w_in_bf, r1(b_in[0]), trow, tcol)
    kx, vx = _ctx_kv(ctx2, modc6, w_in_bf, r1(b_in[0]))
    a_attn = _attn(sink[0], q, k, v, kx, vx)
    wr = jnp.pad(w_router[0], ((0, 0), (0, LANES - N_EXPERTS)))
    a_conv = _conv(u, w_dw[0], r1(b_dw[0]), r1(conv_ln_g[0]), r1(conv_ln_b[0]))
    xmid, h2, aff_t = _out_proj(a_conv, a_attn, x2, mod6, w_out_bf, r1(b_out[0]),
                                r1(ln1_g[0]), r1(ln1_b[0]), wr)

    idx, g, base, pos = _select(aff_t.reshape(N_EXPERTS, N_CHUNKS, CHUNK))
    y = _ffn(idx.reshape(N_EXPERTS, CAP), h2, g, w_gate[0], w_up[0], w_down[0])

    base_flat = jnp.concatenate(
        [base[:, 0, :N_CHUNKS], jnp.full((N_EXPERTS, 1), CAP, jnp.int32)], axis=1).reshape(-1)
    pos_te = jnp.pad(pos.reshape(N_EXPERTS, SEQ).T, ((0, 0), (0, LANES - N_EXPERTS)),
                     constant_values=-1)
    out = _combine(base_flat, pos_te, xmid, mod6, r1(ln2_g[0]), r1(ln2_b[0]),
                   y.reshape(N_EXPERTS * CAP, D_MODEL))
    return out[None]
```
